```python
import math
import jax
import jax.numpy as jnp
from jax import lax
import numpy as np

D_MODEL = 2048
BATCH = 16
SEQ = 256
DEPTH = 2
DEC_BATCH = 2
DEC_SEQ = 1024
PAST_LEN = 512

GRID_W = 64
NA_HEADS = 8
NA_DH = 128
NA_W = NA_HEADS * NA_DH
NA_ROWS = 8
NA_COLS = 16
NA_QB = 16
NA_KB = NA_QB + NA_COLS
NA_ROW_BLOCK = 2
Q_BLOCK = 128
GLA_HEADS = 4
GLA_DK = 128
GLA_DV = 256
GLA_QK_W = GLA_HEADS * GLA_DK
GLA_V_W = GLA_HEADS * GLA_DV
GLA_LR = 16
GLA_TAU = 16.0
GLA_CHUNK = 64
ROPE_BASE = 10000.0
S5_W = 1024
S5_GS = 16
S5_GROUPS = S5_W // S5_GS
S5_N = 64
S5_DT_MIN = 0.001
S5_DT_MAX = 0.1
IN_W = 3 * NA_W + 2 * GLA_QK_W + 2 * GLA_V_W + 2 * GLA_LR + S5_W
N_EXPERTS = 64
TOP_K = 8
N_GROUPS = 8
TOPK_GROUPS = 4
F_EXPERT = 512
F_SHARED = 512
ROUTED_SCALE = 2.5
MOE_BLOCK = 128
EPS = 1e-6

kernel_name = 'hybrid_na_gla_s5_moe_prefix_step'


def rmsnorm(x, g):
    xf = x.astype(jnp.float32)
    y = xf * lax.rsqrt(jnp.mean(xf * xf, axis=-1, keepdims=True) + EPS)
    return (y * g.astype(jnp.float32)).astype(x.dtype)


def axial_rope(x):
    B, L, H, dr = x.shape
    half = dr // 2
    nf = half // 2
    t = jnp.arange(L)
    freqs = ROPE_BASE ** (-jnp.arange(nf, dtype=jnp.float32) / nf)

    def rot(xp, pos):
        ang = pos.astype(jnp.float32)[:, None] * freqs
        cos = jnp.cos(ang)[None, :, None, :]
        sin = jnp.sin(ang)[None, :, None, :]
        x1 = xp[..., :nf].astype(jnp.float32)
        x2 = xp[..., nf:].astype(jnp.float32)
        return jnp.concatenate([x1 * cos - x2 * sin, x1 * sin + x2 * cos], axis=-1)

    out = jnp.concatenate([rot(x[..., :half], t // GRID_W), rot(x[..., half:], t % GRID_W)], axis=-1)
    return out.astype(x.dtype)


def context_attention(q, k, v):
    B, L, H, Dh = q.shape
    nb = L // Q_BLOCK
    qb = jnp.moveaxis(q.reshape(B, nb, Q_BLOCK, H, Dh), 1, 0)

    def block(qi):
        s = jnp.einsum('bqhd,bkhd->bhqk', qi, k).astype(jnp.float32) * (Dh ** -0.5)
        p = jax.nn.softmax(s, axis=-1).astype(v.dtype)
        return jnp.einsum('bhqk,bkhd->bqhd', p, v)

    o = lax.map(block, qb)
    return jnp.moveaxis(o, 0, 1).reshape(B, L, H, Dh)


def neighborhood_attention(q, k, v, ck, cv, rpb):
    B, L, H, Dh = q.shape
    rows = L // GRID_W
    kr = min(NA_ROWS, rows)
    nqb = GRID_W // NA_QB
    n_rb = rows // NA_ROW_BLOCK
    r = jnp.arange(rows)
    key_rows = jnp.clip(r - kr // 2, 0, rows - kr)[:, None] + jnp.arange(kr)
    qcol = jnp.arange(GRID_W).reshape(nqb, NA_QB)
    win_start = jnp.clip(qcol - NA_COLS // 2, 0, GRID_W - NA_COLS)
    key_cols = jnp.clip(qcol[:, :1] - NA_COLS // 2, 0, GRID_W - NA_KB) + jnp.arange(NA_KB)
    kc3 = key_cols[:, None, :]
    valid = (kc3 >= win_start[:, :, None]) & (kc3 < win_start[:, :, None] + NA_COLS)
    col_idx = jnp.clip(kc3 - qcol[:, :, None] + NA_COLS - 1, 0, 2 * NA_COLS - 2)
    row_idx = (key_rows - r[:, None] + NA_ROWS - 1).reshape(n_rb, NA_ROW_BLOCK, kr)
    flat = (key_rows[:, None, :, None] * GRID_W + key_cols[None, :, None, :]).reshape(n_rb, NA_ROW_BLOCK, nqb, kr, NA_KB)
    qg = jnp.moveaxis(q.reshape(B, n_rb, NA_ROW_BLOCK, nqb, NA_QB, H, Dh), 1, 0)
    scale = Dh ** -0.5
    n_loc = kr * NA_KB

    def block(args):
        qi, fi, ri = args
        kg = k[:, fi]
        vg = v[:, fi]
        bias = rpb[:, ri[:, None, None, :, None], col_idx[None, :, :, None, :]]
        s_loc = jnp.einsum('brjqhd,brjakhd->bhrjqak', qi, kg).astype(jnp.float32) * scale + bias.astype(jnp.float32)
        s_loc = jnp.where(valid[:, :, None, :], s_loc, -jnp.inf)
        s_ctx = jnp.einsum('brjqhd,bchd->bhrjqc', qi, ck).astype(jnp.float32) * scale
        s = jnp.concatenate([s_loc.reshape(s_loc.shape[:5] + (n_loc,)), s_ctx], axis=-1)
        p = jax.nn.softmax(s, axis=-1).astype(v.dtype)
        p_loc = p[..., :n_loc].reshape(s_loc.shape)
        return (jnp.einsum('bhrjqak,brjakhd->brjqhd', p_loc, vg)
                + jnp.einsum('bhrjqc,bchd->brjqhd', p[..., n_loc:], cv))

    o = lax.map(block, (qg, flat, row_idx))
    return jnp.moveaxis(o, 0, 1).reshape(B, L, H, Dh)


def gla_chunked(q, k, v, log_a, s0):
    B, L, H, dk = q.shape
    dv = v.shape[-1]
    n = L // GLA_CHUNK

    def chunks(t):
        return t.astype(jnp.float32).reshape(B, n, GLA_CHUNK, H, t.shape[-1]).transpose(1, 0, 3, 2, 4)

    causal = jnp.tril(jnp.ones((GLA_CHUNK, GLA_CHUNK), bool))

    def step(s, inp):
        qi, ki, vi, ai = inp
        b = jnp.cumsum(ai, axis=2)
        diff = jnp.where(causal[:, :, None], b[:, :, :, None, :] - b[:, :, None, :, :], -jnp.inf)
        att = jnp.einsum('bhtd,bhsd,bhtsd->bhts', qi, ki, jnp.exp(diff))
        o = att @ vi + jnp.einsum('bhtd,bhdv->bhtv', qi * jnp.exp(b), s)
        b_last = b[:, :, -1:, :]
        s = (jnp.exp(b[:, :, -1, :])[..., None] * s
             + jnp.einsum('bhsd,bhsv->bhdv', ki * jnp.exp(b_last - b), vi))
        return s, o

    s_fin, o = lax.scan(step, s0.astype(jnp.float32), (chunks(q), chunks(k), chunks(v), chunks(log_a)))
    return o.transpose(1, 0, 3, 2, 4).reshape(B, L, H, dv), s_fin


def s5_scan(u, a_re, a_im, log_dt, b_re, b_im, c_re, c_im, h0):
    lam = lax.complex(a_re.astype(jnp.float32), a_im.astype(jnp.float32))
    lam_dt = lam * jnp.exp(log_dt.astype(jnp.float32))[:, None]
    a_bar = jnp.exp(lam_dt)
    b_bar = ((a_bar - 1.0) / lam)[:, :, None] * lax.complex(b_re.astype(jnp.float32), b_im.astype(jnp.float32))
    bu = jnp.einsum('gns,blgs->blgn', b_bar, u.astype(jnp.complex64))
    L = u.shape[1]

    def combine(e1, e2):
        return e1[0] * e2[0], e2[0] * e1[1] + e2[1]

    _, hs = lax.associative_scan(combine, (jnp.broadcast_to(a_bar, bu.shape), bu), axis=1)
    steps = jnp.arange(1, L + 1, dtype=jnp.float32)[:, None, None]
    hs = hs + jnp.exp(lam_dt * steps)[None] * h0[:, None]
    c = lax.complex(c_re.astype(jnp.float32), c_im.astype(jnp.float32))
    y = jnp.real(jnp.einsum('gsn,blgn->blgs', c, hs))
    return y, hs[:, -1]


def token_mixer(h, lp, cache):
    B, L, _ = h.shape
    sizes = (NA_W, NA_W, NA_W, GLA_QK_W, GLA_QK_W, GLA_V_W, GLA_V_W, 2 * GLA_LR)
    cuts = [sum(sizes[:i + 1]) for i in range(len(sizes))]
    na_q, na_k, na_v, g_q, g_k, g_v, g_g, g_lr, s_u = jnp.split(h @ lp['w_in'], cuts, axis=-1)
    heads = lambda t, n: t.reshape(B, L, n, -1)
    flip = lambda t: jnp.flip(t, axis=1)

    na_q, na_k, na_v = heads(na_q, NA_HEADS), heads(na_k, NA_HEADS), heads(na_v, NA_HEADS)
    if cache is None:
        o_na = context_attention(na_q, na_k, na_v)
    else:
        o_na = neighborhood_attention(na_q, na_k, na_v, cache[0], cache[1], lp['na_rpb'])

    g_q = heads(g_q, GLA_HEADS) * (GLA_DK ** -0.5)
    g_k = heads(g_k, GLA_HEADS)
    g_v = heads(g_v, GLA_HEADS)
    if cache is not None:
        g_q, g_k = axial_rope(g_q), axial_rope(g_k)
    lr = g_lr.reshape(B, L, 2, GLA_LR).astype(jnp.float32)
    log_a = jax.nn.log_sigmoid(jnp.einsum('bldr,drk->bldk', lr, lp['gla_wa'].astype(jnp.float32))
                               + lp['gla_ba'].astype(jnp.float32)) / GLA_TAU
    log_a = log_a.reshape(B, L, 2, GLA_HEADS, GLA_DK)
    if cache is None:
        s0 = jnp.zeros((B, 2, GLA_HEADS, GLA_DK, GLA_DV), jnp.float32)
    else:
        s0 = cache[2]
    o_f, s_f = gla_chunked(g_q, g_k, g_v, log_a[:, :, 0], s0[:, 0])
    o_b, s_b = gla_chunked(flip(g_q), flip(g_k), flip(g_v), flip(log_a[:, :, 1]), s0[:, 1])
    o_gla = o_f + flip(o_b)
    o_gla = o_gla * lax.rsqrt(jnp.mean(o_gla * o_gla, axis=-1, keepdims=True) + EPS)
    o_gla = (o_gla.reshape(B, L, GLA_V_W) * lp['gla_norm_g'].astype(jnp.float32)).astype(h.dtype) * jax.nn.silu(g_g)

    u = s_u.astype(jnp.float32).reshape(B, L, S5_GROUPS, S5_GS)
    if cache is None:
        h0 = jnp.zeros((B, 2, S5_GROUPS, S5_N), jnp.complex64)
    else:
        st = cache[3].astype(jnp.float32)
        h0 = lax.complex(st[..., 0], st[..., 1])

    def s5_dir(i, seq, h_init):
        return s5_scan(seq, lp['s5_a_re'][i], lp['s5_a_im'][i], lp['s5_log_dt'][i], lp['s5_b_re'][i],
                       lp['s5_b_im'][i], lp['s5_c_re'][i], lp['s5_c_im'][i], h_init)

    y_f, h_f = s5_dir(0, u, h0[:, 0])
    y_b, h_b = s5_dir(1, flip(u), h0[:, 1])
    y = (y_f + flip(y_b)).reshape(B, L, S5_W) + lp['s5_d'].astype(jnp.float32) * s_u.astype(jnp.float32)
    z = jax.nn.gelu(y).astype(h.dtype)
    z = z * jax.nn.sigmoid(z @ lp['s5_w_glu'])

    g_a, g_b, g_c = jnp.split(jax.nn.sigmoid(h @ lp['w_merge']), 3, axis=-1)
    merged = (g_a * (o_na.reshape(B, L, NA_W) @ lp['w_br_na'])
              + g_b * (o_gla @ lp['w_br_gla'])
              + g_c * (z @ lp['w_br_s5']))
    out = merged @ lp['w_out']
    if cache is None:
        h_fin = jnp.stack([h_f, h_b], axis=1)
        ctx_state = (na_k, na_v, jnp.stack([s_f, s_b], axis=1),
                     jnp.stack([jnp.real(h_fin), jnp.imag(h_fin)], axis=-1))
    else:
        ctx_state = None
    return out, ctx_state


def swiglu(x, wg, wu, wd):
    return (jax.nn.silu(x @ wg) * (x @ wu)) @ wd


def route(x, router_w, router_bias):
    T = x.shape[0]
    per = N_EXPERTS // N_GROUPS
    scores = jax.nn.sigmoid((x @ router_w).astype(jnp.float32))
    sel = scores + router_bias.astype(jnp.float32)
    grp = lax.top_k(sel.reshape(T, N_GROUPS, per), 2)[0].sum(-1)
    _, gidx = lax.top_k(grp, TOPK_GROUPS)
    gmask = jnp.any(gidx[:, :, None] == jnp.arange(N_GROUPS), axis=1)
    sel = jnp.where(jnp.repeat(gmask, per, axis=1), sel, -jnp.inf)
    _, idx = lax.top_k(sel, TOP_K)
    w = jnp.take_along_axis(scores, idx, axis=1)
    return idx, w / jnp.sum(w, axis=-1, keepdims=True) * ROUTED_SCALE


def routed_experts(x, idx, w, wg, wu, wd):
    T, D = x.shape
    A = T * TOP_K
    flat_e = idx.reshape(-1)
    order = jnp.argsort(flat_e)
    sorted_e = flat_e[order]
    counts = jnp.bincount(flat_e, length=N_EXPERTS)
    padded = (counts + MOE_BLOCK - 1) // MOE_BLOCK * MOE_BLOCK
    pad_end = jnp.cumsum(padded)
    slot = (pad_end - padded)[sorted_e] + jnp.arange(A) - (jnp.cumsum(counts) - counts)[sorted_e]
    n_blocks = -(-A // MOE_BLOCK) + N_EXPERTS
    slot_tok = jnp.zeros((n_blocks * MOE_BLOCK,), jnp.int32).at[slot].set((order // TOP_K).astype(jnp.int32))
    slot_w = jnp.zeros((n_blocks * MOE_BLOCK,), x.dtype).at[slot].set(w.reshape(-1)[order].astype(x.dtype))
    block_e = jnp.minimum(jnp.searchsorted(pad_end, jnp.arange(n_blocks) * MOE_BLOCK, side='right'), N_EXPERTS - 1)

    def block(args):
        e, tok, wt = args
        return swiglu(x[tok], wg[e], wu[e], wd[e]) * wt[:, None]

    out = lax.map(block, (block_e, slot_tok.reshape(n_blocks, MOE_BLOCK), slot_w.reshape(n_blocks, MOE_BLOCK)))
    return jnp.zeros_like(x).at[slot_tok].add(out.reshape(-1, D))


def moe(h, lp):
    B, L, D = h.shape
    x = h.reshape(B * L, D)
    idx, w = route(x, lp['router_w'], lp['router_bias'])
    y = (routed_experts(x, idx, w, lp['exp_wg'], lp['exp_wu'], lp['exp_wd'])
         + swiglu(x, lp['sh_wg'], lp['sh_wu'], lp['sh_wd']))
    return y.reshape(B, L, D)


def trunk_layer(x, cvec, lp, cache):
    mod = jax.nn.silu(cvec) @ lp['ada_w'] + lp['ada_b']
    sh1, sc1, g1, sh2, sc2, g2 = [m[:, None, :] for m in jnp.split(mod, 6, axis=-1)]
    h = rmsnorm(x, lp['norm1_g']) * (1 + sc1) + sh1
    mix, ctx_state = token_mixer(h, lp, cache)
    x = x + g1 * mix
    h = rmsnorm(x, lp['norm2_g']) * (1 + sc2) + sh2
    x = x + g2 * moe(h, lp)
    return x, ctx_state


def setup_inputs(seed: int = 0) -> dict:
    key = jax.random.key(seed)
    keys = iter(jax.random.split(key, 64))

    def nrm(shape, scale=1.0):
        return jax.random.normal(next(keys), shape, jnp.float32) * scale

    def gain(shape):
        return 1.0 + nrm(shape, 0.02)

    D = D_MODEL
    s5_shape = (DEPTH, 2, S5_GROUPS, S5_N)
    a_im0 = math.pi * jnp.arange(S5_N, dtype=jnp.float32)
    return {
        'x_prompt': nrm((BATCH, SEQ, D)),
        'x_sample': nrm((DEC_BATCH, DEC_SEQ, D)),
        'cache_na_k': nrm((DEC_BATCH, DEPTH, PAST_LEN, NA_HEADS, NA_DH)),
        'cache_na_v': nrm((DEC_BATCH, DEPTH, PAST_LEN, NA_HEADS, NA_DH)),
        'state_gla': nrm((DEC_BATCH, DEPTH, 2, GLA_HEADS, GLA_DK, GLA_DV)),
        'state_s5': nrm((DEC_BATCH, DEPTH, 2, S5_GROUPS, S5_N, 2), 0.5),
        'c': nrm((DEC_BATCH, D)),
        'c_ctx': nrm((D,)),
        'ada_w': nrm((DEPTH, D, 6 * D), 0.5 * D ** -0.5),
        'ada_b': nrm((DEPTH, 6 * D), 0.02),
        'norm1_g': gain((DEPTH, D)),
        'norm2_g': gain((DEPTH, D)),
        'w_in': nrm((DEPTH, D, IN_W), D ** -0.5),
        'na_rpb': nrm((DEPTH, NA_HEADS, 2 * NA_ROWS - 1, 2 * NA_COLS - 1), 0.02),
        'gla_wa': nrm((DEPTH, 2, GLA_LR, GLA_QK_W), GLA_LR ** -0.5),
        'gla_ba': nrm((DEPTH, 2, GLA_QK_W), 0.02),
        'gla_norm_g': gain((DEPTH, GLA_V_W)),
        's5_a_re': -0.5 + nrm(s5_shape, 0.01),
        's5_a_im': a_im0 + nrm(s5_shape, 0.01),
        's5_log_dt': jax.random.uniform(next(keys), (DEPTH, 2, S5_GROUPS), jnp.float32,
                                        math.log(S5_DT_MIN), math.log(S5_DT_MAX)),
        's5_b_re': nrm((DEPTH, 2, S5_GROUPS, S5_N, S5_GS), (2 * S5_GS) ** -0.5),
        's5_b_im': nrm((DEPTH, 2, S5_GROUPS, S5_N, S5_GS), (2 * S5_GS) ** -0.5),
        's5_c_re': nrm((DEPTH, 2, S5_GROUPS, S5_GS, S5_N), (2 * S5_N) ** -0.5),
        's5_c_im': nrm((DEPTH, 2, S5_GROUPS, S5_GS, S5_N), (2 * S5_N) ** -0.5),
        's5_d': nrm((DEPTH, S5_W)),
        's5_w_glu': nrm((DEPTH, S5_W, S5_W), S5_W ** -0.5),
        'w_br_na': nrm((DEPTH, NA_W, D), NA_W ** -0.5),
        'w_br_gla': nrm((DEPTH, GLA_V_W, D), GLA_V_W ** -0.5),
        'w_br_s5': nrm((DEPTH, S5_W, D), S5_W ** -0.5),
        'w_merge': nrm((DEPTH, D, 3 * D), D ** -0.5),
        'w_out': nrm((DEPTH, D, D), D ** -0.5),
        'router_w': nrm((DEPTH, D, N_EXPERTS), D ** -0.5),
        'router_bias': nrm((DEPTH, N_EXPERTS), 0.01),
        'exp_wg': nrm((DEPTH, N_EXPERTS, D, F_EXPERT), D ** -0.5),
        'exp_wu': nrm((DEPTH, N_EXPERTS, D, F_EXPERT), D ** -0.5),
        'exp_wd': nrm((DEPTH, N_EXPERTS, F_EXPERT, D), F_EXPERT ** -0.5),
        'sh_wg': nrm((DEPTH, D, F_SHARED), D ** -0.5),
        'sh_wu': nrm((DEPTH, D, F_SHARED), D ** -0.5),
        'sh_wd': nrm((DEPTH, F_SHARED, D), F_SHARED ** -0.5),
        'final_norm_g': gain((D,)),
    }


def reference(x_prompt, x_sample, cache_na_k, cache_na_v, state_gla, state_s5, c, c_ctx,
              ada_w, ada_b, norm1_g, norm2_g, w_in, na_rpb, gla_wa, gla_ba, gla_norm_g,
              s5_a_re, s5_a_im, s5_log_dt, s5_b_re, s5_b_im, s5_c_re, s5_c_im, s5_d, s5_w_glu,
              w_br_na, w_br_gla, w_br_s5, w_merge, w_out, router_w, router_bias,
              exp_wg, exp_wu, exp_wd, sh_wg, sh_wu, sh_wd, final_norm_g):
    xp, xs = x_prompt, x_sample
    ks, vs, glas, s5s = [], [], [], []
    for l in range(DEPTH):
        lp = dict(ada_w=ada_w[l], ada_b=ada_b[l], norm1_g=norm1_g[l], norm2_g=norm2_g[l], w_in=w_in[l],
                  na_rpb=na_rpb[l], gla_wa=gla_wa[l], gla_ba=gla_ba[l], gla_norm_g=gla_norm_g[l],
                  s5_a_re=s5_a_re[l], s5_a_im=s5_a_im[l], s5_log_dt=s5_log_dt[l], s5_b_re=s5_b_re[l],
                  s5_b_im=s5_b_im[l], s5_c_re=s5_c_re[l], s5_c_im=s5_c_im[l], s5_d=s5_d[l],
                  s5_w_glu=s5_w_glu[l], w_br_na=w_br_na[l], w_br_gla=w_br_gla[l], w_br_s5=w_br_s5[l],
                  w_merge=w_merge[l], w_out=w_out[l], router_w=router_w[l], router_bias=router_bias[l],
                  exp_wg=exp_wg[l], exp_wu=exp_wu[l], exp_wd=exp_wd[l],
                  sh_wg=sh_wg[l], sh_wu=sh_wu[l], sh_wd=sh_wd[l])
        xp, st = trunk_layer(xp, c_ctx[None, :], lp, None)
        ks.append(st[0])
        vs.append(st[1])
        glas.append(st[2])
        s5s.append(st[3])
        xs, _ = trunk_layer(xs, c, lp, (cache_na_k[:, l], cache_na_v[:, l], state_gla[:, l], state_s5[:, l]))
    y_prompt = rmsnorm(xp, final_norm_g)
    y_sample = rmsnorm(xs, final_norm_g)
    new_na_k = jnp.stack(ks, axis=1)
    new_na_v = jnp.stack(vs, axis=1)
    new_state_gla = jnp.stack(glas, axis=1)
    new_state_s5 = jnp.stack(s5s, axis=1)
    return (y_prompt, y_sample, new_na_k, new_na_v, new_state_gla, new_state_s5)
```

```python
import functools
import math

import jax
import jax.numpy as jnp
from jax import lax
from jax.experimental import pallas as pl
from jax.experimental.pallas import tpu as pltpu

F32 = jnp.float32
BF16 = jnp.bfloat16

D_MODEL = 2048
BATCH = 16
SEQ = 256
DEPTH = 2
DEC_BATCH = 2
DEC_SEQ = 1024
PAST_LEN = 512
GRID_W = 64
NA_HEADS = 8
NA_DH = 128
NA_W = NA_HEADS * NA_DH
NA_ROWS = 8
NA_COLS = 16
GLA_HEADS = 4
GLA_DK = 128
GLA_DV = 256
GLA_QK_W = GLA_HEADS * GLA_DK
GLA_V_W = GLA_HEADS * GLA_DV
GLA_LR = 16
GLA_TAU = 16.0
GLA_CHUNK = 64
GLA_SUB = 16
ROPE_BASE = 10000.0
S5_W = 1024
S5_GS = 16
S5_GROUPS = S5_W // S5_GS
S5_N = 64
S5_T = 16
IN_W = 3 * NA_W + 2 * GLA_QK_W + 2 * GLA_V_W + 2 * GLA_LR + S5_W
IN_MAIN = 3 * NA_W + 2 * GLA_QK_W + 2 * GLA_V_W
N_EXPERTS = 64
TOP_K = 8
N_GROUPS = 8
TOPK_GROUPS = 4
F_EXPERT = 512
F_SHARED = 512
ROUTED_SCALE = 2.5
EPS = 1e-6

T_CTX = BATCH * SEQ
T_LAT = DEC_BATCH * DEC_SEQ
T_ALL = T_CTX + T_LAT
N_MOD = 1 + DEC_BATCH
MOD_ROWS = 8

TM = 512
MOE_MB = 256
MOE_NB = T_ALL * TOP_K // MOE_MB + N_EXPERTS
MOE_SLOTS = MOE_NB * MOE_MB
CMB_TM = 128
NEG = -1e30

VMEM_LIMIT = 56 * 1024 * 1024

NT = (((1,), (1,)), ((), ()))
TN = (((0,), (0,)), ((), ()))


def _cparams(n_axes):
    return pltpu.CompilerParams(dimension_semantics=("arbitrary",) * n_axes,
                                vmem_limit_bytes=VMEM_LIMIT)


def _group_of_tile(i, tm):
    row = i * tm
    return jnp.where(row < T_CTX, 0, 1 + (row - T_CTX) // DEC_SEQ)


def _silu(x):
    return x * jax.nn.sigmoid(x)


def _gelu_tanh(x):
    return 0.5 * x * (1.0 + jnp.tanh(math.sqrt(2.0 / math.pi) * (x + 0.044715 * (x * x * x))))


def _ada_kernel(c_ref, w_ref, b_ref, o_ref):
    s = _silu(c_ref[...]).astype(BF16)
    o_ref[...] = jnp.dot(s, w_ref[...].astype(BF16), preferred_element_type=F32) + b_ref[...]


def _ada(cvec, ada_w, ada_b):
    tn = 1024
    return pl.pallas_call(
        _ada_kernel,
        grid=(DEPTH, 6 * D_MODEL // tn),
        in_specs=[pl.BlockSpec((MOD_ROWS, D_MODEL), lambda l, j: (0, 0)),
                  pl.BlockSpec((None, D_MODEL, tn), lambda l, j: (l, 0, j)),
                  pl.BlockSpec((None, 1, tn), lambda l, j: (l, 0, j))],
        out_specs=pl.BlockSpec((None, MOD_ROWS, tn), lambda l, j: (l, 0, j)),
        out_shape=jax.ShapeDtypeStruct((DEPTH, MOD_ROWS, 6 * D_MODEL), F32),
        compiler_params=_cparams(2), name="ada",
    )(cvec, ada_w, ada_b.reshape(DEPTH, 1, 6 * D_MODEL))


def _norm_kernel(x_ref, g_ref, *rest, rows):
    o_ref = rest[-1]
    x = x_ref[...]
    y = x * lax.rsqrt(jnp.mean(x * x, axis=-1, keepdims=True) + EPS) * g_ref[...]
    if rows is not None:
        mod_ref = rest[0]
        y = y * (1.0 + mod_ref[rows[1]:rows[1] + 1, :]) + mod_ref[rows[0]:rows[0] + 1, :]
    o_ref[...] = y.astype(o_ref.dtype)


def _norm(x, g, mod, rows, out_dtype):
    tm = 256
    in_specs = [pl.BlockSpec((tm, D_MODEL), lambda i: (i, 0)),
                pl.BlockSpec((1, D_MODEL), lambda i: (0, 0))]
    args = [x, g.reshape(1, D_MODEL)]
    if rows is not None:
        in_specs.append(pl.BlockSpec((None, 6, D_MODEL), lambda i: (_group_of_tile(i, tm), 0, 0)))
        args.append(mod)
    return pl.pallas_call(
        functools.partial(_norm_kernel, rows=rows),
        grid=(T_ALL // tm,),
        in_specs=in_specs,
        out_specs=pl.BlockSpec((tm, D_MODEL), lambda i: (i, 0)),
        out_shape=jax.ShapeDtypeStruct((T_ALL, D_MODEL), out_dtype),
        compiler_params=_cparams(1), name="norm",
    )(*args)


def _mm_kernel(x_ref, w_ref, o_ref, wbf_ref, *, act):
    @pl.when(pl.program_id(1) == 0)
    def _():
        wbf_ref[...] = w_ref[...].astype(BF16)

    acc = jnp.dot(x_ref[...].astype(BF16), wbf_ref[...], preferred_element_type=F32)
    if act == "sigmoid":
        acc = jax.nn.sigmoid(acc)
    o_ref[...] = acc.astype(o_ref.dtype)


def _mm(x, w, layer, tn, act=None, out_dtype=F32, n=None):
    k = x.shape[1]
    n = w.shape[-1] if n is None else n
    if layer is None:
        w_spec = pl.BlockSpec((k, tn), lambda j, i: (0, j))
    else:
        w_spec = pl.BlockSpec((None, k, tn), lambda j, i: (layer, 0, j))
    return pl.pallas_call(
        functools.partial(_mm_kernel, act=act),
        grid=(n // tn, T_ALL // TM),
        in_specs=[pl.BlockSpec((TM, k), lambda j, i: (i, 0)), w_spec],
        out_specs=pl.BlockSpec((TM, tn), lambda j, i: (i, j)),
        out_shape=jax.ShapeDtypeStruct((T_ALL, n), out_dtype),
        scratch_shapes=[pltpu.VMEM((k, tn), BF16)],
        compiler_params=_cparams(2), name="mm",
    )(x, w)


def _ctx_attn_kernel(q_ref, k_ref, v_ref, o_ref):
    scale = NA_DH ** -0.5
    for h in range(NA_HEADS):
        sl = slice(h * NA_DH, (h + 1) * NA_DH)
        q = q_ref[:, sl].astype(BF16)
        k = k_ref[:, sl].astype(BF16)
        v = v_ref[:, sl].astype(BF16)
        s = lax.dot_general(q, k, NT, preferred_element_type=F32) * scale
        p = jnp.exp(s - jnp.max(s, axis=-1, keepdims=True))
        o = jnp.dot(p.astype(BF16), v, preferred_element_type=F32) / jnp.sum(p, axis=-1, keepdims=True)
        o_ref[:, sl] = o.astype(o_ref.dtype)


def _ctx_attn(u):
    spec = lambda cb: pl.BlockSpec((SEQ, NA_W), lambda b: (b, cb))
    return pl.pallas_call(
        _ctx_attn_kernel,
        grid=(BATCH,),
        in_specs=[spec(0), spec(1), spec(2)],
        out_specs=pl.BlockSpec((SEQ, NA_W), lambda b: (b, 0)),
        out_shape=jax.ShapeDtypeStruct((T_CTX, NA_W), BF16),
        compiler_params=_cparams(1), name="ctx_attn",
    )(u, u, u)


NA_GRID_ROWS = DEC_SEQ // GRID_W
NA_KR = min(NA_ROWS, NA_GRID_ROWS)
NA_LOC = NA_KR * GRID_W


def _na_bias(rpb):
    r = jnp.arange(NA_GRID_ROWS)
    start = jnp.clip(r - NA_KR // 2, 0, NA_GRID_ROWS - NA_KR)
    row_idx = start[:, None] + jnp.arange(NA_KR)[None, :] - r[:, None] + NA_ROWS - 1
    qc = jnp.arange(GRID_W)[:, None]
    kc = jnp.arange(GRID_W)[None, :]
    win = jnp.clip(qc - NA_COLS // 2, 0, GRID_W - NA_COLS)
    valid = (kc >= win) & (kc < win + NA_COLS)
    col_idx = jnp.clip(kc - qc + NA_COLS - 1, 0, 2 * NA_COLS - 2)
    bias = rpb[:, row_idx[:, :, None, None], col_idx[None, None, :, :]]
    bias = jnp.where(valid[None, None, None], bias, NEG)
    return bias.transpose(0, 1, 3, 2, 4).reshape(NA_HEADS, NA_GRID_ROWS, GRID_W, NA_LOC)


def _na_kernel(q_ref, k_ref, v_ref, ck_ref, cv_ref, bias_ref, o_ref):
    scale = NA_DH ** -0.5
    r = pl.program_id(2)
    start = pl.multiple_of(jnp.clip(r - NA_KR // 2, 0, NA_GRID_ROWS - NA_KR) * GRID_W, GRID_W)
    q = q_ref[...].astype(BF16)
    kl = k_ref[pl.ds(start, NA_LOC), :].astype(BF16)
    vl = v_ref[pl.ds(start, NA_LOC), :].astype(BF16)
    s1 = lax.dot_general(q, kl, NT, preferred_element_type=F32) * scale + bias_ref[...]
    s2 = lax.dot_general(q, ck_ref[...].astype(BF16), NT, preferred_element_type=F32) * scale
    m = jnp.maximum(jnp.max(s1, axis=-1, keepdims=True), jnp.max(s2, axis=-1, keepdims=True))
    p1 = jnp.exp(s1 - m)
    p2 = jnp.exp(s2 - m)
    den = jnp.sum(p1, axis=-1, keepdims=True) + jnp.sum(p2, axis=-1, keepdims=True)
    o = (jnp.dot(p1.astype(BF16), vl, preferred_element_type=F32)
         + jnp.dot(p2.astype(BF16), cv_ref[...].astype(BF16), preferred_element_type=F32))
    o_ref[...] = (o / den).astype(o_ref.dtype)


def _na_attn(u, cache_k, cache_v, bias, layer):
    lat_rb = T_CTX // GRID_W
    lat_sb = T_CTX // DEC_SEQ
    ck = cache_k.reshape(DEC_BATCH, DEPTH, PAST_LEN, NA_W)
    cv = cache_v.reshape(DEC_BATCH, DEPTH, PAST_LEN, NA_W)
    kv_spec = lambda cb: pl.BlockSpec((DEC_SEQ, NA_DH), lambda b, h, r: (lat_sb + b, cb * NA_HEADS + h))
    c_spec = pl.BlockSpec((None, None, PAST_LEN, NA_DH), lambda b, h, r: (b, layer, 0, h))
    return pl.pallas_call(
        _na_kernel,
        grid=(DEC_BATCH, NA_HEADS, NA_GRID_ROWS),
        in_specs=[pl.BlockSpec((GRID_W, NA_DH), lambda b, h, r: (lat_rb + b * NA_GRID_ROWS + r, h)),
                  kv_spec(1), kv_spec(2), c_spec, c_spec,
                  pl.BlockSpec((None, None, GRID_W, NA_LOC), lambda b, h, r: (h, r, 0, 0))],
        out_specs=pl.BlockSpec((GRID_W, NA_DH), lambda b, h, r: (b * NA_GRID_ROWS + r, h)),
        out_shape=jax.ShapeDtypeStruct((T_LAT, NA_W), BF16),
        compiler_params=_cparams(3), name="na_attn",
    )(u, u, u, ck, cv, bias)


def _rope_tables():
    half = GLA_DK // 2
    nf = half // 2
    t = jnp.arange(DEC_SEQ)
    freqs = ROPE_BASE ** (-jnp.arange(nf, dtype=F32) / nf)
    ang_r = (t // GRID_W).astype(F32)[:, None] * freqs
    ang_c = (t % GRID_W).astype(F32)[:, None] * freqs
    cos = jnp.concatenate([jnp.cos(ang_r), jnp.cos(ang_r), jnp.cos(ang_c), jnp.cos(ang_c)], axis=-1)
    sin = jnp.concatenate([-jnp.sin(ang_r), jnp.sin(ang_r), -jnp.sin(ang_c), jnp.sin(ang_c)], axis=-1)
    return cos, sin


def _split_bf16(x):
    hi = x.astype(BF16)
    return hi, (x - hi.astype(F32)).astype(BF16)


def _gla_kernel(q_ref, k_ref, v_ref, gg_ref, lr_ref, wa_ref, ba_ref, ng_ref, cos_ref, sin_ref, s0_ref,
                o_ref, sfin_ref, qs, ks, las, o_acc, st, *, seq, rope):
    nc = seq // GLA_CHUNK
    nq = GLA_DK // 4

    def rot(x):
        lane = lax.broadcasted_iota(jnp.int32, x.shape, 1)
        partner = jnp.where((lane % (2 * nq)) < nq,
                            pltpu.roll(x, GLA_DK - nq, axis=1), pltpu.roll(x, nq, axis=1))
        return x * cos_ref[...] + partner * sin_ref[...]

    q = q_ref[...] * (GLA_DK ** -0.5)
    k = k_ref[...]
    if rope:
        q = rot(q)
        k = rot(k)
    qs[...] = q
    ks[...] = k
    lr = lr_ref[...]
    for d in range(2):
        z = jnp.dot(lr[:, d * GLA_LR:(d + 1) * GLA_LR].astype(BF16), wa_ref[d].astype(BF16),
                    preferred_element_type=F32) + ba_ref[d:d + 1, :]
        las[d] = -(jnp.maximum(-z, 0.0) + jnp.log1p(jnp.exp(-jnp.abs(z)))) / GLA_TAU
        st[d] = s0_ref[d]

    row = lax.broadcasted_iota(jnp.int32, (GLA_CHUNK, GLA_CHUNK), 0)
    col = lax.broadcasted_iota(jnp.int32, (GLA_CHUNK, GLA_CHUNK), 1)
    key_row = lax.broadcasted_iota(jnp.int32, (GLA_CHUNK, 1), 0)

    def chunk(c, d):
        rev = d == 1
        rows = pl.ds(pl.multiple_of(c * GLA_CHUNK, GLA_CHUNK), GLA_CHUNK)
        qc = qs[rows, :]
        kc = ks[rows, :]
        vc = v_ref[rows, :].astype(BF16)
        la = las[d, rows, :]
        causal = (col >= row) if rev else (col <= row)
        tri = jnp.where(causal, 1.0, 0.0).astype(BF16)
        la_hi, la_lo = _split_bf16(la)
        b = (jnp.dot(tri, la_hi, preferred_element_type=F32)
             + jnp.dot(tri, la_lo, preferred_element_type=F32))
        bex = b - la
        b_last = b[0:1, :] if rev else b[GLA_CHUNK - 1:GLA_CHUNK, :]
        blocks = []
        for i in range(GLA_CHUNK // GLA_SUB):
            lo, hi = i * GLA_SUB, (i + 1) * GLA_SUB
            ref = bex[hi - 1:hi, :] if rev else bex[lo:lo + 1, :]
            qt = (qc[lo:hi, :] * jnp.exp(b[lo:hi, :] - ref)).astype(BF16)
            allowed = (key_row >= lo) if rev else (key_row < hi)
            kt = (kc * jnp.exp(jnp.where(allowed, ref - b, -jnp.inf))).astype(BF16)
            blocks.append(lax.dot_general(qt, kt, NT, preferred_element_type=F32))
        att = jnp.where(causal, jnp.concatenate(blocks, axis=0), 0.0)
        s_t = st[d]
        o = (jnp.dot(att.astype(BF16), vc, preferred_element_type=F32)
             + lax.dot_general((qc * jnp.exp(b)).astype(BF16), s_t.astype(BF16), NT,
                               preferred_element_type=F32))
        khat = (kc * jnp.exp(b_last - b)).astype(BF16)
        st[d] = s_t * jnp.exp(b_last) + lax.dot_general(vc, khat, TN, preferred_element_type=F32)
        if rev:
            o_acc[rows, :] = o_acc[rows, :] + o
        else:
            o_acc[rows, :] = o

    def fwd(c, carry):
        chunk(c, 0)
        return carry

    def bwd(c, carry):
        chunk(nc - 1 - c, 1)
        return carry

    lax.fori_loop(0, nc, fwd, 0)
    lax.fori_loop(0, nc, bwd, 0)
    sfin_ref[...] = st[...]
    o = o_acc[...]
    o = o * lax.rsqrt(jnp.mean(o * o, axis=-1, keepdims=True) + EPS) * ng_ref[...]
    o_ref[...] = (o * _silu(gg_ref[...])).astype(o_ref.dtype)


def _gla(u, lr, wa, ba, norm_g, s0_t, cos, sin, layer, *, latent):
    seq, nb, first = (DEC_SEQ, DEC_BATCH, T_CTX // DEC_SEQ) if latent else (SEQ, BATCH, 0)
    qk_cb = 3 * NA_W // GLA_DK
    v_cb = (3 * NA_W + 2 * GLA_QK_W) // GLA_DV
    row = lambda w, cb: pl.BlockSpec((seq, w), lambda b, h: (first + b, cb + h))
    return pl.pallas_call(
        functools.partial(_gla_kernel, seq=seq, rope=latent),
        grid=(nb, GLA_HEADS),
        in_specs=[row(GLA_DK, qk_cb), row(GLA_DK, qk_cb + GLA_HEADS), row(GLA_DV, v_cb),
                  row(GLA_DV, v_cb + GLA_HEADS),
                  pl.BlockSpec((seq, 2 * GLA_LR), lambda b, h: (first + b, 0)),
                  pl.BlockSpec((None, 2, GLA_LR, GLA_DK), lambda b, h: (layer, 0, 0, h)),
                  pl.BlockSpec((None, 2, GLA_DK), lambda b, h: (layer, 0, h)),
                  pl.BlockSpec((None, 1, GLA_DV), lambda b, h: (layer, 0, h)),
                  pl.BlockSpec((seq, GLA_DK), lambda b, h: (0, 0)),
                  pl.BlockSpec((seq, GLA_DK), lambda b, h: (0, 0)),
                  pl.BlockSpec((None, 2, None, GLA_DV, GLA_DK), lambda b, h: (b, 0, h, 0, 0))],
        out_specs=[pl.BlockSpec((seq, GLA_DV), lambda b, h: (b, h)),
                   pl.BlockSpec((None, 2, None, GLA_DV, GLA_DK), lambda b, h: (b, 0, h, 0, 0))],
        out_shape=[jax.ShapeDtypeStruct((nb * seq, GLA_V_W), BF16),
                   jax.ShapeDtypeStruct((nb, 2, GLA_HEADS, GLA_DV, GLA_DK), F32)],
        scratch_shapes=[pltpu.VMEM((seq, GLA_DK), F32), pltpu.VMEM((seq, GLA_DK), F32),
                        pltpu.VMEM((2, seq, GLA_DK), F32), pltpu.VMEM((seq, GLA_DV), F32),
                        pltpu.VMEM((2, GLA_DV, GLA_DK), F32)],
        compiler_params=_cparams(2), name="gla",
    )(u, u, u, u, lr, wa, ba, norm_g.reshape(DEPTH, 1, GLA_V_W), cos[:seq], sin[:seq], s0_t)


def _s5_operators(a_re, a_im, log_dt, b_re, b_im, c_re, c_im):
    hp = lax.Precision.HIGHEST
    lam = lax.complex(a_re.astype(F32), a_im.astype(F32))
    lam_dt = lam * jnp.exp(log_dt.astype(F32))[..., None]
    a_bar = jnp.exp(lam_dt)
    b_bar = ((a_bar - 1.0) / lam)[..., None] * lax.complex(b_re.astype(F32), b_im.astype(F32))
    cc = lax.complex(c_re.astype(F32), c_im.astype(F32))
    taus = jnp.arange(S5_T + 1, dtype=F32)
    pw = jnp.exp(lam_dt[:, :, None, :] * taus[None, None, :, None])
    kern = jnp.real(jnp.einsum("dgjn,dgtn,dgni->dgtji", cc, pw[:, :, :S5_T], b_bar, precision=hp))
    s = jnp.arange(S5_T)[:, None]
    t = jnp.arange(S5_T)[None, :]
    lanes = S5_T * S5_GS

    def toeplitz(kd, lag, mask):
        m = kd[:, jnp.clip(lag, 0, S5_T - 1)] * mask[None, :, :, None, None]
        return m.transpose(0, 1, 4, 2, 3).reshape(S5_GROUPS, lanes, lanes)

    toep_f = toeplitz(kern[0], t - s, (t >= s).astype(F32))
    toep_b = toeplitz(kern[1], s - t, (s >= t).astype(F32))

    def state_in(pwd, bd):
        m = pwd[:, :, None, :] * bd.transpose(0, 2, 1)[:, None, :, :]
        m = m.reshape(S5_GROUPS, lanes, S5_N)
        return jnp.concatenate([jnp.real(m), jnp.imag(m)], axis=-1)

    p_f = state_in(pw[0, :, S5_T - 1::-1][:, :S5_T], b_bar[0])
    p_b = state_in(pw[1, :, :S5_T], b_bar[1])

    def state_out(pwd, cd):
        m = pwd[:, :, None, :] * cd[:, None, :, :]
        m = m.transpose(0, 3, 1, 2).reshape(S5_GROUPS, S5_N, lanes)
        return jnp.concatenate([jnp.real(m), -jnp.imag(m)], axis=1)

    q_f = state_out(pw[0, :, 1:S5_T + 1], cc[0])
    q_b = state_out(pw[1, :, S5_T:0:-1], cc[1])
    ops = [m.astype(BF16) for m in (toep_f, toep_b, p_f, p_b, q_f, q_b)]
    a_t = pw[:, :, S5_T]
    ar, ai = jnp.real(a_t), jnp.imag(a_t)
    step = jnp.stack([jnp.concatenate([ar[0], ar[0]], -1), jnp.concatenate([-ai[0], ai[0]], -1),
                      jnp.concatenate([ar[1], ar[1]], -1), jnp.concatenate([-ai[1], ai[1]], -1)], axis=1)
    return ops, step


def _s5_kernel(u_ref, tf_ref, tb_ref, pf_ref, pb_ref, qf_ref, qb_ref, step_ref, h0_ref,
               y_ref, hfin_ref, e_scr, hin_scr, *, nc, nb):
    u = u_ref[...]
    e_scr[0] = jnp.dot(u, pf_ref[...], preferred_element_type=F32)
    e_scr[1] = jnp.dot(u, pb_ref[...], preferred_element_type=F32)
    for d in range(2):
        m1 = step_ref[2 * d:2 * d + 1, :]
        m2 = step_ref[2 * d + 1:2 * d + 2, :]
        h = h0_ref[d]
        for c in (range(nc) if d == 0 else range(nc - 1, -1, -1)):
            hin_scr[d, c * nb:(c + 1) * nb, :] = h
            h = h * m1 + pltpu.roll(h, S5_N, axis=1) * m2 + e_scr[d, c * nb:(c + 1) * nb, :]
        hfin_ref[d] = h
    y_ref[...] = (jnp.dot(u, tf_ref[...], preferred_element_type=F32)
                  + jnp.dot(u, tb_ref[...], preferred_element_type=F32)
                  + jnp.dot(hin_scr[0].astype(BF16), qf_ref[...], preferred_element_type=F32)
                  + jnp.dot(hin_scr[1].astype(BF16), qb_ref[...], preferred_element_type=F32))


def _s5(su, ops, step, h0, *, nb, seq):
    nc = seq // S5_T
    nbp = max(nb, 8)
    lanes = S5_T * S5_GS
    x = su.reshape(nb, nc, S5_T, S5_GROUPS, S5_GS).transpose(3, 1, 0, 2, 4)
    if nbp != nb:
        x = jnp.pad(x, ((0, 0), (0, 0), (0, nbp - nb), (0, 0), (0, 0)))
        h0 = jnp.pad(h0, ((0, 0), (0, 0), (0, nbp - nb), (0, 0)))
    x = x.reshape(S5_GROUPS, nc * nbp, lanes).astype(BF16)
    rows = nc * nbp
    g3 = lambda a, b: pl.BlockSpec((None, a, b), lambda g: (g, 0, 0))
    y, hfin = pl.pallas_call(
        functools.partial(_s5_kernel, nc=nc, nb=nbp),
        grid=(S5_GROUPS,),
        in_specs=[g3(rows, lanes), g3(lanes, lanes), g3(lanes, lanes), g3(lanes, 2 * S5_N), g3(lanes, 2 * S5_N),
                  g3(2 * S5_N, lanes), g3(2 * S5_N, lanes), g3(4, 2 * S5_N),
                  pl.BlockSpec((None, 2, nbp, 2 * S5_N), lambda g: (g, 0, 0, 0))],
        out_specs=[g3(rows, lanes), pl.BlockSpec((None, 2, nbp, 2 * S5_N), lambda g: (g, 0, 0, 0))],
        out_shape=[jax.ShapeDtypeStruct((S5_GROUPS, rows, lanes), F32),
                   jax.ShapeDtypeStruct((S5_GROUPS, 2, nbp, 2 * S5_N), F32)],
        scratch_shapes=[pltpu.VMEM((2, rows, 2 * S5_N), F32), pltpu.VMEM((2, rows, 2 * S5_N), F32)],
        compiler_params=_cparams(1), name="s5",
    )(x, *ops, step, h0)
    y = y.reshape(S5_GROUPS, nc, nbp, S5_T, S5_GS)[:, :, :nb].transpose(2, 1, 3, 0, 4)
    return y.reshape(nb * seq, S5_W), hfin


def _glu_kernel(y_ref, su_ref, d_ref, w_ref, o_ref, wbf_ref):
    @pl.when(pl.program_id(0) == 0)
    def _():
        wbf_ref[...] = w_ref[...].astype(BF16)

    z = _gelu_tanh(y_ref[...] + d_ref[...] * su_ref[...])
    gate = jax.nn.sigmoid(jnp.dot(z.astype(BF16), wbf_ref[...], preferred_element_type=F32))
    o_ref[...] = (z * gate).astype(o_ref.dtype)


def _glu(y, su, d, w_glu, layer):
    row = pl.BlockSpec((TM, S5_W), lambda i: (i, 0))
    return pl.pallas_call(
        _glu_kernel,
        grid=(T_ALL // TM,),
        in_specs=[row, row, pl.BlockSpec((None, 1, S5_W), lambda i: (layer, 0, 0)),
                  pl.BlockSpec((None, S5_W, S5_W), lambda i: (layer, 0, 0))],
        out_specs=row,
        out_shape=jax.ShapeDtypeStruct((T_ALL, S5_W), BF16),
        scratch_shapes=[pltpu.VMEM((S5_W, S5_W), BF16)],
        compiler_params=_cparams(1), name="glu",
    )(y, su, d.reshape(DEPTH, 1, S5_W), w_glu)


def _merge_kernel(a_ref, b_ref, c_ref, ga_ref, gb_ref, gc_ref, wa_ref, wb_ref, wc_ref, o_ref, wbf_ref):
    @pl.when(pl.program_id(1) == 0)
    def _():
        wbf_ref[0] = wa_ref[...].astype(BF16)
        wbf_ref[1] = wb_ref[...].astype(BF16)
        wbf_ref[2] = wc_ref[...].astype(BF16)

    acc = ga_ref[...] * jnp.dot(a_ref[...], wbf_ref[0], preferred_element_type=F32)
    acc += gb_ref[...] * jnp.dot(b_ref[...], wbf_ref[1], preferred_element_type=F32)
    acc += gc_ref[...] * jnp.dot(c_ref[...], wbf_ref[2], preferred_element_type=F32)
    o_ref[...] = acc.astype(o_ref.dtype)


def _merge(o_na, o_gla, zz, gates, w_na, w_gla, w_s5, layer):
    tn = 1024
    ncb = D_MODEL // tn
    act = pl.BlockSpec((TM, NA_W), lambda j, i: (i, 0))
    gate = lambda k: pl.BlockSpec((TM, tn), lambda j, i: (i, k * ncb + j))
    w = pl.BlockSpec((None, NA_W, tn), lambda j, i: (layer, 0, j))
    return pl.pallas_call(
        _merge_kernel,
        grid=(ncb, T_ALL // TM),
        in_specs=[act, act, act, gate(0), gate(1), gate(2), w, w, w],
        out_specs=pl.BlockSpec((TM, tn), lambda j, i: (i, j)),
        out_shape=jax.ShapeDtypeStruct((T_ALL, D_MODEL), BF16),
        scratch_shapes=[pltpu.VMEM((3, NA_W, tn), BF16)],
        compiler_params=_cparams(2), name="merge",
    )(o_na, o_gla, zz, gates, gates, gates, w_na, w_gla, w_s5)


def _out_proj_kernel(m_ref, w_ref, x_ref, mod_ref, o_ref, wbf_ref, *, gate_row):
    @pl.when(pl.program_id(1) == 0)
    def _():
        wbf_ref[...] = w_ref[...].astype(BF16)

    y = jnp.dot(m_ref[...], wbf_ref[...], preferred_element_type=F32)
    o_ref[...] = x_ref[...] + mod_ref[gate_row:gate_row + 1, :] * y


def _out_proj(merged, w_out, x, mod, layer):
    tn = 1024
    return pl.pallas_call(
        functools.partial(_out_proj_kernel, gate_row=2),
        grid=(D_MODEL // tn, T_ALL // TM),
        in_specs=[pl.BlockSpec((TM, D_MODEL), lambda j, i: (i, 0)),
                  pl.BlockSpec((None, D_MODEL, tn), lambda j, i: (layer, 0, j)),
                  pl.BlockSpec((TM, tn), lambda j, i: (i, j)),
                  pl.BlockSpec((None, 6, tn), lambda j, i: (_group_of_tile(i, TM), 0, j))],
        out_specs=pl.BlockSpec((TM, tn), lambda j, i: (i, j)),
        out_shape=jax.ShapeDtypeStruct((T_ALL, D_MODEL), F32),
        scratch_shapes=[pltpu.VMEM((D_MODEL, tn), BF16)],
        compiler_params=_cparams(2), name="out_proj",
    )(merged, w_out, x, mod)


def _first_max(v, iota, n):
    m = jnp.max(v, axis=0, keepdims=True)
    first = jnp.min(jnp.where(v == m, iota, float(n)), axis=0, keepdims=True)
    return m, first


def _router_kernel(x_ref, wt_ref, bias_ref, idx_ref, w_ref):
    per = N_EXPERTS // N_GROUPS
    x_hi, x_lo = _split_bf16(x_ref[...])
    w_hi, w_lo = _split_bf16(wt_ref[...])
    logits = (lax.dot_general(w_hi, x_hi, NT, preferred_element_type=F32)
              + lax.dot_general(w_hi, x_lo, NT, preferred_element_type=F32)
              + lax.dot_general(w_lo, x_hi, NT, preferred_element_type=F32))
    scores = jax.nn.sigmoid(logits)
    sel = scores + bias_ref[...]
    tm = sel.shape[1]
    iota_g = lax.broadcasted_iota(jnp.int32, (per, tm), 0).astype(F32)
    grp_rows = []
    for g in range(N_GROUPS):
        v = sel[g * per:(g + 1) * per, :]
        m1, first = _first_max(v, iota_g, per)
        m2 = jnp.max(jnp.where(iota_g == first, -jnp.inf, v), axis=0, keepdims=True)
        grp_rows.append(m1 + m2)
    grp = jnp.concatenate(grp_rows, axis=0)
    iota_n = lax.broadcasted_iota(jnp.int32, (N_GROUPS, tm), 0).astype(F32)
    chosen = jnp.zeros((N_GROUPS, tm), F32)
    for _ in range(TOPK_GROUPS):
        _, first = _first_max(grp, iota_n, N_GROUPS)
        hit = iota_n == first
        chosen = jnp.where(hit, 1.0, chosen)
        grp = jnp.where(hit, -jnp.inf, grp)
    mask = jnp.concatenate([jnp.broadcast_to(chosen[g:g + 1, :], (per, tm)) for g in range(N_GROUPS)], axis=0)
    sel = jnp.where(mask > 0.5, sel, -jnp.inf)
    iota_e = lax.broadcasted_iota(jnp.int32, (N_EXPERTS, tm), 0).astype(F32)
    ids, ws = [], []
    for _ in range(TOP_K):
        _, first = _first_max(sel, iota_e, N_EXPERTS)
        hit = iota_e == first
        ids.append(first)
        ws.append(jnp.sum(jnp.where(hit, scores, 0.0), axis=0, keepdims=True))
        sel = jnp.where(hit, -jnp.inf, sel)
    w = jnp.concatenate(ws, axis=0)
    idx_ref[...] = jnp.concatenate(ids, axis=0).astype(jnp.int32)
    w_ref[...] = w / jnp.sum(w, axis=0, keepdims=True) * ROUTED_SCALE


def _router(h, router_w_t, router_bias):
    tm = 256
    return pl.pallas_call(
        _router_kernel,
        grid=(T_ALL // tm,),
        in_specs=[pl.BlockSpec((tm, D_MODEL), lambda i: (i, 0)),
                  pl.BlockSpec((N_EXPERTS, D_MODEL), lambda i: (0, 0)),
                  pl.BlockSpec((N_EXPERTS, 1), lambda i: (0, 0))],
        out_specs=[pl.BlockSpec((TOP_K, tm), lambda i: (0, i)), pl.BlockSpec((TOP_K, tm), lambda i: (0, i))],
        out_shape=[jax.ShapeDtypeStruct((TOP_K, T_ALL), jnp.int32), jax.ShapeDtypeStruct((TOP_K, T_ALL), F32)],
        compiler_params=_cparams(1), name="router",
    )(h, router_w_t, router_bias.reshape(N_EXPERTS, 1))


def _dispatch(idx, w):
    n_assign = T_ALL * TOP_K
    flat_e = idx.reshape(-1)
    order = jnp.argsort(flat_e)
    sorted_e = flat_e[order]
    counts = jnp.bincount(flat_e, length=N_EXPERTS)
    padded = (counts + MOE_MB - 1) // MOE_MB * MOE_MB
    pad_end = jnp.cumsum(padded)
    slot_sorted = ((pad_end - padded)[sorted_e] + jnp.arange(n_assign)
                   - (jnp.cumsum(counts) - counts)[sorted_e]).astype(jnp.int32)
    slot_tok = jnp.zeros((MOE_SLOTS,), jnp.int32).at[slot_sorted].set((order // TOP_K).astype(jnp.int32))
    slot_w = jnp.zeros((MOE_SLOTS,), F32).at[slot_sorted].set(w.reshape(-1)[order])
    slot_of = jnp.zeros((n_assign,), jnp.int32).at[order].set(slot_sorted).reshape(T_ALL, TOP_K)
    block_e = jnp.minimum(jnp.searchsorted(pad_end, jnp.arange(MOE_NB) * MOE_MB, side="right"),
                          N_EXPERTS - 1).astype(jnp.int32)
    n_used = (pad_end[-1] // MOE_MB).astype(jnp.int32).reshape(1)
    return slot_tok, slot_w, slot_of, block_e, n_used


def _row_copy(src_hbm, dst_hbm, sem, src_row, dst_row):
    return pltpu.make_async_copy(src_hbm.at[pl.ds(src_row, 1), :], dst_hbm.at[pl.ds(dst_row, 1), :], sem)


def _gather_kernel(nu_ref, tok_ref, x_hbm, o_hbm, zeros, sems):
    i = pl.program_id(0)
    n_used = nu_ref[0]

    @pl.when(i == 0)
    def _():
        zeros[...] = jnp.zeros_like(zeros)

    @pl.when(i >= n_used)
    def _():
        pltpu.sync_copy(zeros, o_hbm.at[pl.ds(i * MOE_MB, MOE_MB), :])

    def issue(blk, sem):
        def body(j, carry):
            _row_copy(x_hbm, o_hbm, sem, tok_ref[0, 0, j], blk * MOE_MB + j).start()
            return carry
        lax.fori_loop(0, MOE_MB, body, 0)

    def drain(sem):
        def body(j, carry):
            _row_copy(x_hbm, o_hbm, sem, 0, 0).wait()
            return carry
        lax.fori_loop(0, MOE_MB, body, 0)

    for par in range(2):
        @pl.when((i < n_used) & (i % 2 == par))
        def _():
            issue(i, sems.at[par])

    for par in range(2):
        @pl.when((i >= 1) & (i - 1 < n_used) & ((i - 1) % 2 == par))
        def _():
            drain(sems.at[par])

        @pl.when((i == MOE_NB - 1) & (i < n_used) & (i % 2 == par))
        def _():
            drain(sems.at[par])


def _gather_rows(x, slot_tok, n_used):
    return pl.pallas_call(
        _gather_kernel,
        grid_spec=pltpu.PrefetchScalarGridSpec(
            num_scalar_prefetch=1,
            grid=(MOE_NB,),
            in_specs=[pl.BlockSpec((1, 1, MOE_MB), lambda i, nu: (i, 0, 0), memory_space=pltpu.SMEM),
                      pl.BlockSpec(memory_space=pl.ANY)],
            out_specs=pl.BlockSpec(memory_space=pl.ANY),
            scratch_shapes=[pltpu.VMEM((MOE_MB, D_MODEL), F32), pltpu.SemaphoreType.DMA((2,))]),
        out_shape=jax.ShapeDtypeStruct((MOE_SLOTS, D_MODEL), F32),
        compiler_params=_cparams(1), name="moe_gather",
    )(n_used, slot_tok.reshape(MOE_NB, 1, MOE_MB), x)


def _expert_kernel(be_ref, nu_ref, x_ref, wg_ref, wu_ref, wd_ref, sw_ref, o_ref, wg_bf, wu_bf, wd_bf):
    i = pl.program_id(0)
    used = i < nu_ref[0]

    @pl.when(used)
    def _():
        @pl.when((i == 0) | (be_ref[i] != be_ref[jnp.maximum(i - 1, 0)]))
        def _():
            wg_bf[...] = wg_ref[...].astype(BF16)
            wu_bf[...] = wu_ref[...].astype(BF16)
            wd_bf[...] = wd_ref[...].astype(BF16)

        x = x_ref[...].astype(BF16)
        g = jnp.dot(x, wg_bf[...], preferred_element_type=F32)
        u = jnp.dot(x, wu_bf[...], preferred_element_type=F32)
        a = (_silu(g) * u).astype(BF16)
        o_ref[...] = jnp.dot(a, wd_bf[...], preferred_element_type=F32) * sw_ref[...]

    @pl.when(jnp.logical_not(used))
    def _():
        o_ref[...] = jnp.zeros_like(o_ref)


def _experts(x_sorted, slot_w, block_e, n_used, wg, wu, wd, layer):
    row_blk = lambda i, be, nu: (i, 0)
    w_up = pl.BlockSpec((None, None, D_MODEL, F_EXPERT), lambda i, be, nu: (layer, be[i], 0, 0))
    return pl.pallas_call(
        _expert_kernel,
        grid_spec=pltpu.PrefetchScalarGridSpec(
            num_scalar_prefetch=2,
            grid=(MOE_NB,),
            in_specs=[pl.BlockSpec((MOE_MB, D_MODEL), row_blk), w_up, w_up,
                      pl.BlockSpec((None, None, F_EXPERT, D_MODEL), lambda i, be, nu: (layer, be[i], 0, 0)),
                      pl.BlockSpec((MOE_MB, 1), lambda i, be, nu: (i, 0))],
            out_specs=pl.BlockSpec((MOE_MB, D_MODEL), lambda i, be, nu: (i, 0)),
            scratch_shapes=[pltpu.VMEM((D_MODEL, F_EXPERT), BF16), pltpu.VMEM((D_MODEL, F_EXPERT), BF16),
                            pltpu.VMEM((F_EXPERT, D_MODEL), BF16)]),
        out_shape=jax.ShapeDtypeStruct((MOE_SLOTS, D_MODEL), F32),
        compiler_params=_cparams(1), name="moe_experts",
    )(block_e, n_used, x_sorted, wg, wu, wd, slot_w.reshape(MOE_SLOTS, 1))


def _shared_up_kernel(x_ref, wg_ref, wu_ref, o_ref, wg_bf, wu_bf):
    @pl.when(pl.program_id(0) == 0)
    def _():
        wg_bf[...] = wg_ref[...].astype(BF16)
        wu_bf[...] = wu_ref[...].astype(BF16)

    x = x_ref[...].astype(BF16)
    g = jnp.dot(x, wg_bf[...], preferred_element_type=F32)
    u = jnp.dot(x, wu_bf[...], preferred_element_type=F32)
    o_ref[...] = (_silu(g) * u).astype(o_ref.dtype)


def _shared_up(h, wg, wu, layer):
    w = pl.BlockSpec((None, D_MODEL, F_SHARED), lambda i: (layer, 0, 0))
    return pl.pallas_call(
        _shared_up_kernel,
        grid=(T_ALL // TM,),
        in_specs=[pl.BlockSpec((TM, D_MODEL), lambda i: (i, 0)), w, w],
        out_specs=pl.BlockSpec((TM, F_SHARED), lambda i: (i, 0)),
        out_shape=jax.ShapeDtypeStruct((T_ALL, F_SHARED), BF16),
        scratch_shapes=[pltpu.VMEM((D_MODEL, F_SHARED), BF16), pltpu.VMEM((D_MODEL, F_SHARED), BF16)],
        compiler_params=_cparams(1), name="shared_up",
    )(h, wg, wu)


def _combine_kernel(slot_ref, y_hbm, act_ref, wd_ref, x_ref, mod_ref, o_ref, buf, wd_bf, sem, *, gate_row):
    @pl.when(pl.program_id(0) == 0)
    def _():
        wd_bf[...] = wd_ref[...].astype(BF16)

    n_rows = TOP_K * CMB_TM

    def issue(j, carry):
        pltpu.make_async_copy(y_hbm.at[pl.ds(slot_ref[0, 0, j], 1), :], buf.at[pl.ds(j, 1), :], sem).start()
        return carry

    def drain(j, carry):
        pltpu.make_async_copy(y_hbm.at[pl.ds(0, 1), :], buf.at[pl.ds(0, 1), :], sem).wait()
        return carry

    lax.fori_loop(0, n_rows, issue, 0)
    y = jnp.dot(act_ref[...], wd_bf[...], preferred_element_type=F32)
    lax.fori_loop(0, n_rows, drain, 0)
    for k in range(TOP_K):
        y = y + buf[k * CMB_TM:(k + 1) * CMB_TM, :]
    o_ref[...] = x_ref[...] + mod_ref[gate_row:gate_row + 1, :] * y


def _combine(y_sorted, slot_of, act, sh_wd, x, mod, layer):
    nt = T_ALL // CMB_TM
    slots = slot_of.reshape(nt, CMB_TM, TOP_K).transpose(0, 2, 1).reshape(nt, 1, TOP_K * CMB_TM)
    return pl.pallas_call(
        functools.partial(_combine_kernel, gate_row=5),
        grid=(nt,),
        in_specs=[pl.BlockSpec((1, 1, TOP_K * CMB_TM), lambda i: (i, 0, 0), memory_space=pltpu.SMEM),
                  pl.BlockSpec(memory_space=pl.ANY),
                  pl.BlockSpec((CMB_TM, F_SHARED), lambda i: (i, 0)),
                  pl.BlockSpec((None, F_SHARED, D_MODEL), lambda i: (layer, 0, 0)),
                  pl.BlockSpec((CMB_TM, D_MODEL), lambda i: (i, 0)),
                  pl.BlockSpec((None, 6, D_MODEL), lambda i: (_group_of_tile(i, CMB_TM), 0, 0))],
        out_specs=pl.BlockSpec((CMB_TM, D_MODEL), lambda i: (i, 0)),
        out_shape=jax.ShapeDtypeStruct((T_ALL, D_MODEL), F32),
        scratch_shapes=[pltpu.VMEM((TOP_K * CMB_TM, D_MODEL), F32), pltpu.VMEM((F_SHARED, D_MODEL), BF16),
                        pltpu.SemaphoreType.DMA(())],
        compiler_params=_cparams(1), name="moe_combine",
    )(slots, y_sorted, act, sh_wd, x, mod)


def kernel(x_prompt, x_sample, cache_na_k, cache_na_v, state_gla, state_s5, c, c_ctx, ada_w, ada_b, norm1_g, norm2_g, w_in, na_rpb, gla_wa, gla_ba, gla_norm_g, s5_a_re, s5_a_im, s5_log_dt, s5_b_re, s5_b_im, s5_c_re, s5_c_im, s5_d, s5_w_glu, w_br_na, w_br_gla, w_br_s5, w_merge, w_out, router_w, router_bias, exp_wg, exp_wu, exp_wd, sh_wg, sh_wu, sh_wd, final_norm_g):
    x = jnp.concatenate([x_prompt.reshape(T_CTX, D_MODEL), x_sample.reshape(T_LAT, D_MODEL)], axis=0)
    cvec = jnp.concatenate([c_ctx[None, :], c, jnp.zeros((MOD_ROWS - N_MOD, D_MODEL), F32)], axis=0)
    mods = _ada(cvec, ada_w, ada_b).reshape(DEPTH, MOD_ROWS, 6, D_MODEL)
    cos, sin = _rope_tables()
    zero_gla = jnp.zeros((BATCH, 2, GLA_HEADS, GLA_DV, GLA_DK), F32)
    zero_s5 = jnp.zeros((S5_GROUPS, 2, BATCH, 2 * S5_N), F32)
    new_k, new_v, new_gla, new_s5 = [], [], [], []
    for l in range(DEPTH):
        mod = mods[l]
        h = _norm(x, norm1_g[l], mod, (0, 1), BF16)
        u = _mm(h, w_in, l, 1024, n=IN_MAIN)
        tail = _mm(h, w_in[l, :, IN_MAIN:], None, IN_W - IN_MAIN)
        lr = tail[:, :2 * GLA_LR]
        su = tail[:, 2 * GLA_LR:]
        gates = _mm(h, w_merge, l, 1024, act="sigmoid")

        o_na = jnp.concatenate([_ctx_attn(u),
                                _na_attn(u, cache_na_k, cache_na_v, _na_bias(na_rpb[l]), l)], axis=0)

        s0_lat = state_gla[:, l].transpose(0, 1, 2, 4, 3)
        og_ctx, sfin = _gla(u, lr, gla_wa, gla_ba, gla_norm_g, zero_gla, cos, sin, l, latent=False)
        og_lat, _ = _gla(u, lr, gla_wa, gla_ba, gla_norm_g, s0_lat, cos, sin, l, latent=True)
        o_gla = jnp.concatenate([og_ctx, og_lat], axis=0)

        ops, step = _s5_operators(s5_a_re[l], s5_a_im[l], s5_log_dt[l], s5_b_re[l], s5_b_im[l],
                                  s5_c_re[l], s5_c_im[l])
        st = state_s5[:, l].astype(F32)
        h0_lat = jnp.concatenate([st[..., 0], st[..., 1]], axis=-1).transpose(2, 1, 0, 3)
        y_ctx, hfin = _s5(su[:T_CTX], ops, step, zero_s5, nb=BATCH, seq=SEQ)
        y_lat, _ = _s5(su[T_CTX:], ops, step, h0_lat, nb=DEC_BATCH, seq=DEC_SEQ)
        zz = _glu(jnp.concatenate([y_ctx, y_lat], axis=0), su, s5_d, s5_w_glu, l)

        merged = _merge(o_na, o_gla, zz, gates, w_br_na, w_br_gla, w_br_s5, l)
        x = _out_proj(merged, w_out, x, mod, l)

        h2 = _norm(x, norm2_g[l], mod, (3, 4), F32)
        idx_t, w_t = _router(h2, router_w[l].T, router_bias[l])
        slot_tok, slot_w, slot_of, block_e, n_used = _dispatch(idx_t.T, w_t.T)
        x_sorted = _gather_rows(h2, slot_tok, n_used)
        y_sorted = _experts(x_sorted, slot_w, block_e, n_used, exp_wg, exp_wu, exp_wd, l)
        act = _shared_up(h2, sh_wg, sh_wu, l)
        x = _combine(y_sorted, slot_of, act, sh_wd, x, mod, l)

        new_k.append(u[:T_CTX, NA_W:2 * NA_W].reshape(BATCH, SEQ, NA_HEADS, NA_DH))
        new_v.append(u[:T_CTX, 2 * NA_W:3 * NA_W].reshape(BATCH, SEQ, NA_HEADS, NA_DH))
        new_gla.append(sfin.transpose(0, 1, 2, 4, 3))
        hf = hfin.transpose(2, 1, 0, 3)
        new_s5.append(jnp.stack([hf[..., :S5_N], hf[..., S5_N:]], axis=-1))

    y = _norm(x, final_norm_g, None, None, F32)
    return (y[:T_CTX].reshape(BATCH, SEQ, D_MODEL), y[T_CTX:].reshape(DEC_BATCH, DEC_SEQ, D_MODEL),
            jnp.stack(new_k, axis=1), jnp.stack(new_v, axis=1),
            jnp.stack(new_gla, axis=1), jnp.stack(new_s5, axis=1))
```

```python
import functools
import math

import jax
import jax.numpy as jnp
from jax import lax
from jax.experimental import pallas as pl
from jax.experimental.pallas import tpu as pltpu

F32 = jnp.float32
BF16 = jnp.bfloat16

D_MODEL = 2048
BATCH = 16
SEQ = 256
DEPTH = 2
DEC_BATCH = 2
DEC_SEQ = 1024
PAST_LEN = 512
GRID_W = 64
NA_HEADS = 8
NA_DH = 128
NA_W = NA_HEADS * NA_DH
NA_ROWS = 8
NA_COLS = 16
GLA_HEADS = 4
GLA_DK = 128
GLA_DV = 256
GLA_QK_W = GLA_HEADS * GLA_DK
GLA_V_W = GLA_HEADS * GLA_DV
GLA_LR = 16
GLA_TAU = 16.0
GLA_CHUNK = 64
GLA_SUB = 16
ROPE_BASE = 10000.0
S5_W = 1024
S5_GS = 16
S5_GROUPS = S5_W // S5_GS
S5_N = 64
S5_T = 16
IN_W = 3 * NA_W + 2 * GLA_QK_W + 2 * GLA_V_W + 2 * GLA_LR + S5_W
IN_MAIN = 3 * NA_W + 2 * GLA_QK_W + 2 * GLA_V_W
N_EXPERTS = 64
TOP_K = 8
N_GROUPS = 8
TOPK_GROUPS = 4
F_EXPERT = 512
F_SHARED = 512
ROUTED_SCALE = 2.5
EPS = 1e-6

T_CTX = BATCH * SEQ
T_LAT = DEC_BATCH * DEC_SEQ
T_ALL = T_CTX + T_LAT
N_MOD = 1 + DEC_BATCH
MOD_ROWS = 8

TM = 512
MOE_MB = 256
MOE_NB = T_ALL * TOP_K // MOE_MB + N_EXPERTS
MOE_SLOTS = MOE_NB * MOE_MB
CMB_TM = 128
NEG = -1e30

VMEM_LIMIT = 56 * 1024 * 1024

NT = (((1,), (1,)), ((), ()))
TN = (((0,), (0,)), ((), ()))


def _cparams(n_axes):
    return pltpu.CompilerParams(dimension_semantics=("arbitrary",) * n_axes,
                                vmem_limit_bytes=VMEM_LIMIT)


def _group_of_tile(i, tm):
    row = i * tm
    return jnp.where(row < T_CTX, 0, 1 + (row - T_CTX) // DEC_SEQ)


def _silu(x):
    return x * jax.nn.sigmoid(x)


def _gelu_tanh(x):
    return 0.5 * x * (1.0 + jnp.tanh(math.sqrt(2.0 / math.pi) * (x + 0.044715 * (x * x * x))))


def _ada_kernel(c_ref, w_ref, b_ref, o_ref):
    s = _silu(c_ref[...]).astype(BF16)
    o_ref[...] = jnp.dot(s, w_ref[...].astype(BF16), preferred_element_type=F32) + b_ref[...]


def _ada(cvec, ada_w, ada_b):
    tn = 1024
    return pl.pallas_call(
        _ada_kernel,
        grid=(DEPTH, 6 * D_MODEL // tn),
        in_specs=[pl.BlockSpec((MOD_ROWS, D_MODEL), lambda l, j: (0, 0)),
                  pl.BlockSpec((None, D_MODEL, tn), lambda l, j: (l, 0, j)),
                  pl.BlockSpec((None, 1, tn), lambda l, j: (l, 0, j))],
        out_specs=pl.BlockSpec((None, MOD_ROWS, tn), lambda l, j: (l, 0, j)),
        out_shape=jax.ShapeDtypeStruct((DEPTH, MOD_ROWS, 6 * D_MODEL), F32),
        compiler_params=_cparams(2), name="ada",
    )(cvec, ada_w, ada_b.reshape(DEPTH, 1, 6 * D_MODEL))


def _norm_kernel(x_ref, g_ref, *rest, rows):
    o_ref = rest[-1]
    x = x_ref[...]
    y = x * lax.rsqrt(jnp.mean(x * x, axis=-1, keepdims=True) + EPS) * g_ref[...]
    if rows is not None:
        mod_ref = rest[0]
        y = y * (1.0 + mod_ref[rows[1]:rows[1] + 1, :]) + mod_ref[rows[0]:rows[0] + 1, :]
    o_ref[...] = y.astype(o_ref.dtype)


def _norm(x, g, mod, rows, out_dtype):
    tm = 256
    in_specs = [pl.BlockSpec((tm, D_MODEL), lambda i: (i, 0)),
                pl.BlockSpec((1, D_MODEL), lambda i: (0, 0))]
    args = [x, g.reshape(1, D_MODEL)]
    if rows is not None:
        in_specs.append(pl.BlockSpec((None, 6, D_MODEL), lambda i: (_group_of_tile(i, tm), 0, 0)))
        args.append(mod)
    return pl.pallas_call(
        functools.partial(_norm_kernel, rows=rows),
        grid=(T_ALL // tm,),
        in_specs=in_specs,
        out_specs=pl.BlockSpec((tm, D_MODEL), lambda i: (i, 0)),
        out_shape=jax.ShapeDtypeStruct((T_ALL, D_MODEL), out_dtype),
        compiler_params=_cparams(1), name="norm",
    )(*args)


def _mm_kernel(x_ref, w_ref, o_ref, wbf_ref, *, act):
    @pl.when(pl.program_id(1) == 0)
    def _():
        wbf_ref[...] = w_ref[...].astype(BF16)

    acc = jnp.dot(x_ref[...].astype(BF16), wbf_ref[...], preferred_element_type=F32)
    if act == "sigmoid":
        acc = jax.nn.sigmoid(acc)
    o_ref[...] = acc.astype(o_ref.dtype)


def _mm(x, w, layer, tn, act=None, out_dtype=F32, n=None):
    k = x.shape[1]
    n = w.shape[-1] if n is None else n
    if layer is None:
        w_spec = pl.BlockSpec((k, tn), lambda j, i: (0, j))
    else:
        w_spec = pl.BlockSpec((None, k, tn), lambda j, i: (layer, 0, j))
    return pl.pallas_call(
        functools.partial(_mm_kernel, act=act),
        grid=(n // tn, T_ALL // TM),
        in_specs=[pl.BlockSpec((TM, k), lambda j, i: (i, 0)), w_spec],
        out_specs=pl.BlockSpec((TM, tn), lambda j, i: (i, j)),
        out_shape=jax.ShapeDtypeStruct((T_ALL, n), out_dtype),
        scratch_shapes=[pltpu.VMEM((k, tn), BF16)],
        compiler_params=_cparams(2), name="mm",
    )(x, w)


def _ctx_attn_kernel(q_ref, k_ref, v_ref, o_ref):
    scale = NA_DH ** -0.5
    for h in range(NA_HEADS):
        sl = slice(h * NA_DH, (h + 1) * NA_DH)
        q = q_ref[:, sl].astype(BF16)
        k = k_ref[:, sl].astype(BF16)
        v = v_ref[:, sl].astype(BF16)
        s = lax.dot_general(q, k, NT, preferred_element_type=F32) * scale
        p = jnp.exp(s - jnp.max(s, axis=-1, keepdims=True))
        o = jnp.dot(p.astype(BF16), v, preferred_element_type=F32) / jnp.sum(p, axis=-1, keepdims=True)
        o_ref[:, sl] = o.astype(o_ref.dtype)


def _ctx_attn(u):
    spec = lambda cb: pl.BlockSpec((SEQ, NA_W), lambda b: (b, cb))
    return pl.pallas_call(
        _ctx_attn_kernel,
        grid=(BATCH,),
        in_specs=[spec(0), spec(1), spec(2)],
        out_specs=pl.BlockSpec((SEQ, NA_W), lambda b: (b, 0)),
        out_shape=jax.ShapeDtypeStruct((T_CTX, NA_W), BF16),
        compiler_params=_cparams(1), name="ctx_attn",
    )(u, u, u)


NA_GRID_ROWS = DEC_SEQ // GRID_W
NA_KR = min(NA_ROWS, NA_GRID_ROWS)
NA_LOC = NA_KR * GRID_W


def _na_bias(rpb):
    pad = GRID_W - NA_COLS
    p = jnp.pad(rpb, ((0, 0), (0, 0), (pad, pad)))
    band = jnp.stack([p[:, :, GRID_W - 1 - qc:2 * GRID_W - 1 - qc] for qc in range(GRID_W)], axis=2)
    qc = jnp.arange(GRID_W)[:, None]
    kc = jnp.arange(GRID_W)[None, :]
    win = jnp.clip(qc - NA_COLS // 2, 0, GRID_W - NA_COLS)
    valid = (kc >= win) & (kc < win + NA_COLS)
    band = jnp.where(valid[None, None], band, NEG)
    rows = []
    for r in range(NA_GRID_ROWS):
        lo = min(max(r - NA_KR // 2, 0), NA_GRID_ROWS - NA_KR) - r + NA_ROWS - 1
        rows.append(band[:, lo:lo + NA_KR].transpose(0, 2, 1, 3).reshape(NA_HEADS, GRID_W, NA_LOC))
    return jnp.stack(rows, axis=1)


def _na_kernel(q_ref, k_ref, v_ref, ck_ref, cv_ref, bias_ref, o_ref):
    scale = NA_DH ** -0.5
    r = pl.program_id(2)
    start = pl.multiple_of(jnp.clip(r - NA_KR // 2, 0, NA_GRID_ROWS - NA_KR) * GRID_W, GRID_W)
    q = q_ref[...].astype(BF16)
    kl = k_ref[pl.ds(start, NA_LOC), :].astype(BF16)
    vl = v_ref[pl.ds(start, NA_LOC), :].astype(BF16)
    s1 = lax.dot_general(q, kl, NT, preferred_element_type=F32) * scale + bias_ref[...]
    s2 = lax.dot_general(q, ck_ref[...].astype(BF16), NT, preferred_element_type=F32) * scale
    m = jnp.maximum(jnp.max(s1, axis=-1, keepdims=True), jnp.max(s2, axis=-1, keepdims=True))
    p1 = jnp.exp(s1 - m)
    p2 = jnp.exp(s2 - m)
    den = jnp.sum(p1, axis=-1, keepdims=True) + jnp.sum(p2, axis=-1, keepdims=True)
    o = (jnp.dot(p1.astype(BF16), vl, preferred_element_type=F32)
         + jnp.dot(p2.astype(BF16), cv_ref[...].astype(BF16), preferred_element_type=F32))
    o_ref[...] = (o / den).astype(o_ref.dtype)


def _na_attn(u, cache_k, cache_v, bias, layer):
    lat_rb = T_CTX // GRID_W
    lat_sb = T_CTX // DEC_SEQ
    ck = cache_k.reshape(DEC_BATCH, DEPTH, PAST_LEN, NA_W)
    cv = cache_v.reshape(DEC_BATCH, DEPTH, PAST_LEN, NA_W)
    kv_spec = lambda cb: pl.BlockSpec((DEC_SEQ, NA_DH), lambda b, h, r: (lat_sb + b, cb * NA_HEADS + h))
    c_spec = pl.BlockSpec((None, None, PAST_LEN, NA_DH), lambda b, h, r: (b, layer, 0, h))
    return pl.pallas_call(
        _na_kernel,
        grid=(DEC_BATCH, NA_HEADS, NA_GRID_ROWS),
        in_specs=[pl.BlockSpec((GRID_W, NA_DH), lambda b, h, r: (lat_rb + b * NA_GRID_ROWS + r, h)),
                  kv_spec(1), kv_spec(2), c_spec, c_spec,
                  pl.BlockSpec((None, None, GRID_W, NA_LOC), lambda b, h, r: (h, r, 0, 0))],
        out_specs=pl.BlockSpec((GRID_W, NA_DH), lambda b, h, r: (b * NA_GRID_ROWS + r, h)),
        out_shape=jax.ShapeDtypeStruct((T_LAT, NA_W), BF16),
        compiler_params=_cparams(3), name="na_attn",
    )(u, u, u, ck, cv, bias)


def _rope_tables():
    half = GLA_DK // 2
    nf = half // 2
    t = jnp.arange(DEC_SEQ)
    freqs = ROPE_BASE ** (-jnp.arange(nf, dtype=F32) / nf)
    ang_r = (t // GRID_W).astype(F32)[:, None] * freqs
    ang_c = (t % GRID_W).astype(F32)[:, None] * freqs
    cos = jnp.concatenate([jnp.cos(ang_r), jnp.cos(ang_r), jnp.cos(ang_c), jnp.cos(ang_c)], axis=-1)
    sin = jnp.concatenate([-jnp.sin(ang_r), jnp.sin(ang_r), -jnp.sin(ang_c), jnp.sin(ang_c)], axis=-1)
    return cos, sin


def _split_bf16(x):
    hi = x.astype(BF16)
    return hi, (x - hi.astype(F32)).astype(BF16)


def _gla_kernel(q_ref, k_ref, v_ref, gg_ref, lr_ref, wa_ref, ba_ref, ng_ref, cos_ref, sin_ref, s0_ref,
                o_ref, sfin_ref, qs, ks, las, o_acc, st, *, seq, rope):
    nc = seq // GLA_CHUNK
    nq = GLA_DK // 4

    def rot(x):
        lane = lax.broadcasted_iota(jnp.int32, x.shape, 1)
        partner = jnp.where((lane % (2 * nq)) < nq,
                            pltpu.roll(x, GLA_DK - nq, axis=1), pltpu.roll(x, nq, axis=1))
        return x * cos_ref[...] + partner * sin_ref[...]

    q = q_ref[...] * (GLA_DK ** -0.5)
    k = k_ref[...]
    if rope:
        q = rot(q)
        k = rot(k)
    qs[...] = q
    ks[...] = k
    lr = lr_ref[...]
    for d in range(2):
        z = jnp.dot(lr[:, d * GLA_LR:(d + 1) * GLA_LR].astype(BF16), wa_ref[d].astype(BF16),
                    preferred_element_type=F32) + ba_ref[d:d + 1, :]
        las[d] = -(jnp.maximum(-z, 0.0) + jnp.log1p(jnp.exp(-jnp.abs(z)))) / GLA_TAU
        st[d] = s0_ref[d]

    row = lax.broadcasted_iota(jnp.int32, (GLA_CHUNK, GLA_CHUNK), 0)
    col = lax.broadcasted_iota(jnp.int32, (GLA_CHUNK, GLA_CHUNK), 1)
    key_row = lax.broadcasted_iota(jnp.int32, (GLA_CHUNK, 1), 0)

    def chunk(c, d):
        rev = d == 1
        rows = pl.ds(pl.multiple_of(c * GLA_CHUNK, GLA_CHUNK), GLA_CHUNK)
        qc = qs[rows, :]
        kc = ks[rows, :]
        vc = v_ref[rows, :].astype(BF16)
        la = las[d, rows, :]
        causal = (col >= row) if rev else (col <= row)
        tri = jnp.where(causal, 1.0, 0.0).astype(BF16)
        la_hi, la_lo = _split_bf16(la)
        b = (jnp.dot(tri, la_hi, preferred_element_type=F32)
             + jnp.dot(tri, la_lo, preferred_element_type=F32))
        bex = b - la
        b_last = b[0:1, :] if rev else b[GLA_CHUNK - 1:GLA_CHUNK, :]
        blocks = []
        for i in range(GLA_CHUNK // GLA_SUB):
            lo, hi = i * GLA_SUB, (i + 1) * GLA_SUB
            ref = bex[hi - 1:hi, :] if rev else bex[lo:lo + 1, :]
            qt = (qc[lo:hi, :] * jnp.exp(b[lo:hi, :] - ref)).astype(BF16)
            allowed = (key_row >= lo) if rev else (key_row < hi)
            kt = (kc * jnp.exp(jnp.where(allowed, ref - b, -jnp.inf))).astype(BF16)
            blocks.append(lax.dot_general(qt, kt, NT, preferred_element_type=F32))
        att = jnp.where(causal, jnp.concatenate(blocks, axis=0), 0.0)
        s_t = st[d]
        o = (jnp.dot(att.astype(BF16), vc, preferred_element_type=F32)
             + lax.dot_general((qc * jnp.exp(b)).astype(BF16), s_t.astype(BF16), NT,
                               preferred_element_type=F32))
        khat = (kc * jnp.exp(b_last - b)).astype(BF16)
        st[d] = s_t * jnp.exp(b_last) + lax.dot_general(vc, khat, TN, preferred_element_type=F32)
        if rev:
            o_acc[rows, :] = o_acc[rows, :] + o
        else:
            o_acc[rows, :] = o

    def fwd(c, carry):
        chunk(c, 0)
        return carry

    def bwd(c, carry):
        chunk(nc - 1 - c, 1)
        return carry

    lax.fori_loop(0, nc, fwd, 0)
    lax.fori_loop(0, nc, bwd, 0)
    sfin_ref[...] = st[...]
    o = o_acc[...]
    o = o * lax.rsqrt(jnp.mean(o * o, axis=-1, keepdims=True) + EPS) * ng_ref[...]
    o_ref[...] = (o * _silu(gg_ref[...])).astype(o_ref.dtype)


def _gla(u, lr, wa, ba, norm_g, s0_t, cos, sin, layer, *, latent):
    seq, nb, first = (DEC_SEQ, DEC_BATCH, T_CTX // DEC_SEQ) if latent else (SEQ, BATCH, 0)
    qk_cb = 3 * NA_W // GLA_DK
    v_cb = (3 * NA_W + 2 * GLA_QK_W) // GLA_DV
    row = lambda w, cb: pl.BlockSpec((seq, w), lambda b, h: (first + b, cb + h))
    return pl.pallas_call(
        functools.partial(_gla_kernel, seq=seq, rope=latent),
        grid=(nb, GLA_HEADS),
        in_specs=[row(GLA_DK, qk_cb), row(GLA_DK, qk_cb + GLA_HEADS), row(GLA_DV, v_cb),
                  row(GLA_DV, v_cb + GLA_HEADS),
                  pl.BlockSpec((seq, 2 * GLA_LR), lambda b, h: (first + b, 0)),
                  pl.BlockSpec((None, 2, GLA_LR, GLA_DK), lambda b, h: (layer, 0, 0, h)),
                  pl.BlockSpec((None, 2, GLA_DK), lambda b, h: (layer, 0, h)),
                  pl.BlockSpec((None, 1, GLA_DV), lambda b, h: (layer, 0, h)),
                  pl.BlockSpec((seq, GLA_DK), lambda b, h: (0, 0)),
                  pl.BlockSpec((seq, GLA_DK), lambda b, h: (0, 0)),
                  pl.BlockSpec((None, 2, None, GLA_DV, GLA_DK), lambda b, h: (b, 0, h, 0, 0))],
        out_specs=[pl.BlockSpec((seq, GLA_DV), lambda b, h: (b, h)),
                   pl.BlockSpec((None, 2, None, GLA_DV, GLA_DK), lambda b, h: (b, 0, h, 0, 0))],
        out_shape=[jax.ShapeDtypeStruct((nb * seq, GLA_V_W), BF16),
                   jax.ShapeDtypeStruct((nb, 2, GLA_HEADS, GLA_DV, GLA_DK), F32)],
        scratch_shapes=[pltpu.VMEM((seq, GLA_DK), F32), pltpu.VMEM((seq, GLA_DK), F32),
                        pltpu.VMEM((2, seq, GLA_DK), F32), pltpu.VMEM((seq, GLA_DV), F32),
                        pltpu.VMEM((2, GLA_DV, GLA_DK), F32)],
        compiler_params=_cparams(2), name="gla",
    )(u, u, u, u, lr, wa, ba, norm_g.reshape(DEPTH, 1, GLA_V_W), cos[:seq], sin[:seq], s0_t)


def _s5_operators(a_re, a_im, log_dt, b_re, b_im, c_re, c_im):
    hp = lax.Precision.HIGHEST
    lam = lax.complex(a_re.astype(F32), a_im.astype(F32))
    lam_dt = lam * jnp.exp(log_dt.astype(F32))[..., None]
    a_bar = jnp.exp(lam_dt)
    b_bar = ((a_bar - 1.0) / lam)[..., None] * lax.complex(b_re.astype(F32), b_im.astype(F32))
    cc = lax.complex(c_re.astype(F32), c_im.astype(F32))
    taus = jnp.arange(S5_T + 1, dtype=F32)
    pw = jnp.exp(lam_dt[:, :, None, :] * taus[None, None, :, None])
    kern = jnp.real(jnp.einsum("dgjn,dgtn,dgni->dgtji", cc, pw[:, :, :S5_T], b_bar, precision=hp))
    lanes = S5_T * S5_GS
    zpad = lambda a, before, after: jnp.pad(a, ((0, 0), (before, after), (0, 0), (0, 0)))

    def toeplitz(rows):
        return jnp.stack(rows, axis=1).transpose(0, 1, 4, 2, 3).reshape(S5_GROUPS, lanes, lanes)

    toep_f = toeplitz([zpad(kern[0][:, :S5_T - s], s, 0) for s in range(S5_T)])
    toep_b = toeplitz([zpad(kern[1][:, s::-1], 0, S5_T - 1 - s) for s in range(S5_T)])

    def state_in(pwd, bd):
        m = pwd[:, :, None, :] * bd.transpose(0, 2, 1)[:, None, :, :]
        m = m.reshape(S5_GROUPS, lanes, S5_N)
        return jnp.concatenate([jnp.real(m), jnp.imag(m)], axis=-1)

    p_f = state_in(pw[0, :, S5_T - 1::-1][:, :S5_T], b_bar[0])
    p_b = state_in(pw[1, :, :S5_T], b_bar[1])

    def state_out(pwd, cd):
        m = pwd[:, :, None, :] * cd[:, None, :, :]
        m = m.transpose(0, 3, 1, 2).reshape(S5_GROUPS, S5_N, lanes)
        return jnp.concatenate([jnp.real(m), -jnp.imag(m)], axis=1)

    q_f = state_out(pw[0, :, 1:S5_T + 1], cc[0])
    q_b = state_out(pw[1, :, S5_T:0:-1], cc[1])
    ops = [m.astype(BF16) for m in (toep_f, toep_b, p_f, p_b, q_f, q_b)]
    a_t = pw[:, :, S5_T]
    ar, ai = jnp.real(a_t), jnp.imag(a_t)
    step = jnp.stack([jnp.concatenate([ar[0], ar[0]], -1), jnp.concatenate([-ai[0], ai[0]], -1),
                      jnp.concatenate([ar[1], ar[1]], -1), jnp.concatenate([-ai[1], ai[1]], -1)], axis=1)
    return ops, step


def _s5_kernel(u_ref, tf_ref, tb_ref, pf_ref, pb_ref, qf_ref, qb_ref, step_ref, h0_ref,
               y_ref, hfin_ref, e_scr, hin_scr, *, nc, nb):
    u = u_ref[...]
    e_scr[0] = jnp.dot(u, pf_ref[...], preferred_element_type=F32)
    e_scr[1] = jnp.dot(u, pb_ref[...], preferred_element_type=F32)
    for d in range(2):
        m1 = step_ref[2 * d:2 * d + 1, :]
        m2 = step_ref[2 * d + 1:2 * d + 2, :]
        h = h0_ref[d]
        for c in (range(nc) if d == 0 else range(nc - 1, -1, -1)):
            hin_scr[d, c * nb:(c + 1) * nb, :] = h
            h = h * m1 + pltpu.roll(h, S5_N, axis=1) * m2 + e_scr[d, c * nb:(c + 1) * nb, :]
        hfin_ref[d] = h
    y_ref[...] = (jnp.dot(u, tf_ref[...], preferred_element_type=F32)
                  + jnp.dot(u, tb_ref[...], preferred_element_type=F32)
                  + jnp.dot(hin_scr[0].astype(BF16), qf_ref[...], preferred_element_type=F32)
                  + jnp.dot(hin_scr[1].astype(BF16), qb_ref[...], preferred_element_type=F32))


def _s5(su, ops, step, h0, *, nb, seq):
    nc = seq // S5_T
    nbp = max(nb, 8)
    lanes = S5_T * S5_GS
    x = su.reshape(nb, nc, S5_T, S5_GROUPS, S5_GS).transpose(3, 1, 0, 2, 4)
    if nbp != nb:
        x = jnp.pad(x, ((0, 0), (0, 0), (0, nbp - nb), (0, 0), (0, 0)))
        h0 = jnp.pad(h0, ((0, 0), (0, 0), (0, nbp - nb), (0, 0)))
    x = x.reshape(S5_GROUPS, nc * nbp, lanes).astype(BF16)
    rows = nc * nbp
    g3 = lambda a, b: pl.BlockSpec((None, a, b), lambda g: (g, 0, 0))
    y, hfin = pl.pallas_call(
        functools.partial(_s5_kernel, nc=nc, nb=nbp),
        grid=(S5_GROUPS,),
        in_specs=[g3(rows, lanes), g3(lanes, lanes), g3(lanes, lanes), g3(lanes, 2 * S5_N), g3(lanes, 2 * S5_N),
                  g3(2 * S5_N, lanes), g3(2 * S5_N, lanes), g3(4, 2 * S5_N),
                  pl.BlockSpec((None, 2, nbp, 2 * S5_N), lambda g: (g, 0, 0, 0))],
        out_specs=[g3(rows, lanes), pl.BlockSpec((None, 2, nbp, 2 * S5_N), lambda g: (g, 0, 0, 0))],
        out_shape=[jax.ShapeDtypeStruct((S5_GROUPS, rows, lanes), F32),
                   jax.ShapeDtypeStruct((S5_GROUPS, 2, nbp, 2 * S5_N), F32)],
        scratch_shapes=[pltpu.VMEM((2, rows, 2 * S5_N), F32), pltpu.VMEM((2, rows, 2 * S5_N), F32)],
        compiler_params=_cparams(1), name="s5",
    )(x, *ops, step, h0)
    y = y.reshape(S5_GROUPS, nc, nbp, S5_T, S5_GS)[:, :, :nb].transpose(2, 1, 3, 0, 4)
    return y.reshape(nb * seq, S5_W), hfin


def _glu_kernel(y_ref, su_ref, d_ref, w_ref, o_ref, wbf_ref):
    @pl.when(pl.program_id(0) == 0)
    def _():
        wbf_ref[...] = w_ref[...].astype(BF16)

    z = _gelu_tanh(y_ref[...] + d_ref[...] * su_ref[...])
    gate = jax.nn.sigmoid(jnp.dot(z.astype(BF16), wbf_ref[...], preferred_element_type=F32))
    o_ref[...] = (z * gate).astype(o_ref.dtype)


def _glu(y, su, d, w_glu, layer):
    row = pl.BlockSpec((TM, S5_W), lambda i: (i, 0))
    return pl.pallas_call(
        _glu_kernel,
        grid=(T_ALL // TM,),
        in_specs=[row, row, pl.BlockSpec((None, 1, S5_W), lambda i: (layer, 0, 0)),
                  pl.BlockSpec((None, S5_W, S5_W), lambda i: (layer, 0, 0))],
        out_specs=row,
        out_shape=jax.ShapeDtypeStruct((T_ALL, S5_W), BF16),
        scratch_shapes=[pltpu.VMEM((S5_W, S5_W), BF16)],
        compiler_params=_cparams(1), name="glu",
    )(y, su, d.reshape(DEPTH, 1, S5_W), w_glu)


def _merge_kernel(a_ref, b_ref, c_ref, ga_ref, gb_ref, gc_ref, wa_ref, wb_ref, wc_ref, o_ref, wbf_ref):
    @pl.when(pl.program_id(1) == 0)
    def _():
        wbf_ref[0] = wa_ref[...].astype(BF16)
        wbf_ref[1] = wb_ref[...].astype(BF16)
        wbf_ref[2] = wc_ref[...].astype(BF16)

    acc = ga_ref[...] * jnp.dot(a_ref[...], wbf_ref[0], preferred_element_type=F32)
    acc += gb_ref[...] * jnp.dot(b_ref[...], wbf_ref[1], preferred_element_type=F32)
    acc += gc_ref[...] * jnp.dot(c_ref[...], wbf_ref[2], preferred_element_type=F32)
    o_ref[...] = acc.astype(o_ref.dtype)


def _merge(o_na, o_gla, zz, gates, w_na, w_gla, w_s5, layer):
    tn = 1024
    ncb = D_MODEL // tn
    act = pl.BlockSpec((TM, NA_W), lambda j, i: (i, 0))
    gate = lambda k: pl.BlockSpec((TM, tn), lambda j, i: (i, k * ncb + j))
    w = pl.BlockSpec((None, NA_W, tn), lambda j, i: (layer, 0, j))
    return pl.pallas_call(
        _merge_kernel,
        grid=(ncb, T_ALL // TM),
        in_specs=[act, act, act, gate(0), gate(1), gate(2), w, w, w],
        out_specs=pl.BlockSpec((TM, tn), lambda j, i: (i, j)),
        out_shape=jax.ShapeDtypeStruct((T_ALL, D_MODEL), BF16),
        scratch_shapes=[pltpu.VMEM((3, NA_W, tn), BF16)],
        compiler_params=_cparams(2), name="merge",
    )(o_na, o_gla, zz, gates, gates, gates, w_na, w_gla, w_s5)


def _out_proj_kernel(m_ref, w_ref, x_ref, mod_ref, o_ref, wbf_ref, *, gate_row):
    @pl.when(pl.program_id(1) == 0)
    def _():
        wbf_ref[...] = w_ref[...].astype(BF16)

    y = jnp.dot(m_ref[...], wbf_ref[...], preferred_element_type=F32)
    o_ref[...] = x_ref[...] + mod_ref[gate_row:gate_row + 1, :] * y


def _out_proj(merged, w_out, x, mod, layer):
    tn = 1024
    return pl.pallas_call(
        functools.partial(_out_proj_kernel, gate_row=2),
        grid=(D_MODEL // tn, T_ALL // TM),
        in_specs=[pl.BlockSpec((TM, D_MODEL), lambda j, i: (i, 0)),
                  pl.BlockSpec((None, D_MODEL, tn), lambda j, i: (layer, 0, j)),
                  pl.BlockSpec((TM, tn), lambda j, i: (i, j)),
                  pl.BlockSpec((None, 6, tn), lambda j, i: (_group_of_tile(i, TM), 0, j))],
        out_specs=pl.BlockSpec((TM, tn), lambda j, i: (i, j)),
        out_shape=jax.ShapeDtypeStruct((T_ALL, D_MODEL), F32),
        scratch_shapes=[pltpu.VMEM((D_MODEL, tn), BF16)],
        compiler_params=_cparams(2), name="out_proj",
    )(merged, w_out, x, mod)


def _first_max(v, iota, n):
    m = jnp.max(v, axis=0, keepdims=True)
    first = jnp.min(jnp.where(v == m, iota, float(n)), axis=0, keepdims=True)
    return m, first


def _router_kernel(x_ref, wt_ref, bias_ref, idx_ref, w_ref):
    per = N_EXPERTS // N_GROUPS
    x_hi, x_lo = _split_bf16(x_ref[...])
    w_hi, w_lo = _split_bf16(wt_ref[...])
    logits = (lax.dot_general(w_hi, x_hi, NT, preferred_element_type=F32)
              + lax.dot_general(w_hi, x_lo, NT, preferred_element_type=F32)
              + lax.dot_general(w_lo, x_hi, NT, preferred_element_type=F32))
    scores = jax.nn.sigmoid(logits)
    sel = scores + bias_ref[...]
    tm = sel.shape[1]
    iota_g = lax.broadcasted_iota(jnp.int32, (per, tm), 0).astype(F32)
    grp_rows = []
    for g in range(N_GROUPS):
        v = sel[g * per:(g + 1) * per, :]
        m1, first = _first_max(v, iota_g, per)
        m2 = jnp.max(jnp.where(iota_g == first, -jnp.inf, v), axis=0, keepdims=True)
        grp_rows.append(m1 + m2)
    grp = jnp.concatenate(grp_rows, axis=0)
    iota_n = lax.broadcasted_iota(jnp.int32, (N_GROUPS, tm), 0).astype(F32)
    chosen = jnp.zeros((N_GROUPS, tm), F32)
    for _ in range(TOPK_GROUPS):
        _, first = _first_max(grp, iota_n, N_GROUPS)
        hit = iota_n == first
        chosen = jnp.where(hit, 1.0, chosen)
        grp = jnp.where(hit, -jnp.inf, grp)
    mask = jnp.concatenate([jnp.broadcast_to(chosen[g:g + 1, :], (per, tm)) for g in range(N_GROUPS)], axis=0)
    sel = jnp.where(mask > 0.5, sel, -jnp.inf)
    iota_e = lax.broadcasted_iota(jnp.int32, (N_EXPERTS, tm), 0).astype(F32)
    ids, ws = [], []
    for _ in range(TOP_K):
        _, first = _first_max(sel, iota_e, N_EXPERTS)
        hit = iota_e == first
        ids.append(first)
        ws.append(jnp.sum(jnp.where(hit, scores, 0.0), axis=0, keepdims=True))
        sel = jnp.where(hit, -jnp.inf, sel)
    w = jnp.concatenate(ws, axis=0)
    idx_ref[...] = jnp.concatenate(ids, axis=0).astype(jnp.int32)
    w_ref[...] = w / jnp.sum(w, axis=0, keepdims=True) * ROUTED_SCALE


def _router(h, router_w_t, router_bias):
    tm = 256
    return pl.pallas_call(
        _router_kernel,
        grid=(T_ALL // tm,),
        in_specs=[pl.BlockSpec((tm, D_MODEL), lambda i: (i, 0)),
                  pl.BlockSpec((N_EXPERTS, D_MODEL), lambda i: (0, 0)),
                  pl.BlockSpec((N_EXPERTS, 1), lambda i: (0, 0))],
        out_specs=[pl.BlockSpec((TOP_K, tm), lambda i: (0, i)), pl.BlockSpec((TOP_K, tm), lambda i: (0, i))],
        out_shape=[jax.ShapeDtypeStruct((TOP_K, T_ALL), jnp.int32), jax.ShapeDtypeStruct((TOP_K, T_ALL), F32)],
        compiler_params=_cparams(1), name="router",
    )(h, router_w_t, router_bias.reshape(N_EXPERTS, 1))


def _dispatch(idx, w):
    n_assign = T_ALL * TOP_K
    flat_e = idx.reshape(-1)
    experts = jnp.arange(N_EXPERTS, dtype=jnp.int32)
    sorted_e, order, sorted_w = lax.sort((flat_e, jnp.arange(n_assign, dtype=jnp.int32), w.reshape(-1)),
                                         num_keys=1, is_stable=True)
    counts = jnp.sum((flat_e[:, None] == experts[None, :]).astype(jnp.int32), axis=0)
    cnt_end = jnp.cumsum(counts)
    cnt_start = cnt_end - counts
    padded = (counts + MOE_MB - 1) // MOE_MB * MOE_MB
    pad_end = jnp.cumsum(padded)
    pad_start = pad_end - padded
    onehot = (sorted_e[:, None] == experts[None, :]).astype(jnp.int32)
    slot_sorted = jnp.arange(n_assign, dtype=jnp.int32) + jnp.sum(onehot * (pad_start - cnt_start)[None, :], axis=1)
    _, slot_of = lax.sort((order, slot_sorted), num_keys=1)
    blk_first = jnp.arange(MOE_NB, dtype=jnp.int32) * MOE_MB
    block_e = jnp.minimum(jnp.sum((pad_end[None, :] <= blk_first[:, None]).astype(jnp.int32), axis=1),
                          N_EXPERTS - 1)
    blk_onehot = (block_e[:, None] == experts[None, :]).astype(jnp.int32)
    pick = lambda v: jnp.sum(blk_onehot * v[None, :], axis=1)
    off = blk_first - pick(pad_start)
    n_valid = jnp.clip(pick(counts) - off, 0, MOE_MB)
    src = jnp.clip(pick(cnt_start) + off, 0, n_assign)
    take = lambda v: jax.vmap(lambda j0: lax.dynamic_slice(jnp.pad(v, (0, MOE_MB)), (j0,), (MOE_MB,)))(src)
    valid = jnp.arange(MOE_MB, dtype=jnp.int32)[None, :] < n_valid[:, None]
    slot_tok = jnp.where(valid, take(order // TOP_K), 0)
    slot_w = jnp.where(valid, take(sorted_w), 0.0)
    n_used = (pad_end[-1] // MOE_MB).astype(jnp.int32).reshape(1)
    return slot_tok, slot_w, slot_of.reshape(T_ALL, TOP_K), block_e.astype(jnp.int32), n_used


def _expert_kernel(be_ref, nu_ref, tok_ref, nxt_ref, x_hbm, wg_ref, wu_ref, wd_ref, sw_ref, o_ref,
                   xbuf, wg_bf, wu_bf, wd_bf, sems):
    i = pl.program_id(0)
    n_used = nu_ref[0]

    def gather(idx_ref, slot):
        def body(j, carry):
            pltpu.make_async_copy(x_hbm.at[pl.ds(idx_ref[0, 0, j], 1), :], xbuf.at[slot, pl.ds(j, 1), :],
                                  sems.at[slot]).start()
            return carry
        lax.fori_loop(0, MOE_MB, body, 0, unroll=8)

    @pl.when(i == 0)
    def _():
        gather(tok_ref, 0)

    @pl.when(i + 1 < n_used)
    def _():
        gather(nxt_ref, (i + 1) % 2)

    @pl.when(i < n_used)
    def _():
        @pl.when((i == 0) | (be_ref[i] != be_ref[jnp.maximum(i - 1, 0)]))
        def _():
            wg_bf[...] = wg_ref[...].astype(BF16)
            wu_bf[...] = wu_ref[...].astype(BF16)
            wd_bf[...] = wd_ref[...].astype(BF16)

        slot = i % 2
        pltpu.make_async_copy(x_hbm.at[pl.ds(0, MOE_MB), :], xbuf.at[slot], sems.at[slot]).wait()
        x = xbuf[slot].astype(BF16)
        g = jnp.dot(x, wg_bf[...], preferred_element_type=F32)
        u = jnp.dot(x, wu_bf[...], preferred_element_type=F32)
        a = (_silu(g) * u).astype(BF16)
        o_ref[...] = jnp.dot(a, wd_bf[...], preferred_element_type=F32) * sw_ref[...]

    @pl.when(i >= n_used)
    def _():
        o_ref[...] = jnp.zeros_like(o_ref)


def _experts(x, slot_tok, slot_w, block_e, n_used, wg, wu, wd, layer):
    tok = slot_tok.reshape(MOE_NB, 1, MOE_MB)
    tok_spec = lambda nxt: pl.BlockSpec((1, 1, MOE_MB), lambda i, be, nu: (jnp.minimum(i + nxt, MOE_NB - 1), 0, 0),
                                        memory_space=pltpu.SMEM)
    w_up = pl.BlockSpec((None, None, D_MODEL, F_EXPERT), lambda i, be, nu: (layer, be[i], 0, 0))
    return pl.pallas_call(
        _expert_kernel,
        grid_spec=pltpu.PrefetchScalarGridSpec(
            num_scalar_prefetch=2,
            grid=(MOE_NB,),
            in_specs=[tok_spec(0), tok_spec(1), pl.BlockSpec(memory_space=pl.ANY), w_up, w_up,
                      pl.BlockSpec((None, None, F_EXPERT, D_MODEL), lambda i, be, nu: (layer, be[i], 0, 0)),
                      pl.BlockSpec((MOE_MB, 1), lambda i, be, nu: (i, 0))],
            out_specs=pl.BlockSpec((MOE_MB, D_MODEL), lambda i, be, nu: (i, 0)),
            scratch_shapes=[pltpu.VMEM((2, MOE_MB, D_MODEL), F32),
                            pltpu.VMEM((D_MODEL, F_EXPERT), BF16), pltpu.VMEM((D_MODEL, F_EXPERT), BF16),
                            pltpu.VMEM((F_EXPERT, D_MODEL), BF16), pltpu.SemaphoreType.DMA((2,))]),
        out_shape=jax.ShapeDtypeStruct((MOE_SLOTS, D_MODEL), F32),
        compiler_params=_cparams(1), name="moe_experts",
    )(block_e, n_used, tok, tok, x, wg, wu, wd, slot_w.reshape(MOE_SLOTS, 1))


def _shared_up_kernel(x_ref, wg_ref, wu_ref, o_ref, wg_bf, wu_bf):
    @pl.when(pl.program_id(0) == 0)
    def _():
        wg_bf[...] = wg_ref[...].astype(BF16)
        wu_bf[...] = wu_ref[...].astype(BF16)

    x = x_ref[...].astype(BF16)
    g = jnp.dot(x, wg_bf[...], preferred_element_type=F32)
    u = jnp.dot(x, wu_bf[...], preferred_element_type=F32)
    o_ref[...] = (_silu(g) * u).astype(o_ref.dtype)


def _shared_up(h, wg, wu, layer):
    w = pl.BlockSpec((None, D_MODEL, F_SHARED), lambda i: (layer, 0, 0))
    return pl.pallas_call(
        _shared_up_kernel,
        grid=(T_ALL // TM,),
        in_specs=[pl.BlockSpec((TM, D_MODEL), lambda i: (i, 0)), w, w],
        out_specs=pl.BlockSpec((TM, F_SHARED), lambda i: (i, 0)),
        out_shape=jax.ShapeDtypeStruct((T_ALL, F_SHARED), BF16),
        scratch_shapes=[pltpu.VMEM((D_MODEL, F_SHARED), BF16), pltpu.VMEM((D_MODEL, F_SHARED), BF16)],
        compiler_params=_cparams(1), name="shared_up",
    )(h, wg, wu)


def _combine_kernel(slot_ref, nxt_ref, y_hbm, act_ref, wd_ref, x_ref, mod_ref, o_ref, buf, wd_bf, sems, *, gate_row):
    i = pl.program_id(0)
    n_rows = TOP_K * CMB_TM

    def gather(idx_ref, slot):
        def body(j, carry):
            pltpu.make_async_copy(y_hbm.at[pl.ds(idx_ref[0, 0, j], 1), :], buf.at[slot, pl.ds(j, 1), :],
                                  sems.at[slot]).start()
            return carry
        lax.fori_loop(0, n_rows, body, 0, unroll=8)

    @pl.when(i == 0)
    def _():
        wd_bf[...] = wd_ref[...].astype(BF16)
        gather(slot_ref, 0)

    @pl.when(i + 1 < pl.num_programs(0))
    def _():
        gather(nxt_ref, (i + 1) % 2)

    slot = i % 2
    y = jnp.dot(act_ref[...], wd_bf[...], preferred_element_type=F32)
    pltpu.make_async_copy(y_hbm.at[pl.ds(0, n_rows), :], buf.at[slot], sems.at[slot]).wait()
    for k in range(TOP_K):
        y = y + buf[slot, k * CMB_TM:(k + 1) * CMB_TM, :]
    o_ref[...] = x_ref[...] + mod_ref[gate_row:gate_row + 1, :] * y


def _combine(y_sorted, slot_of, act, sh_wd, x, mod, layer):
    nt = T_ALL // CMB_TM
    slots = slot_of.reshape(nt, CMB_TM, TOP_K).transpose(0, 2, 1).reshape(nt, 1, TOP_K * CMB_TM)
    slot_spec = lambda nxt: pl.BlockSpec((1, 1, TOP_K * CMB_TM), lambda i: (jnp.minimum(i + nxt, nt - 1), 0, 0),
                                         memory_space=pltpu.SMEM)
    return pl.pallas_call(
        functools.partial(_combine_kernel, gate_row=5),
        grid=(nt,),
        in_specs=[slot_spec(0), slot_spec(1),
                  pl.BlockSpec(memory_space=pl.ANY),
                  pl.BlockSpec((CMB_TM, F_SHARED), lambda i: (i, 0)),
                  pl.BlockSpec((None, F_SHARED, D_MODEL), lambda i: (layer, 0, 0)),
                  pl.BlockSpec((CMB_TM, D_MODEL), lambda i: (i, 0)),
                  pl.BlockSpec((None, 6, D_MODEL), lambda i: (_group_of_tile(i, CMB_TM), 0, 0))],
        out_specs=pl.BlockSpec((CMB_TM, D_MODEL), lambda i: (i, 0)),
        out_shape=jax.ShapeDtypeStruct((T_ALL, D_MODEL), F32),
        scratch_shapes=[pltpu.VMEM((2, TOP_K * CMB_TM, D_MODEL), F32), pltpu.VMEM((F_SHARED, D_MODEL), BF16),
                        pltpu.SemaphoreType.DMA((2,))],
        compiler_params=_cparams(1), name="moe_combine",
    )(slots, slots, y_sorted, act, sh_wd, x, mod)


def kernel(x_prompt, x_sample, cache_na_k, cache_na_v, state_gla, state_s5, c, c_ctx, ada_w, ada_b, norm1_g, norm2_g, w_in, na_rpb, gla_wa, gla_ba, gla_norm_g, s5_a_re, s5_a_im, s5_log_dt, s5_b_re, s5_b_im, s5_c_re, s5_c_im, s5_d, s5_w_glu, w_br_na, w_br_gla, w_br_s5, w_merge, w_out, router_w, router_bias, exp_wg, exp_wu, exp_wd, sh_wg, sh_wu, sh_wd, final_norm_g):
    x = jnp.concatenate([x_prompt.reshape(T_CTX, D_MODEL), x_sample.reshape(T_LAT, D_MODEL)], axis=0)
    cvec = jnp.concatenate([c_ctx[None, :], c, jnp.zeros((MOD_ROWS - N_MOD, D_MODEL), F32)], axis=0)
    mods = _ada(cvec, ada_w, ada_b).reshape(DEPTH, MOD_ROWS, 6, D_MODEL)
    cos, sin = _rope_tables()
    zero_gla = jnp.zeros((BATCH, 2, GLA_HEADS, GLA_DV, GLA_DK), F32)
    zero_s5 = jnp.zeros((S5_GROUPS, 2, BATCH, 2 * S5_N), F32)
    new_k, new_v, new_gla, new_s5 = [], [], [], []
    for l in range(DEPTH):
        mod = mods[l]
        h = _norm(x, norm1_g[l], mod, (0, 1), BF16)
        u = _mm(h, w_in, l, 1024, n=IN_MAIN)
        lr = _mm(h, w_in[l, :, IN_MAIN:IN_MAIN + 2 * GLA_LR], None, 2 * GLA_LR)
        su = _mm(h, w_in[l, :, IN_MAIN + 2 * GLA_LR:], None, S5_W)
        gates = _mm(h, w_merge, l, 1024, act="sigmoid")

        o_na = jnp.concatenate([_ctx_attn(u),
                                _na_attn(u, cache_na_k, cache_na_v, _na_bias(na_rpb[l]), l)], axis=0)

        s0_lat = state_gla[:, l].transpose(0, 1, 2, 4, 3)
        og_ctx, sfin = _gla(u, lr, gla_wa, gla_ba, gla_norm_g, zero_gla, cos, sin, l, latent=False)
        og_lat, _ = _gla(u, lr, gla_wa, gla_ba, gla_norm_g, s0_lat, cos, sin, l, latent=True)
        o_gla = jnp.concatenate([og_ctx, og_lat], axis=0)

        ops, step = _s5_operators(s5_a_re[l], s5_a_im[l], s5_log_dt[l], s5_b_re[l], s5_b_im[l],
                                  s5_c_re[l], s5_c_im[l])
        st = state_s5[:, l].astype(F32)
        h0_lat = jnp.concatenate([st[..., 0], st[..., 1]], axis=-1).transpose(2, 1, 0, 3)
        y_ctx, hfin = _s5(su[:T_CTX], ops, step, zero_s5, nb=BATCH, seq=SEQ)
        y_lat, _ = _s5(su[T_CTX:], ops, step, h0_lat, nb=DEC_BATCH, seq=DEC_SEQ)
        zz = _glu(jnp.concatenate([y_ctx, y_lat], axis=0), su, s5_d, s5_w_glu, l)

        merged = _merge(o_na, o_gla, zz, gates, w_br_na, w_br_gla, w_br_s5, l)
        x = _out_proj(merged, w_out, x, mod, l)

        h2 = _norm(x, norm2_g[l], mod, (3, 4), F32)
        idx_t, w_t = _router(h2, router_w[l].T, router_bias[l])
        slot_tok, slot_w, slot_of, block_e, n_used = _dispatch(idx_t.T, w_t.T)
        y_sorted = _experts(h2, slot_tok, slot_w, block_e, n_used, exp_wg, exp_wu, exp_wd, l)
        act = _shared_up(h2, sh_wg, sh_wu, l)
        x = _combine(y_sorted, slot_of, act, sh_wd, x, mod, l)

        new_k.append(u[:T_CTX, NA_W:2 * NA_W].reshape(BATCH, SEQ, NA_HEADS, NA_DH))
        new_v.append(u[:T_CTX, 2 * NA_W:3 * NA_W].reshape(BATCH, SEQ, NA_HEADS, NA_DH))
        new_gla.append(sfin.transpose(0, 1, 2, 4, 3))
        hf = hfin.transpose(2, 1, 0, 3)
        new_s5.append(jnp.stack([hf[..., :S5_N], hf[..., S5_N:]], axis=-1))

    y = _norm(x, final_norm_g, None, None, F32)
    return (y[:T_CTX].reshape(BATCH, SEQ, D_MODEL), y[T_CTX:].reshape(DEC_BATCH, DEC_SEQ, D_MODEL),
            jnp.stack(new_k, axis=1), jnp.stack(new_v, axis=1),
            jnp.stack(new_gla, axis=1), jnp.stack(new_s5, axis=1))
```

```python
import functools
import math

import jax
import jax.numpy as jnp
from jax import lax
from jax.experimental import pallas as pl
from jax.experimental.pallas import tpu as pltpu

F32 = jnp.float32
BF16 = jnp.bfloat16

D_MODEL = 2048
BATCH = 16
SEQ = 256
DEPTH = 2
DEC_BATCH = 2
DEC_SEQ = 1024
PAST_LEN = 512
GRID_W = 64
NA_HEADS = 8
NA_DH = 128
NA_W = NA_HEADS * NA_DH
NA_ROWS = 8
NA_COLS = 16
GLA_HEADS = 4
GLA_DK = 128
GLA_DV = 256
GLA_QK_W = GLA_HEADS * GLA_DK
GLA_V_W = GLA_HEADS * GLA_DV
GLA_LR = 16
GLA_TAU = 16.0
GLA_CHUNK = 64
GLA_SUB = 16
ROPE_BASE = 10000.0
S5_W = 1024
S5_GS = 16
S5_GROUPS = S5_W // S5_GS
S5_N = 64
S5_T = 16
IN_W = 3 * NA_W + 2 * GLA_QK_W + 2 * GLA_V_W + 2 * GLA_LR + S5_W
IN_MAIN = 3 * NA_W + 2 * GLA_QK_W + 2 * GLA_V_W
N_EXPERTS = 64
TOP_K = 8
N_GROUPS = 8
TOPK_GROUPS = 4
F_EXPERT = 512
F_SHARED = 512
ROUTED_SCALE = 2.5
EPS = 1e-6

T_CTX = BATCH * SEQ
T_LAT = DEC_BATCH * DEC_SEQ
T_ALL = T_CTX + T_LAT
N_MOD = 1 + DEC_BATCH
MOD_ROWS = 8

TM = 512
MOE_MB = 256
MOE_NB = T_ALL * TOP_K // MOE_MB + N_EXPERTS
MOE_SLOTS = MOE_NB * MOE_MB
CMB_TM = 128
NEG = -1e30

VMEM_LIMIT = 56 * 1024 * 1024

NT = (((1,), (1,)), ((), ()))
TN = (((0,), (0,)), ((), ()))


def _cparams(n_axes):
    return pltpu.CompilerParams(dimension_semantics=("arbitrary",) * n_axes,
                                vmem_limit_bytes=VMEM_LIMIT)


def _group_of_tile(i, tm):
    row = i * tm
    return jnp.where(row < T_CTX, 0, 1 + (row - T_CTX) // DEC_SEQ)


def _silu(x):
    return x * jax.nn.sigmoid(x)


def _gelu_tanh(x):
    return 0.5 * x * (1.0 + jnp.tanh(math.sqrt(2.0 / math.pi) * (x + 0.044715 * (x * x * x))))


def _ada_kernel(c_ref, w_ref, b_ref, o_ref):
    s = _silu(c_ref[...]).astype(BF16)
    o_ref[...] = jnp.dot(s, w_ref[...].astype(BF16), preferred_element_type=F32) + b_ref[...]


def _ada(cvec, ada_w, ada_b):
    tn = 1024
    return pl.pallas_call(
        _ada_kernel,
        grid=(DEPTH, 6 * D_MODEL // tn),
        in_specs=[pl.BlockSpec((MOD_ROWS, D_MODEL), lambda l, j: (0, 0)),
                  pl.BlockSpec((None, D_MODEL, tn), lambda l, j: (l, 0, j)),
                  pl.BlockSpec((None, 1, tn), lambda l, j: (l, 0, j))],
        out_specs=pl.BlockSpec((None, MOD_ROWS, tn), lambda l, j: (l, 0, j)),
        out_shape=jax.ShapeDtypeStruct((DEPTH, MOD_ROWS, 6 * D_MODEL), F32),
        compiler_params=_cparams(2), name="ada",
    )(cvec, ada_w, ada_b.reshape(DEPTH, 1, 6 * D_MODEL))


def _norm_kernel(x_ref, g_ref, *rest, rows):
    o_ref = rest[-1]
    x = x_ref[...]
    y = x * lax.rsqrt(jnp.mean(x * x, axis=-1, keepdims=True) + EPS) * g_ref[...]
    if rows is not None:
        mod_ref = rest[0]
        y = y * (1.0 + mod_ref[rows[1]:rows[1] + 1, :]) + mod_ref[rows[0]:rows[0] + 1, :]
    o_ref[...] = y.astype(o_ref.dtype)


def _norm(x, g, mod, rows, out_dtype):
    tm = 256
    in_specs = [pl.BlockSpec((tm, D_MODEL), lambda i: (i, 0)),
                pl.BlockSpec((1, D_MODEL), lambda i: (0, 0))]
    args = [x, g.reshape(1, D_MODEL)]
    if rows is not None:
        in_specs.append(pl.BlockSpec((None, 6, D_MODEL), lambda i: (_group_of_tile(i, tm), 0, 0)))
        args.append(mod)
    return pl.pallas_call(
        functools.partial(_norm_kernel, rows=rows),
        grid=(T_ALL // tm,),
        in_specs=in_specs,
        out_specs=pl.BlockSpec((tm, D_MODEL), lambda i: (i, 0)),
        out_shape=jax.ShapeDtypeStruct((T_ALL, D_MODEL), out_dtype),
        compiler_params=_cparams(1), name="norm",
    )(*args)


def _mm_kernel(x_ref, w_ref, o_ref, wbf_ref, *, act):
    @pl.when(pl.program_id(1) == 0)
    def _():
        wbf_ref[...] = w_ref[...].astype(BF16)

    acc = jnp.dot(x_ref[...].astype(BF16), wbf_ref[...], preferred_element_type=F32)
    if act == "sigmoid":
        acc = jax.nn.sigmoid(acc)
    o_ref[...] = acc.astype(o_ref.dtype)


def _mm(x, w, layer, tn, act=None, out_dtype=F32, n=None):
    k = x.shape[1]
    n = w.shape[-1] if n is None else n
    if layer is None:
        w_spec = pl.BlockSpec((k, tn), lambda j, i: (0, j))
    else:
        w_spec = pl.BlockSpec((None, k, tn), lambda j, i: (layer, 0, j))
    return pl.pallas_call(
        functools.partial(_mm_kernel, act=act),
        grid=(n // tn, T_ALL // TM),
        in_specs=[pl.BlockSpec((TM, k), lambda j, i: (i, 0)), w_spec],
        out_specs=pl.BlockSpec((TM, tn), lambda j, i: (i, j)),
        out_shape=jax.ShapeDtypeStruct((T_ALL, n), out_dtype),
        scratch_shapes=[pltpu.VMEM((k, tn), BF16)],
        compiler_params=_cparams(2), name="mm",
    )(x, w)


def _ctx_attn_kernel(q_ref, k_ref, v_ref, o_ref):
    scale = NA_DH ** -0.5
    for h in range(NA_HEADS):
        sl = slice(h * NA_DH, (h + 1) * NA_DH)
        q = q_ref[:, sl].astype(BF16)
        k = k_ref[:, sl].astype(BF16)
        v = v_ref[:, sl].astype(BF16)
        s = lax.dot_general(q, k, NT, preferred_element_type=F32) * scale
        p = jnp.exp(s - jnp.max(s, axis=-1, keepdims=True))
        o = jnp.dot(p.astype(BF16), v, preferred_element_type=F32) / jnp.sum(p, axis=-1, keepdims=True)
        o_ref[:, sl] = o.astype(o_ref.dtype)


def _ctx_attn(u):
    spec = lambda cb: pl.BlockSpec((SEQ, NA_W), lambda b: (b, cb))
    return pl.pallas_call(
        _ctx_attn_kernel,
        grid=(BATCH,),
        in_specs=[spec(0), spec(1), spec(2)],
        out_specs=pl.BlockSpec((SEQ, NA_W), lambda b: (b, 0)),
        out_shape=jax.ShapeDtypeStruct((T_CTX, NA_W), BF16),
        compiler_params=_cparams(1), name="ctx_attn",
    )(u, u, u)


NA_GRID_ROWS = DEC_SEQ // GRID_W
NA_KR = min(NA_ROWS, NA_GRID_ROWS)
NA_LOC = NA_KR * GRID_W


def _na_bias(rpb):
    pad = GRID_W - NA_COLS
    p = jnp.pad(rpb, ((0, 0), (0, 0), (pad, pad)))
    band = jnp.stack([p[:, :, GRID_W - 1 - qc:2 * GRID_W - 1 - qc] for qc in range(GRID_W)], axis=2)
    qc = jnp.arange(GRID_W)[:, None]
    kc = jnp.arange(GRID_W)[None, :]
    win = jnp.clip(qc - NA_COLS // 2, 0, GRID_W - NA_COLS)
    valid = (kc >= win) & (kc < win + NA_COLS)
    band = jnp.where(valid[None, None], band, NEG)
    rows = []
    for r in range(NA_GRID_ROWS):
        lo = min(max(r - NA_KR // 2, 0), NA_GRID_ROWS - NA_KR) - r + NA_ROWS - 1
        rows.append(band[:, lo:lo + NA_KR].transpose(0, 2, 1, 3).reshape(NA_HEADS, GRID_W, NA_LOC))
    return jnp.stack(rows, axis=1)


def _na_kernel(q_ref, k_ref, v_ref, ck_ref, cv_ref, bias_ref, o_ref):
    scale = NA_DH ** -0.5
    r = pl.program_id(2)
    start = pl.multiple_of(jnp.clip(r - NA_KR // 2, 0, NA_GRID_ROWS - NA_KR) * GRID_W, GRID_W)
    q = q_ref[...].astype(BF16)
    kl = k_ref[pl.ds(start, NA_LOC), :].astype(BF16)
    vl = v_ref[pl.ds(start, NA_LOC), :].astype(BF16)
    s1 = lax.dot_general(q, kl, NT, preferred_element_type=F32) * scale + bias_ref[...]
    s2 = lax.dot_general(q, ck_ref[...].astype(BF16), NT, preferred_element_type=F32) * scale
    m = jnp.maximum(jnp.max(s1, axis=-1, keepdims=True), jnp.max(s2, axis=-1, keepdims=True))
    p1 = jnp.exp(s1 - m)
    p2 = jnp.exp(s2 - m)
    den = jnp.sum(p1, axis=-1, keepdims=True) + jnp.sum(p2, axis=-1, keepdims=True)
    o = (jnp.dot(p1.astype(BF16), vl, preferred_element_type=F32)
         + jnp.dot(p2.astype(BF16), cv_ref[...].astype(BF16), preferred_element_type=F32))
    o_ref[...] = (o / den).astype(o_ref.dtype)


def _na_attn(u, cache_k, cache_v, bias, layer):
    lat_rb = T_CTX // GRID_W
    lat_sb = T_CTX // DEC_SEQ
    ck = cache_k.reshape(DEC_BATCH, DEPTH, PAST_LEN, NA_W)
    cv = cache_v.reshape(DEC_BATCH, DEPTH, PAST_LEN, NA_W)
    kv_spec = lambda cb: pl.BlockSpec((DEC_SEQ, NA_DH), lambda b, h, r: (lat_sb + b, cb * NA_HEADS + h))
    c_spec = pl.BlockSpec((None, None, PAST_LEN, NA_DH), lambda b, h, r: (b, layer, 0, h))
    return pl.pallas_call(
        _na_kernel,
        grid=(DEC_BATCH, NA_HEADS, NA_GRID_ROWS),
        in_specs=[pl.BlockSpec((GRID_W, NA_DH), lambda b, h, r: (lat_rb + b * NA_GRID_ROWS + r, h)),
                  kv_spec(1), kv_spec(2), c_spec, c_spec,
                  pl.BlockSpec((None, None, GRID_W, NA_LOC), lambda b, h, r: (h, r, 0, 0))],
        out_specs=pl.BlockSpec((GRID_W, NA_DH), lambda b, h, r: (b * NA_GRID_ROWS + r, h)),
        out_shape=jax.ShapeDtypeStruct((T_LAT, NA_W), BF16),
        compiler_params=_cparams(3), name="na_attn",
    )(u, u, u, ck, cv, bias)


def _rope_tables():
    half = GLA_DK // 2
    nf = half // 2
    t = jnp.arange(DEC_SEQ)
    freqs = ROPE_BASE ** (-jnp.arange(nf, dtype=F32) / nf)
    ang_r = (t // GRID_W).astype(F32)[:, None] * freqs
    ang_c = (t % GRID_W).astype(F32)[:, None] * freqs
    cos = jnp.concatenate([jnp.cos(ang_r), jnp.cos(ang_r), jnp.cos(ang_c), jnp.cos(ang_c)], axis=-1)
    sin = jnp.concatenate([-jnp.sin(ang_r), jnp.sin(ang_r), -jnp.sin(ang_c), jnp.sin(ang_c)], axis=-1)
    return cos, sin


def _split_bf16(x):
    hi = x.astype(BF16)
    return hi, (x - hi.astype(F32)).astype(BF16)


def _gla_kernel(q_ref, k_ref, v_ref, gg_ref, lr_ref, wa_ref, ba_ref, ng_ref, cos_ref, sin_ref, s0_ref,
                o_ref, sfin_ref, qs, ks, las, o_acc, st, *, seq, rope):
    nc = seq // GLA_CHUNK
    nq = GLA_DK // 4

    def rot(x):
        lane = lax.broadcasted_iota(jnp.int32, x.shape, 1)
        partner = jnp.where((lane % (2 * nq)) < nq,
                            pltpu.roll(x, GLA_DK - nq, axis=1), pltpu.roll(x, nq, axis=1))
        return x * cos_ref[...] + partner * sin_ref[...]

    q = q_ref[...] * (GLA_DK ** -0.5)
    k = k_ref[...]
    if rope:
        q = rot(q)
        k = rot(k)
    qs[...] = q
    ks[...] = k
    lr = lr_ref[...]
    for d in range(2):
        z = jnp.dot(lr[:, d * GLA_LR:(d + 1) * GLA_LR].astype(BF16), wa_ref[d].astype(BF16),
                    preferred_element_type=F32) + ba_ref[d:d + 1, :]
        las[d] = -(jnp.maximum(-z, 0.0) + jnp.log1p(jnp.exp(-jnp.abs(z)))) / GLA_TAU
        st[d] = s0_ref[d]

    row = lax.broadcasted_iota(jnp.int32, (GLA_CHUNK, GLA_CHUNK), 0)
    col = lax.broadcasted_iota(jnp.int32, (GLA_CHUNK, GLA_CHUNK), 1)
    key_row = lax.broadcasted_iota(jnp.int32, (GLA_CHUNK, 1), 0)

    def chunk(c, d):
        rev = d == 1
        rows = pl.ds(pl.multiple_of(c * GLA_CHUNK, GLA_CHUNK), GLA_CHUNK)
        qc = qs[rows, :]
        kc = ks[rows, :]
        vc = v_ref[rows, :].astype(BF16)
        la = las[d, rows, :]
        causal = (col >= row) if rev else (col <= row)
        tri = jnp.where(causal, 1.0, 0.0).astype(BF16)
        la_hi, la_lo = _split_bf16(la)
        b = (jnp.dot(tri, la_hi, preferred_element_type=F32)
             + jnp.dot(tri, la_lo, preferred_element_type=F32))
        bex = b - la
        b_last = b[0:1, :] if rev else b[GLA_CHUNK - 1:GLA_CHUNK, :]
        blocks = []
        for i in range(GLA_CHUNK // GLA_SUB):
            lo, hi = i * GLA_SUB, (i + 1) * GLA_SUB
            ref = bex[hi - 1:hi, :] if rev else bex[lo:lo + 1, :]
            qt = (qc[lo:hi, :] * jnp.exp(b[lo:hi, :] - ref)).astype(BF16)
            allowed = (key_row >= lo) if rev else (key_row < hi)
            kt = (kc * jnp.exp(jnp.where(allowed, ref - b, -jnp.inf))).astype(BF16)
            blocks.append(lax.dot_general(qt, kt, NT, preferred_element_type=F32))
        att = jnp.where(causal, jnp.concatenate(blocks, axis=0), 0.0)
        s_t = st[d]
        o = (jnp.dot(att.astype(BF16), vc, preferred_element_type=F32)
             + lax.dot_general((qc * jnp.exp(b)).astype(BF16), s_t.astype(BF16), NT,
                               preferred_element_type=F32))
        khat = (kc * jnp.exp(b_last - b)).astype(BF16)
        st[d] = s_t * jnp.exp(b_last) + lax.dot_general(vc, khat, TN, preferred_element_type=F32)
        o_acc[d, rows, :] = o

    def both(c, carry):
        chunk(c, 0)
        chunk(nc - 1 - c, 1)
        return carry

    lax.fori_loop(0, nc, both, 0)
    sfin_ref[...] = st[...]
    o = o_acc[0] + o_acc[1]
    o = o * lax.rsqrt(jnp.mean(o * o, axis=-1, keepdims=True) + EPS) * ng_ref[...]
    o_ref[...] = (o * _silu(gg_ref[...])).astype(o_ref.dtype)


def _gla(u, lr, wa, ba, norm_g, s0_t, cos, sin, layer, *, latent):
    seq, nb, first = (DEC_SEQ, DEC_BATCH, T_CTX // DEC_SEQ) if latent else (SEQ, BATCH, 0)
    qk_cb = 3 * NA_W // GLA_DK
    v_cb = (3 * NA_W + 2 * GLA_QK_W) // GLA_DV
    row = lambda w, cb: pl.BlockSpec((seq, w), lambda b, h: (first + b, cb + h))
    return pl.pallas_call(
        functools.partial(_gla_kernel, seq=seq, rope=latent),
        grid=(nb, GLA_HEADS),
        in_specs=[row(GLA_DK, qk_cb), row(GLA_DK, qk_cb + GLA_HEADS), row(GLA_DV, v_cb),
                  row(GLA_DV, v_cb + GLA_HEADS),
                  pl.BlockSpec((seq, 2 * GLA_LR), lambda b, h: (first + b, 0)),
                  pl.BlockSpec((None, 2, GLA_LR, GLA_DK), lambda b, h: (layer, 0, 0, h)),
                  pl.BlockSpec((None, 2, GLA_DK), lambda b, h: (layer, 0, h)),
                  pl.BlockSpec((None, 1, GLA_DV), lambda b, h: (layer, 0, h)),
                  pl.BlockSpec((seq, GLA_DK), lambda b, h: (0, 0)),
                  pl.BlockSpec((seq, GLA_DK), lambda b, h: (0, 0)),
                  pl.BlockSpec((None, 2, None, GLA_DV, GLA_DK), lambda b, h: (b, 0, h, 0, 0))],
        out_specs=[pl.BlockSpec((seq, GLA_DV), lambda b, h: (b, h)),
                   pl.BlockSpec((None, 2, None, GLA_DV, GLA_DK), lambda b, h: (b, 0, h, 0, 0))],
        out_shape=[jax.ShapeDtypeStruct((nb * seq, GLA_V_W), BF16),
                   jax.ShapeDtypeStruct((nb, 2, GLA_HEADS, GLA_DV, GLA_DK), F32)],
        scratch_shapes=[pltpu.VMEM((seq, GLA_DK), F32), pltpu.VMEM((seq, GLA_DK), F32),
                        pltpu.VMEM((2, seq, GLA_DK), F32), pltpu.VMEM((2, seq, GLA_DV), F32),
                        pltpu.VMEM((2, GLA_DV, GLA_DK), F32)],
        compiler_params=_cparams(2), name="gla",
    )(u, u, u, u, lr, wa, ba, norm_g.reshape(DEPTH, 1, GLA_V_W), cos[:seq], sin[:seq], s0_t)


def _s5_operators(a_re, a_im, log_dt, b_re, b_im, c_re, c_im):
    hp = lax.Precision.HIGHEST
    lam = lax.complex(a_re.astype(F32), a_im.astype(F32))
    lam_dt = lam * jnp.exp(log_dt.astype(F32))[..., None]
    a_bar = jnp.exp(lam_dt)
    b_bar = ((a_bar - 1.0) / lam)[..., None] * lax.complex(b_re.astype(F32), b_im.astype(F32))
    cc = lax.complex(c_re.astype(F32), c_im.astype(F32))
    taus = jnp.arange(S5_T + 1, dtype=F32)
    pw = jnp.exp(lam_dt[:, :, None, :] * taus[None, None, :, None])
    kern = jnp.real(jnp.einsum("dgjn,dgtn,dgni->dgtji", cc, pw[:, :, :S5_T], b_bar, precision=hp))
    lanes = S5_T * S5_GS
    zpad = lambda a, before, after: jnp.pad(a, ((0, 0), (before, after), (0, 0), (0, 0)))

    def toeplitz(rows):
        return jnp.stack(rows, axis=1).transpose(0, 1, 4, 2, 3).reshape(S5_GROUPS, lanes, lanes)

    toep_f = toeplitz([zpad(kern[0][:, :S5_T - s], s, 0) for s in range(S5_T)])
    toep_b = toeplitz([zpad(kern[1][:, s::-1], 0, S5_T - 1 - s) for s in range(S5_T)])

    def state_in(pwd, bd):
        m = pwd[:, :, None, :] * bd.transpose(0, 2, 1)[:, None, :, :]
        m = m.reshape(S5_GROUPS, lanes, S5_N)
        return jnp.concatenate([jnp.real(m), jnp.imag(m)], axis=-1)

    p_f = state_in(pw[0, :, S5_T - 1::-1][:, :S5_T], b_bar[0])
    p_b = state_in(pw[1, :, :S5_T], b_bar[1])

    def state_out(pwd, cd):
        m = pwd[:, :, None, :] * cd[:, None, :, :]
        m = m.transpose(0, 3, 1, 2).reshape(S5_GROUPS, S5_N, lanes)
        return jnp.concatenate([jnp.real(m), -jnp.imag(m)], axis=1)

    q_f = state_out(pw[0, :, 1:S5_T + 1], cc[0])
    q_b = state_out(pw[1, :, S5_T:0:-1], cc[1])
    ops = [m.astype(BF16) for m in (toep_f, toep_b, p_f, p_b, q_f, q_b)]
    a_t = pw[:, :, S5_T]
    ar, ai = jnp.real(a_t), jnp.imag(a_t)
    step = jnp.stack([jnp.concatenate([ar[0], ar[0]], -1), jnp.concatenate([-ai[0], ai[0]], -1),
                      jnp.concatenate([ar[1], ar[1]], -1), jnp.concatenate([-ai[1], ai[1]], -1)], axis=1)
    return ops, step


def _s5_kernel(u_ref, tf_ref, tb_ref, pf_ref, pb_ref, qf_ref, qb_ref, step_ref, h0_ref,
               y_ref, hfin_ref, e_scr, hin_scr, *, nc, nb):
    u = u_ref[...]
    re, im = slice(0, S5_N), slice(S5_N, 2 * S5_N)
    for d, p_ref in enumerate((pf_ref, pb_ref)):
        e = jnp.dot(u, p_ref[...], preferred_element_type=F32)
        e_scr[2 * d] = e[:, re]
        e_scr[2 * d + 1] = e[:, im]
    ar = [step_ref[0:1, re], step_ref[2:3, re]]
    ai = [step_ref[1:2, im], step_ref[3:4, im]]
    hr = [h0_ref[0][:, re], h0_ref[1][:, re]]
    hi = [h0_ref[0][:, im], h0_ref[1][:, im]]
    for c in range(nc):
        for d in range(2):
            rows = slice(c * nb, (c + 1) * nb) if d == 0 else slice((nc - 1 - c) * nb, (nc - c) * nb)
            hin_scr[2 * d, rows, :] = hr[d]
            hin_scr[2 * d + 1, rows, :] = hi[d]
            hr[d], hi[d] = (hr[d] * ar[d] - hi[d] * ai[d] + e_scr[2 * d, rows, :],
                            hi[d] * ar[d] + hr[d] * ai[d] + e_scr[2 * d + 1, rows, :])
    for d in range(2):
        hfin_ref[d, :, re] = hr[d]
        hfin_ref[d, :, im] = hi[d]
    y = jnp.dot(u, tf_ref[...], preferred_element_type=F32) + jnp.dot(u, tb_ref[...], preferred_element_type=F32)
    for d, q_ref in enumerate((qf_ref, qb_ref)):
        y += jnp.dot(hin_scr[2 * d].astype(BF16), q_ref[re, :], preferred_element_type=F32)
        y += jnp.dot(hin_scr[2 * d + 1].astype(BF16), q_ref[im, :], preferred_element_type=F32)
    y_ref[...] = y


def _s5(su, ops, step, h0, *, nb, seq):
    nc = seq // S5_T
    nbp = max(nb, 8)
    lanes = S5_T * S5_GS
    x = su.reshape(nb, nc, S5_T, S5_GROUPS, S5_GS).transpose(3, 1, 0, 2, 4)
    if nbp != nb:
        x = jnp.pad(x, ((0, 0), (0, 0), (0, nbp - nb), (0, 0), (0, 0)))
        h0 = jnp.pad(h0, ((0, 0), (0, 0), (0, nbp - nb), (0, 0)))
    x = x.reshape(S5_GROUPS, nc * nbp, lanes).astype(BF16)
    rows = nc * nbp
    g3 = lambda a, b: pl.BlockSpec((None, a, b), lambda g: (g, 0, 0))
    y, hfin = pl.pallas_call(
        functools.partial(_s5_kernel, nc=nc, nb=nbp),
        grid=(S5_GROUPS,),
        in_specs=[g3(rows, lanes), g3(lanes, lanes), g3(lanes, lanes), g3(lanes, 2 * S5_N), g3(lanes, 2 * S5_N),
                  g3(2 * S5_N, lanes), g3(2 * S5_N, lanes), g3(4, 2 * S5_N),
                  pl.BlockSpec((None, 2, nbp, 2 * S5_N), lambda g: (g, 0, 0, 0))],
        out_specs=[g3(rows, lanes), pl.BlockSpec((None, 2, nbp, 2 * S5_N), lambda g: (g, 0, 0, 0))],
        out_shape=[jax.ShapeDtypeStruct((S5_GROUPS, rows, lanes), F32),
                   jax.ShapeDtypeStruct((S5_GROUPS, 2, nbp, 2 * S5_N), F32)],
        scratch_shapes=[pltpu.VMEM((4, rows, S5_N), F32), pltpu.VMEM((4, rows, S5_N), F32)],
        compiler_params=_cparams(1), name="s5",
    )(x, *ops, step, h0)
    y = y.reshape(S5_GROUPS, nc, nbp, S5_T, S5_GS)[:, :, :nb].transpose(2, 1, 3, 0, 4)
    return y.reshape(nb * seq, S5_W), hfin


def _glu_kernel(y_ref, su_ref, d_ref, w_ref, o_ref, wbf_ref):
    @pl.when(pl.program_id(0) == 0)
    def _():
        wbf_ref[...] = w_ref[...].astype(BF16)

    z = _gelu_tanh(y_ref[...] + d_ref[...] * su_ref[...])
    gate = jax.nn.sigmoid(jnp.dot(z.astype(BF16), wbf_ref[...], preferred_element_type=F32))
    o_ref[...] = (z * gate).astype(o_ref.dtype)


def _glu(y, su, d, w_glu, layer):
    row = pl.BlockSpec((TM, S5_W), lambda i: (i, 0))
    return pl.pallas_call(
        _glu_kernel,
        grid=(T_ALL // TM,),
        in_specs=[row, row, pl.BlockSpec((None, 1, S5_W), lambda i: (layer, 0, 0)),
                  pl.BlockSpec((None, S5_W, S5_W), lambda i: (layer, 0, 0))],
        out_specs=row,
        out_shape=jax.ShapeDtypeStruct((T_ALL, S5_W), BF16),
        scratch_shapes=[pltpu.VMEM((S5_W, S5_W), BF16)],
        compiler_params=_cparams(1), name="glu",
    )(y, su, d.reshape(DEPTH, 1, S5_W), w_glu)


def _merge_kernel(a_ref, b_ref, c_ref, ga_ref, gb_ref, gc_ref, wa_ref, wb_ref, wc_ref, o_ref, wbf_ref):
    @pl.when(pl.program_id(1) == 0)
    def _():
        wbf_ref[0] = wa_ref[...].astype(BF16)
        wbf_ref[1] = wb_ref[...].astype(BF16)
        wbf_ref[2] = wc_ref[...].astype(BF16)

    acc = ga_ref[...] * jnp.dot(a_ref[...], wbf_ref[0], preferred_element_type=F32)
    acc += gb_ref[...] * jnp.dot(b_ref[...], wbf_ref[1], preferred_element_type=F32)
    acc += gc_ref[...] * jnp.dot(c_ref[...], wbf_ref[2], preferred_element_type=F32)
    o_ref[...] = acc.astype(o_ref.dtype)


def _merge(o_na, o_gla, zz, gates, w_na, w_gla, w_s5, layer):
    tn = 1024
    ncb = D_MODEL // tn
    act = pl.BlockSpec((TM, NA_W), lambda j, i: (i, 0))
    gate = lambda k: pl.BlockSpec((TM, tn), lambda j, i: (i, k * ncb + j))
    w = pl.BlockSpec((None, NA_W, tn), lambda j, i: (layer, 0, j))
    return pl.pallas_call(
        _merge_kernel,
        grid=(ncb, T_ALL // TM),
        in_specs=[act, act, act, gate(0), gate(1), gate(2), w, w, w],
        out_specs=pl.BlockSpec((TM, tn), lambda j, i: (i, j)),
        out_shape=jax.ShapeDtypeStruct((T_ALL, D_MODEL), BF16),
        scratch_shapes=[pltpu.VMEM((3, NA_W, tn), BF16)],
        compiler_params=_cparams(2), name="merge",
    )(o_na, o_gla, zz, gates, gates, gates, w_na, w_gla, w_s5)


def _out_proj_kernel(m_ref, w_ref, x_ref, mod_ref, o_ref, wbf_ref, *, gate_row):
    @pl.when(pl.program_id(1) == 0)
    def _():
        wbf_ref[...] = w_ref[...].astype(BF16)

    y = jnp.dot(m_ref[...], wbf_ref[...], preferred_element_type=F32)
    o_ref[...] = x_ref[...] + mod_ref[gate_row:gate_row + 1, :] * y


def _out_proj(merged, w_out, x, mod, layer):
    tn = 1024
    return pl.pallas_call(
        functools.partial(_out_proj_kernel, gate_row=2),
        grid=(D_MODEL // tn, T_ALL // TM),
        in_specs=[pl.BlockSpec((TM, D_MODEL), lambda j, i: (i, 0)),
                  pl.BlockSpec((None, D_MODEL, tn), lambda j, i: (layer, 0, j)),
                  pl.BlockSpec((TM, tn), lambda j, i: (i, j)),
                  pl.BlockSpec((None, 6, tn), lambda j, i: (_group_of_tile(i, TM), 0, j))],
        out_specs=pl.BlockSpec((TM, tn), lambda j, i: (i, j)),
        out_shape=jax.ShapeDtypeStruct((T_ALL, D_MODEL), F32),
        scratch_shapes=[pltpu.VMEM((D_MODEL, tn), BF16)],
        compiler_params=_cparams(2), name="out_proj",
    )(merged, w_out, x, mod)


def _first_max(v, iota, n):
    m = jnp.max(v, axis=0, keepdims=True)
    first = jnp.min(jnp.where(v == m, iota, float(n)), axis=0, keepdims=True)
    return m, first


def _router_kernel(x_ref, wt_ref, bias_ref, idx_ref, w_ref):
    per = N_EXPERTS // N_GROUPS
    x_hi, x_lo = _split_bf16(x_ref[...])
    w_hi, w_lo = _split_bf16(wt_ref[...])
    logits = (lax.dot_general(w_hi, x_hi, NT, preferred_element_type=F32)
              + lax.dot_general(w_hi, x_lo, NT, preferred_element_type=F32)
              + lax.dot_general(w_lo, x_hi, NT, preferred_element_type=F32))
    scores = jax.nn.sigmoid(logits)
    sel = scores + bias_ref[...]
    tm = sel.shape[1]
    iota_g = lax.broadcasted_iota(jnp.int32, (per, tm), 0).astype(F32)
    grp_rows = []
    for g in range(N_GROUPS):
        v = sel[g * per:(g + 1) * per, :]
        m1, first = _first_max(v, iota_g, per)
        m2 = jnp.max(jnp.where(iota_g == first, -jnp.inf, v), axis=0, keepdims=True)
        grp_rows.append(m1 + m2)
    grp = jnp.concatenate(grp_rows, axis=0)
    iota_n = lax.broadcasted_iota(jnp.int32, (N_GROUPS, tm), 0).astype(F32)
    chosen = jnp.zeros((N_GROUPS, tm), F32)
    for _ in range(TOPK_GROUPS):
        _, first = _first_max(grp, iota_n, N_GROUPS)
        hit = iota_n == first
        chosen = jnp.where(hit, 1.0, chosen)
        grp = jnp.where(hit, -jnp.inf, grp)
    mask = jnp.concatenate([jnp.broadcast_to(chosen[g:g + 1, :], (per, tm)) for g in range(N_GROUPS)], axis=0)
    sel = jnp.where(mask > 0.5, sel, -jnp.inf)
    iota_e = lax.broadcasted_iota(jnp.int32, (N_EXPERTS, tm), 0).astype(F32)
    ids, ws = [], []
    for _ in range(TOP_K):
        _, first = _first_max(sel, iota_e, N_EXPERTS)
        hit = iota_e == first
        ids.append(first)
        ws.append(jnp.sum(jnp.where(hit, scores, 0.0), axis=0, keepdims=True))
        sel = jnp.where(hit, -jnp.inf, sel)
    w = jnp.concatenate(ws, axis=0)
    idx_ref[...] = jnp.concatenate(ids, axis=0).astype(jnp.int32)
    w_ref[...] = w / jnp.sum(w, axis=0, keepdims=True) * ROUTED_SCALE


def _router(h, router_w_t, router_bias):
    tm = 256
    return pl.pallas_call(
        _router_kernel,
        grid=(T_ALL // tm,),
        in_specs=[pl.BlockSpec((tm, D_MODEL), lambda i: (i, 0)),
                  pl.BlockSpec((N_EXPERTS, D_MODEL), lambda i: (0, 0)),
                  pl.BlockSpec((N_EXPERTS, 1), lambda i: (0, 0))],
        out_specs=[pl.BlockSpec((TOP_K, tm), lambda i: (0, i)), pl.BlockSpec((TOP_K, tm), lambda i: (0, i))],
        out_shape=[jax.ShapeDtypeStruct((TOP_K, T_ALL), jnp.int32), jax.ShapeDtypeStruct((TOP_K, T_ALL), F32)],
        compiler_params=_cparams(1), name="router",
    )(h, router_w_t, router_bias.reshape(N_EXPERTS, 1))


def _dispatch(idx):
    n_assign = T_ALL * TOP_K
    flat_e = idx.reshape(-1)
    experts = jnp.arange(N_EXPERTS, dtype=jnp.int32)
    sorted_e, order = lax.sort((flat_e, jnp.arange(n_assign, dtype=jnp.int32)), num_keys=1, is_stable=True)
    counts = jnp.sum((flat_e[:, None] == experts[None, :]).astype(jnp.int32), axis=0)
    cnt_end = jnp.cumsum(counts)
    cnt_start = cnt_end - counts
    padded = (counts + MOE_MB - 1) // MOE_MB * MOE_MB
    pad_end = jnp.cumsum(padded)
    pad_start = pad_end - padded
    onehot = (sorted_e[:, None] == experts[None, :]).astype(jnp.int32)
    slot_sorted = jnp.arange(n_assign, dtype=jnp.int32) + jnp.sum(onehot * (pad_start - cnt_start)[None, :], axis=1)
    _, slot_of = lax.sort((order, slot_sorted), num_keys=1)
    blk_first = jnp.arange(MOE_NB, dtype=jnp.int32) * MOE_MB
    block_e = jnp.minimum(jnp.sum((pad_end[None, :] <= blk_first[:, None]).astype(jnp.int32), axis=1),
                          N_EXPERTS - 1)
    blk_onehot = (block_e[:, None] == experts[None, :]).astype(jnp.int32)
    pick = lambda v: jnp.sum(blk_onehot * v[None, :], axis=1)
    src = jnp.clip(pick(cnt_start) + blk_first - pick(pad_start), 0, n_assign)
    sorted_tok = jnp.pad(order // TOP_K, (0, MOE_MB))
    n_used = (pad_end[-1] // MOE_MB).astype(jnp.int32).reshape(1)
    return sorted_tok, slot_of.reshape(T_ALL, TOP_K), block_e.astype(jnp.int32), src.astype(jnp.int32), n_used


def _expert_kernel(be_ref, nu_ref, src_ref, stok_ref, x_hbm, wg_ref, wu_ref, wd_ref, o_ref,
                   xbuf, wg_bf, wu_bf, wd_bf, sems):
    i = pl.program_id(0)
    n_used = nu_ref[0]
    last = MOE_NB - 1

    def row_copy(blk, slot, j):
        tok = stok_ref[src_ref[blk] + j]
        return pltpu.make_async_copy(x_hbm.at[pl.ds(tok, 1), :], xbuf.at[slot, pl.ds(j, 1), :], sems.at[slot])

    def wait_block(slot):
        pltpu.make_async_copy(x_hbm.at[pl.ds(0, MOE_MB), :], xbuf.at[slot], sems.at[slot]).wait()

    @pl.when(i == 0)
    def _():
        def body(j, carry):
            row_copy(0, 0, j).start()
            return carry
        lax.fori_loop(0, MOE_MB, body, 0, unroll=8)

    def compute(prefetch):
        @pl.when((i == 0) | (be_ref[i] != be_ref[jnp.maximum(i - 1, 0)]))
        def _():
            wg_bf[...] = wg_ref[...].astype(BF16)
            wu_bf[...] = wu_ref[...].astype(BF16)
            wd_bf[...] = wd_ref[...].astype(BF16)

        slot = i % 2
        wait_block(slot)
        if prefetch:
            for j in range(MOE_MB):
                row_copy(i + 1, 1 - slot, j).start()
        x = xbuf[slot].astype(BF16)
        g = jnp.dot(x, wg_bf[...], preferred_element_type=F32)
        u = jnp.dot(x, wu_bf[...], preferred_element_type=F32)
        a = (_silu(g) * u).astype(BF16)
        o_ref[...] = jnp.dot(a, wd_bf[...], preferred_element_type=F32)

    @pl.when((i < n_used) & (i < last))
    def _():
        compute(True)

    @pl.when((i < n_used) & (i == last))
    def _():
        compute(False)

    @pl.when(i >= n_used)
    def _():
        o_ref[...] = jnp.zeros_like(o_ref)

        @pl.when(i == n_used)
        def _():
            wait_block(i % 2)


def _experts(x, sorted_tok, block_e, src, n_used, wg, wu, wd, layer):
    w_idx = lambda i, be, nu, src, stok: (layer, be[i], 0, 0)
    w_up = pl.BlockSpec((None, None, D_MODEL, F_EXPERT), w_idx)
    return pl.pallas_call(
        _expert_kernel,
        grid_spec=pltpu.PrefetchScalarGridSpec(
            num_scalar_prefetch=4,
            grid=(MOE_NB,),
            in_specs=[pl.BlockSpec(memory_space=pl.ANY), w_up, w_up,
                      pl.BlockSpec((None, None, F_EXPERT, D_MODEL), w_idx)],
            out_specs=pl.BlockSpec((MOE_MB, D_MODEL), lambda i, be, nu, src, stok: (i, 0)),
            scratch_shapes=[pltpu.VMEM((2, MOE_MB, D_MODEL), F32),
                            pltpu.VMEM((D_MODEL, F_EXPERT), BF16), pltpu.VMEM((D_MODEL, F_EXPERT), BF16),
                            pltpu.VMEM((F_EXPERT, D_MODEL), BF16), pltpu.SemaphoreType.DMA((2,))]),
        out_shape=jax.ShapeDtypeStruct((MOE_SLOTS, D_MODEL), F32),
        compiler_params=_cparams(1), name="moe_experts",
    )(block_e, n_used, src, sorted_tok, x, wg, wu, wd)


def _shared_up_kernel(x_ref, wg_ref, wu_ref, o_ref, wg_bf, wu_bf):
    @pl.when(pl.program_id(0) == 0)
    def _():
        wg_bf[...] = wg_ref[...].astype(BF16)
        wu_bf[...] = wu_ref[...].astype(BF16)

    x = x_ref[...].astype(BF16)
    g = jnp.dot(x, wg_bf[...], preferred_element_type=F32)
    u = jnp.dot(x, wu_bf[...], preferred_element_type=F32)
    o_ref[...] = (_silu(g) * u).astype(o_ref.dtype)


def _shared_up(h, wg, wu, layer):
    w = pl.BlockSpec((None, D_MODEL, F_SHARED), lambda i: (layer, 0, 0))
    return pl.pallas_call(
        _shared_up_kernel,
        grid=(T_ALL // TM,),
        in_specs=[pl.BlockSpec((TM, D_MODEL), lambda i: (i, 0)), w, w],
        out_specs=pl.BlockSpec((TM, F_SHARED), lambda i: (i, 0)),
        out_shape=jax.ShapeDtypeStruct((T_ALL, F_SHARED), BF16),
        scratch_shapes=[pltpu.VMEM((D_MODEL, F_SHARED), BF16), pltpu.VMEM((D_MODEL, F_SHARED), BF16)],
        compiler_params=_cparams(1), name="shared_up",
    )(h, wg, wu)


def _combine_kernel(slot_ref, nxt_ref, y_hbm, rw_ref, act_ref, wd_ref, x_ref, mod_ref, o_ref, buf, wd_bf, sems, *,
                    gate_row):
    i = pl.program_id(0)
    n_rows = TOP_K * CMB_TM

    def gather(idx_ref, slot):
        def body(j, carry):
            pltpu.make_async_copy(y_hbm.at[pl.ds(idx_ref[0, 0, j], 1), :], buf.at[slot, pl.ds(j, 1), :],
                                  sems.at[slot]).start()
            return carry
        lax.fori_loop(0, n_rows, body, 0, unroll=8)

    @pl.when(i == 0)
    def _():
        wd_bf[...] = wd_ref[...].astype(BF16)
        gather(slot_ref, 0)

    @pl.when(i + 1 < pl.num_programs(0))
    def _():
        gather(nxt_ref, (i + 1) % 2)

    slot = i % 2
    y = jnp.dot(act_ref[...], wd_bf[...], preferred_element_type=F32)
    pltpu.make_async_copy(y_hbm.at[pl.ds(0, n_rows), :], buf.at[slot], sems.at[slot]).wait()
    for k in range(TOP_K):
        y = y + rw_ref[:, k:k + 1] * buf[slot, k * CMB_TM:(k + 1) * CMB_TM, :]
    o_ref[...] = x_ref[...] + mod_ref[gate_row:gate_row + 1, :] * y


def _combine(y_sorted, slot_of, route_w, act, sh_wd, x, mod, layer):
    nt = T_ALL // CMB_TM
    slots = slot_of.reshape(nt, CMB_TM, TOP_K).transpose(0, 2, 1).reshape(nt, 1, TOP_K * CMB_TM)
    slot_spec = lambda nxt: pl.BlockSpec((1, 1, TOP_K * CMB_TM), lambda i: (jnp.minimum(i + nxt, nt - 1), 0, 0),
                                         memory_space=pltpu.SMEM)
    return pl.pallas_call(
        functools.partial(_combine_kernel, gate_row=5),
        grid=(nt,),
        in_specs=[slot_spec(0), slot_spec(1),
                  pl.BlockSpec(memory_space=pl.ANY),
                  pl.BlockSpec((CMB_TM, TOP_K), lambda i: (i, 0)),
                  pl.BlockSpec((CMB_TM, F_SHARED), lambda i: (i, 0)),
                  pl.BlockSpec((None, F_SHARED, D_MODEL), lambda i: (layer, 0, 0)),
                  pl.BlockSpec((CMB_TM, D_MODEL), lambda i: (i, 0)),
                  pl.BlockSpec((None, 6, D_MODEL), lambda i: (_group_of_tile(i, CMB_TM), 0, 0))],
        out_specs=pl.BlockSpec((CMB_TM, D_MODEL), lambda i: (i, 0)),
        out_shape=jax.ShapeDtypeStruct((T_ALL, D_MODEL), F32),
        scratch_shapes=[pltpu.VMEM((2, TOP_K * CMB_TM, D_MODEL), F32), pltpu.VMEM((F_SHARED, D_MODEL), BF16),
                        pltpu.SemaphoreType.DMA((2,))],
        compiler_params=_cparams(1), name="moe_combine",
    )(slots, slots, y_sorted, route_w, act, sh_wd, x, mod)


def kernel(x_prompt, x_sample, cache_na_k, cache_na_v, state_gla, state_s5, c, c_ctx, ada_w, ada_b, norm1_g, norm2_g, w_in, na_rpb, gla_wa, gla_ba, gla_norm_g, s5_a_re, s5_a_im, s5_log_dt, s5_b_re, s5_b_im, s5_c_re, s5_c_im, s5_d, s5_w_glu, w_br_na, w_br_gla, w_br_s5, w_merge, w_out, router_w, router_bias, exp_wg, exp_wu, exp_wd, sh_wg, sh_wu, sh_wd, final_norm_g):
    x = jnp.concatenate([x_prompt.reshape(T_CTX, D_MODEL), x_sample.reshape(T_LAT, D_MODEL)], axis=0)
    cvec = jnp.concatenate([c_ctx[None, :], c, jnp.zeros((MOD_ROWS - N_MOD, D_MODEL), F32)], axis=0)
    mods = _ada(cvec, ada_w, ada_b).reshape(DEPTH, MOD_ROWS, 6, D_MODEL)
    cos, sin = _rope_tables()
    zero_gla = jnp.zeros((BATCH, 2, GLA_HEADS, GLA_DV, GLA_DK), F32)
    zero_s5 = jnp.zeros((S5_GROUPS, 2, BATCH, 2 * S5_N), F32)
    new_k, new_v, new_gla, new_s5 = [], [], [], []
    for l in range(DEPTH):
        mod = mods[l]
        h = _norm(x, norm1_g[l], mod, (0, 1), BF16)
        u = _mm(h, w_in, l, 1024, n=IN_MAIN)
        lr = _mm(h, w_in[l, :, IN_MAIN:IN_MAIN + 2 * GLA_LR], None, 2 * GLA_LR)
        su = _mm(h, w_in[l, :, IN_MAIN + 2 * GLA_LR:], None, S5_W)
        gates = _mm(h, w_merge, l, 1024, act="sigmoid")

        o_na = jnp.concatenate([_ctx_attn(u),
                                _na_attn(u, cache_na_k, cache_na_v, _na_bias(na_rpb[l]), l)], axis=0)

        s0_lat = state_gla[:, l].transpose(0, 1, 2, 4, 3)
        og_ctx, sfin = _gla(u, lr, gla_wa, gla_ba, gla_norm_g, zero_gla, cos, sin, l, latent=False)
        og_lat, _ = _gla(u, lr, gla_wa, gla_ba, gla_norm_g, s0_lat, cos, sin, l, latent=True)
        o_gla = jnp.concatenate([og_ctx, og_lat], axis=0)

        ops, step = _s5_operators(s5_a_re[l], s5_a_im[l], s5_log_dt[l], s5_b_re[l], s5_b_im[l],
                                  s5_c_re[l], s5_c_im[l])
        st = state_s5[:, l].astype(F32)
        h0_lat = jnp.concatenate([st[..., 0], st[..., 1]], axis=-1).transpose(2, 1, 0, 3)
        y_ctx, hfin = _s5(su[:T_CTX], ops, step, zero_s5, nb=BATCH, seq=SEQ)
        y_lat, _ = _s5(su[T_CTX:], ops, step, h0_lat, nb=DEC_BATCH, seq=DEC_SEQ)
        zz = _glu(jnp.concatenate([y_ctx, y_lat], axis=0), su, s5_d, s5_w_glu, l)

        merged = _merge(o_na, o_gla, zz, gates, w_br_na, w_br_gla, w_br_s5, l)
        x = _out_proj(merged, w_out, x, mod, l)

        h2 = _norm(x, norm2_g[l], mod, (3, 4), F32)
        idx_t, w_t = _router(h2, router_w[l].T, router_bias[l])
        sorted_tok, slot_of, block_e, src, n_used = _dispatch(idx_t.T)
        y_sorted = _experts(h2, sorted_tok, block_e, src, n_used, exp_wg, exp_wu, exp_wd, l)
        act = _shared_up(h2, sh_wg, sh_wu, l)
        x = _combine(y_sorted, slot_of, w_t.T, act, sh_wd, x, mod, l)

        new_k.append(u[:T_CTX, NA_W:2 * NA_W].reshape(BATCH, SEQ, NA_HEADS, NA_DH))
        new_v.append(u[:T_CTX, 2 * NA_W:3 * NA_W].reshape(BATCH, SEQ, NA_HEADS, NA_DH))
        new_gla.append(sfin.transpose(0, 1, 2, 4, 3))
        hf = hfin.transpose(2, 1, 0, 3)
        new_s5.append(jnp.stack([hf[..., :S5_N], hf[..., S5_N:]], axis=-1))

    y = _norm(x, final_norm_g, None, None, F32)
    return (y[:T_CTX].reshape(BATCH, SEQ, D_MODEL), y[T_CTX:].reshape(DEC_BATCH, DEC_SEQ, D_MODEL),
            jnp.stack(new_k, axis=1), jnp.stack(new_v, axis=1),
            jnp.stack(new_gla, axis=1), jnp.stack(new_s5, axis=1))
```

```python
import functools
import math

import jax
import jax.numpy as jnp
from jax import lax
from jax.experimental import pallas as pl
from jax.experimental.pallas import tpu as pltpu

F32 = jnp.float32
BF16 = jnp.bfloat16

D_MODEL = 2048
BATCH = 16
SEQ = 256
DEPTH = 2
DEC_BATCH = 2
DEC_SEQ = 1024
PAST_LEN = 512
GRID_W = 64
NA_HEADS = 8
NA_DH = 128
NA_W = NA_HEADS * NA_DH
NA_ROWS = 8
NA_COLS = 16
GLA_HEADS = 4
GLA_DK = 128
GLA_DV = 256
GLA_QK_W = GLA_HEADS * GLA_DK
GLA_V_W = GLA_HEADS * GLA_DV
GLA_LR = 16
GLA_TAU = 16.0
GLA_CHUNK = 64
GLA_SUB = 16
ROPE_BASE = 10000.0
S5_W = 1024
S5_GS = 16
S5_GROUPS = S5_W // S5_GS
S5_N = 64
S5_T = 16
IN_W = 3 * NA_W + 2 * GLA_QK_W + 2 * GLA_V_W + 2 * GLA_LR + S5_W
IN_MAIN = 3 * NA_W + 2 * GLA_QK_W + 2 * GLA_V_W
N_EXPERTS = 64
TOP_K = 8
N_GROUPS = 8
TOPK_GROUPS = 4
F_EXPERT = 512
F_SHARED = 512
ROUTED_SCALE = 2.5
EPS = 1e-6

T_CTX = BATCH * SEQ
T_LAT = DEC_BATCH * DEC_SEQ
T_ALL = T_CTX + T_LAT
N_MOD = 1 + DEC_BATCH
MOD_ROWS = 8

TM = 512
MOE_MB = 256
MOE_NB = T_ALL * TOP_K // MOE_MB + N_EXPERTS
MOE_SLOTS = MOE_NB * MOE_MB
CMB_TM = 128
NEG = -1e30

VMEM_LIMIT = 56 * 1024 * 1024

NT = (((1,), (1,)), ((), ()))
TN = (((0,), (0,)), ((), ()))


def _cparams(n_axes):
    return pltpu.CompilerParams(dimension_semantics=("arbitrary",) * n_axes,
                                vmem_limit_bytes=VMEM_LIMIT)


def _group_of_tile(i, tm):
    row = i * tm
    return jnp.where(row < T_CTX, 0, 1 + (row - T_CTX) // DEC_SEQ)


def _silu(x):
    return x * jax.nn.sigmoid(x)


def _gelu_tanh(x):
    return 0.5 * x * (1.0 + jnp.tanh(math.sqrt(2.0 / math.pi) * (x + 0.044715 * (x * x * x))))


def _ada_kernel(c_ref, w_ref, b_ref, o_ref):
    s = _silu(c_ref[...]).astype(BF16)
    o_ref[...] = jnp.dot(s, w_ref[...].astype(BF16), preferred_element_type=F32) + b_ref[...]


def _ada(cvec, ada_w, ada_b):
    tn = 1024
    return pl.pallas_call(
        _ada_kernel,
        grid=(DEPTH, 6 * D_MODEL // tn),
        in_specs=[pl.BlockSpec((MOD_ROWS, D_MODEL), lambda l, j: (0, 0)),
                  pl.BlockSpec((None, D_MODEL, tn), lambda l, j: (l, 0, j)),
                  pl.BlockSpec((None, 1, tn), lambda l, j: (l, 0, j))],
        out_specs=pl.BlockSpec((None, MOD_ROWS, tn), lambda l, j: (l, 0, j)),
        out_shape=jax.ShapeDtypeStruct((DEPTH, MOD_ROWS, 6 * D_MODEL), F32),
        compiler_params=_cparams(2), name="ada",
    )(cvec, ada_w, ada_b.reshape(DEPTH, 1, 6 * D_MODEL))


def _norm_kernel(x_ref, g_ref, *rest, rows):
    o_ref = rest[-1]
    x = x_ref[...]
    y = x * lax.rsqrt(jnp.mean(x * x, axis=-1, keepdims=True) + EPS) * g_ref[...]
    if rows is not None:
        mod_ref = rest[0]
        y = y * (1.0 + mod_ref[rows[1]:rows[1] + 1, :]) + mod_ref[rows[0]:rows[0] + 1, :]
    o_ref[...] = y.astype(o_ref.dtype)


def _norm(x, g, mod, rows, out_dtype):
    tm = 256
    in_specs = [pl.BlockSpec((tm, D_MODEL), lambda i: (i, 0)),
                pl.BlockSpec((1, D_MODEL), lambda i: (0, 0))]
    args = [x, g.reshape(1, D_MODEL)]
    if rows is not None:
        in_specs.append(pl.BlockSpec((None, 6, D_MODEL), lambda i: (_group_of_tile(i, tm), 0, 0)))
        args.append(mod)
    return pl.pallas_call(
        functools.partial(_norm_kernel, rows=rows),
        grid=(T_ALL // tm,),
        in_specs=in_specs,
        out_specs=pl.BlockSpec((tm, D_MODEL), lambda i: (i, 0)),
        out_shape=jax.ShapeDtypeStruct((T_ALL, D_MODEL), out_dtype),
        compiler_params=_cparams(1), name="norm",
    )(*args)


def _mm_kernel(x_ref, w_ref, o_ref, wbf_ref, *, act):
    @pl.when(pl.program_id(1) == 0)
    def _():
        wbf_ref[...] = w_ref[...].astype(BF16)

    acc = jnp.dot(x_ref[...].astype(BF16), wbf_ref[...], preferred_element_type=F32)
    if act == "sigmoid":
        acc = jax.nn.sigmoid(acc)
    o_ref[...] = acc.astype(o_ref.dtype)


def _mm(x, w, layer, tn, act=None, out_dtype=F32, n=None):
    k = x.shape[1]
    n = w.shape[-1] if n is None else n
    if layer is None:
        w_spec = pl.BlockSpec((k, tn), lambda j, i: (0, j))
    else:
        w_spec = pl.BlockSpec((None, k, tn), lambda j, i: (layer, 0, j))
    return pl.pallas_call(
        functools.partial(_mm_kernel, act=act),
        grid=(n // tn, T_ALL // TM),
        in_specs=[pl.BlockSpec((TM, k), lambda j, i: (i, 0)), w_spec],
        out_specs=pl.BlockSpec((TM, tn), lambda j, i: (i, j)),
        out_shape=jax.ShapeDtypeStruct((T_ALL, n), out_dtype),
        scratch_shapes=[pltpu.VMEM((k, tn), BF16)],
        compiler_params=_cparams(2), name="mm",
    )(x, w)


def _ctx_attn_kernel(q_ref, k_ref, v_ref, o_ref):
    scale = NA_DH ** -0.5
    for h in range(NA_HEADS):
        sl = slice(h * NA_DH, (h + 1) * NA_DH)
        q = q_ref[:, sl].astype(BF16)
        k = k_ref[:, sl].astype(BF16)
        v = v_ref[:, sl].astype(BF16)
        s = lax.dot_general(q, k, NT, preferred_element_type=F32) * scale
        p = jnp.exp(s - jnp.max(s, axis=-1, keepdims=True))
        o = jnp.dot(p.astype(BF16), v, preferred_element_type=F32) / jnp.sum(p, axis=-1, keepdims=True)
        o_ref[:, sl] = o.astype(o_ref.dtype)


def _ctx_attn(u):
    spec = lambda cb: pl.BlockSpec((SEQ, NA_W), lambda b: (b, cb))
    return pl.pallas_call(
        _ctx_attn_kernel,
        grid=(BATCH,),
        in_specs=[spec(0), spec(1), spec(2)],
        out_specs=pl.BlockSpec((SEQ, NA_W), lambda b: (b, 0)),
        out_shape=jax.ShapeDtypeStruct((T_CTX, NA_W), BF16),
        compiler_params=_cparams(1), name="ctx_attn",
    )(u, u, u)


NA_GRID_ROWS = DEC_SEQ // GRID_W
NA_KR = min(NA_ROWS, NA_GRID_ROWS)
NA_LOC = NA_KR * GRID_W


def _na_bias(rpb):
    n_off = 2 * NA_ROWS - 1
    pad = GRID_W - NA_COLS
    period = 2 * GRID_W
    p = jnp.pad(rpb, ((0, 0), (0, 0), (pad, period - pad - (2 * NA_COLS - 1))))
    hank = jnp.tile(p, (1, 1, GRID_W + 1))[..., :GRID_W * (period + 1)]
    hank = hank.reshape(NA_HEADS, n_off, GRID_W, period + 1)[..., :GRID_W]
    band = hank[:, :, ::-1, :]
    qc = jnp.arange(GRID_W)[:, None]
    kc = jnp.arange(GRID_W)[None, :]
    win = jnp.clip(qc - NA_COLS // 2, 0, GRID_W - NA_COLS)
    valid = (kc >= win) & (kc < win + NA_COLS)
    band = jnp.where(valid[None, None], band, NEG)
    return band.transpose(0, 2, 1, 3).reshape(NA_HEADS, GRID_W, n_off * GRID_W)


def _na_kernel(q_ref, k_ref, v_ref, ck_ref, cv_ref, bias_ref, o_ref):
    scale = NA_DH ** -0.5
    kb = k_ref[...].astype(BF16)
    vb = v_ref[...].astype(BF16)
    ck = ck_ref[...].astype(BF16)
    cv = cv_ref[...].astype(BF16)
    for r in range(NA_GRID_ROWS):
        first = min(max(r - NA_KR // 2, 0), NA_GRID_ROWS - NA_KR)
        off = first - r + NA_ROWS - 1
        rows = slice(r * GRID_W, (r + 1) * GRID_W)
        keys = slice(first * GRID_W, first * GRID_W + NA_LOC)
        q = q_ref[rows, :].astype(BF16)
        s1 = (lax.dot_general(q, kb[keys, :], NT, preferred_element_type=F32) * scale
              + bias_ref[:, off * GRID_W:off * GRID_W + NA_LOC])
        s2 = lax.dot_general(q, ck, NT, preferred_element_type=F32) * scale
        m = jnp.maximum(jnp.max(s1, axis=-1, keepdims=True), jnp.max(s2, axis=-1, keepdims=True))
        p1 = jnp.exp(s1 - m)
        p2 = jnp.exp(s2 - m)
        den = jnp.sum(p1, axis=-1, keepdims=True) + jnp.sum(p2, axis=-1, keepdims=True)
        o = (jnp.dot(p1.astype(BF16), vb[keys, :], preferred_element_type=F32)
             + jnp.dot(p2.astype(BF16), cv, preferred_element_type=F32))
        o_ref[rows, :] = (o / den).astype(o_ref.dtype)


def _na_attn(u, cache_k, cache_v, bias, layer):
    lat_sb = T_CTX // DEC_SEQ
    ck = cache_k.reshape(DEC_BATCH, DEPTH, PAST_LEN, NA_W)
    cv = cache_v.reshape(DEC_BATCH, DEPTH, PAST_LEN, NA_W)
    qkv_spec = lambda cb: pl.BlockSpec((DEC_SEQ, NA_DH), lambda b, h: (lat_sb + b, cb * NA_HEADS + h))
    c_spec = pl.BlockSpec((None, None, PAST_LEN, NA_DH), lambda b, h: (b, layer, 0, h))
    return pl.pallas_call(
        _na_kernel,
        grid=(DEC_BATCH, NA_HEADS),
        in_specs=[qkv_spec(0), qkv_spec(1), qkv_spec(2), c_spec, c_spec,
                  pl.BlockSpec((None, GRID_W, (2 * NA_ROWS - 1) * GRID_W), lambda b, h: (h, 0, 0))],
        out_specs=pl.BlockSpec((DEC_SEQ, NA_DH), lambda b, h: (b, h)),
        out_shape=jax.ShapeDtypeStruct((T_LAT, NA_W), BF16),
        compiler_params=_cparams(2), name="na_attn",
    )(u, u, u, ck, cv, bias)


def _rope_tables():
    half = GLA_DK // 2
    nf = half // 2
    t = jnp.arange(DEC_SEQ)
    freqs = ROPE_BASE ** (-jnp.arange(nf, dtype=F32) / nf)
    ang_r = (t // GRID_W).astype(F32)[:, None] * freqs
    ang_c = (t % GRID_W).astype(F32)[:, None] * freqs
    cos = jnp.concatenate([jnp.cos(ang_r), jnp.cos(ang_r), jnp.cos(ang_c), jnp.cos(ang_c)], axis=-1)
    sin = jnp.concatenate([-jnp.sin(ang_r), jnp.sin(ang_r), -jnp.sin(ang_c), jnp.sin(ang_c)], axis=-1)
    return cos, sin


def _split_bf16(x):
    hi = x.astype(BF16)
    return hi, (x - hi.astype(F32)).astype(BF16)


def _gla_kernel(q_ref, k_ref, v_ref, gg_ref, lr_ref, wa_ref, ba_ref, ng_ref, cos_ref, sin_ref, s0_ref,
                o_ref, sfin_ref, qs, ks, las, o_acc, st, *, seq, rope):
    nc = seq // GLA_CHUNK
    nq = GLA_DK // 4

    def rot(x):
        lane = lax.broadcasted_iota(jnp.int32, x.shape, 1)
        partner = jnp.where((lane % (2 * nq)) < nq,
                            pltpu.roll(x, GLA_DK - nq, axis=1), pltpu.roll(x, nq, axis=1))
        return x * cos_ref[...] + partner * sin_ref[...]

    q = q_ref[...] * (GLA_DK ** -0.5)
    k = k_ref[...]
    if rope:
        q = rot(q)
        k = rot(k)
    qs[...] = q
    ks[...] = k
    lr = lr_ref[...]
    for d in range(2):
        z = jnp.dot(lr[:, d * GLA_LR:(d + 1) * GLA_LR].astype(BF16), wa_ref[d].astype(BF16),
                    preferred_element_type=F32) + ba_ref[d:d + 1, :]
        las[d] = -(jnp.maximum(-z, 0.0) + jnp.log1p(jnp.exp(-jnp.abs(z)))) / GLA_TAU
        st[d] = s0_ref[d]

    row = lax.broadcasted_iota(jnp.int32, (GLA_CHUNK, GLA_CHUNK), 0)
    col = lax.broadcasted_iota(jnp.int32, (GLA_CHUNK, GLA_CHUNK), 1)
    key_row = lax.broadcasted_iota(jnp.int32, (GLA_CHUNK, 1), 0)

    def chunk(c, d):
        rev = d == 1
        rows = pl.ds(pl.multiple_of(c * GLA_CHUNK, GLA_CHUNK), GLA_CHUNK)
        qc = qs[rows, :]
        kc = ks[rows, :]
        vc = v_ref[rows, :].astype(BF16)
        la = las[d, rows, :]
        causal = (col >= row) if rev else (col <= row)
        tri = jnp.where(causal, 1.0, 0.0).astype(BF16)
        la_hi, la_lo = _split_bf16(la)
        b = (jnp.dot(tri, la_hi, preferred_element_type=F32)
             + jnp.dot(tri, la_lo, preferred_element_type=F32))
        bex = b - la
        b_last = b[0:1, :] if rev else b[GLA_CHUNK - 1:GLA_CHUNK, :]
        blocks = []
        for i in range(GLA_CHUNK // GLA_SUB):
            lo, hi = i * GLA_SUB, (i + 1) * GLA_SUB
            ref = bex[hi - 1:hi, :] if rev else bex[lo:lo + 1, :]
            qt = (qc[lo:hi, :] * jnp.exp(b[lo:hi, :] - ref)).astype(BF16)
            allowed = (key_row >= lo) if rev else (key_row < hi)
            kt = (kc * jnp.exp(jnp.where(allowed, ref - b, -jnp.inf))).astype(BF16)
            blocks.append(lax.dot_general(qt, kt, NT, preferred_element_type=F32))
        att = jnp.where(causal, jnp.concatenate(blocks, axis=0), 0.0)
        s_t = st[d]
        o = (jnp.dot(att.astype(BF16), vc, preferred_element_type=F32)
             + lax.dot_general((qc * jnp.exp(b)).astype(BF16), s_t.astype(BF16), NT,
                               preferred_element_type=F32))
        khat = (kc * jnp.exp(b_last - b)).astype(BF16)
        st[d] = s_t * jnp.exp(b_last) + lax.dot_general(vc, khat, TN, preferred_element_type=F32)
        o_acc[d, rows, :] = o

    def both(c, carry):
        chunk(c, 0)
        chunk(nc - 1 - c, 1)
        return carry

    lax.fori_loop(0, nc, both, 0)
    sfin_ref[...] = st[...]
    o = o_acc[0] + o_acc[1]
    o = o * lax.rsqrt(jnp.mean(o * o, axis=-1, keepdims=True) + EPS) * ng_ref[...]
    o_ref[...] = (o * _silu(gg_ref[...])).astype(o_ref.dtype)


def _gla(u, lr, wa, ba, norm_g, s0_t, cos, sin, layer, *, latent):
    seq, nb, first = (DEC_SEQ, DEC_BATCH, T_CTX // DEC_SEQ) if latent else (SEQ, BATCH, 0)
    qk_cb = 3 * NA_W // GLA_DK
    v_cb = (3 * NA_W + 2 * GLA_QK_W) // GLA_DV
    row = lambda w, cb: pl.BlockSpec((seq, w), lambda b, h: (first + b, cb + h))
    return pl.pallas_call(
        functools.partial(_gla_kernel, seq=seq, rope=latent),
        grid=(nb, GLA_HEADS),
        in_specs=[row(GLA_DK, qk_cb), row(GLA_DK, qk_cb + GLA_HEADS), row(GLA_DV, v_cb),
                  row(GLA_DV, v_cb + GLA_HEADS),
                  pl.BlockSpec((seq, 2 * GLA_LR), lambda b, h: (first + b, 0)),
                  pl.BlockSpec((None, 2, GLA_LR, GLA_DK), lambda b, h: (layer, 0, 0, h)),
                  pl.BlockSpec((None, 2, GLA_DK), lambda b, h: (layer, 0, h)),
                  pl.BlockSpec((None, 1, GLA_DV), lambda b, h: (layer, 0, h)),
                  pl.BlockSpec((seq, GLA_DK), lambda b, h: (0, 0)),
                  pl.BlockSpec((seq, GLA_DK), lambda b, h: (0, 0)),
                  pl.BlockSpec((None, 2, None, GLA_DV, GLA_DK), lambda b, h: (b, 0, h, 0, 0))],
        out_specs=[pl.BlockSpec((seq, GLA_DV), lambda b, h: (b, h)),
                   pl.BlockSpec((None, 2, None, GLA_DV, GLA_DK), lambda b, h: (b, 0, h, 0, 0))],
        out_shape=[jax.ShapeDtypeStruct((nb * seq, GLA_V_W), BF16),
                   jax.ShapeDtypeStruct((nb, 2, GLA_HEADS, GLA_DV, GLA_DK), F32)],
        scratch_shapes=[pltpu.VMEM((seq, GLA_DK), F32), pltpu.VMEM((seq, GLA_DK), F32),
                        pltpu.VMEM((2, seq, GLA_DK), F32), pltpu.VMEM((2, seq, GLA_DV), F32),
                        pltpu.VMEM((2, GLA_DV, GLA_DK), F32)],
        compiler_params=_cparams(2), name="gla",
    )(u, u, u, u, lr, wa, ba, norm_g.reshape(DEPTH, 1, GLA_V_W), cos[:seq], sin[:seq], s0_t)


def _s5_operators(a_re, a_im, log_dt, b_re, b_im, c_re, c_im):
    hp = lax.Precision.HIGHEST
    lam = lax.complex(a_re.astype(F32), a_im.astype(F32))
    lam_dt = lam * jnp.exp(log_dt.astype(F32))[..., None]
    a_bar = jnp.exp(lam_dt)
    b_bar = ((a_bar - 1.0) / lam)[..., None] * lax.complex(b_re.astype(F32), b_im.astype(F32))
    cc = lax.complex(c_re.astype(F32), c_im.astype(F32))
    taus = jnp.arange(S5_T + 1, dtype=F32)
    pw = jnp.exp(lam_dt[:, :, None, :] * taus[None, None, :, None])
    kern = jnp.real(jnp.einsum("dgjn,dgtn,dgni->dgitj", cc, pw[:, :, :S5_T], b_bar, precision=hp))
    lanes = S5_T * S5_GS
    zpad = lambda a, before, after: jnp.pad(a, ((0, 0), (0, 0), (before, after), (0, 0)))

    def toeplitz(rows):
        return jnp.stack(rows, axis=1).reshape(S5_GROUPS, lanes, lanes)

    toep_f = toeplitz([zpad(kern[0][:, :, :S5_T - s], s, 0) for s in range(S5_T)])
    toep_b = toeplitz([zpad(kern[1][:, :, s::-1], 0, S5_T - 1 - s) for s in range(S5_T)])

    def state_in(pwd, bd):
        m = pwd[:, :, None, :] * bd.transpose(0, 2, 1)[:, None, :, :]
        m = m.reshape(S5_GROUPS, lanes, S5_N)
        return [jnp.real(m), jnp.imag(m)]

    p = jnp.stack(state_in(pw[0, :, S5_T - 1::-1], b_bar[0])
                  + state_in(pw[1, :, :S5_T], b_bar[1]), axis=1)

    def state_out(pwd, cd):
        m = pwd.transpose(0, 2, 1)[:, :, :, None] * cd.transpose(0, 2, 1)[:, :, None, :]
        m = m.reshape(S5_GROUPS, S5_N, lanes)
        return [jnp.real(m), -jnp.imag(m)]

    q = jnp.stack(state_out(pw[0, :, 1:S5_T + 1], cc[0])
                  + state_out(pw[1, :, S5_T:0:-1], cc[1]), axis=1)
    ops = [m.astype(BF16) for m in (toep_f, toep_b, p, q)]
    a_t = pw[:, :, S5_T]
    step = jnp.stack([jnp.real(a_t[0]), jnp.imag(a_t[0]), jnp.real(a_t[1]), jnp.imag(a_t[1])], axis=1)
    return ops, step


def _s5_kernel(u_ref, tf_ref, tb_ref, p_ref, q_ref, step_ref, h0_ref, y_ref, hfin_ref, e_scr, hin_scr, *, nc, nb):
    u = u_ref[...]
    for m in range(4):
        e_scr[m] = jnp.dot(u, p_ref[m], preferred_element_type=F32)
    ar = [step_ref[0:1, :], step_ref[2:3, :]]
    ai = [step_ref[1:2, :], step_ref[3:4, :]]
    hr = [h0_ref[0], h0_ref[2]]
    hi = [h0_ref[1], h0_ref[3]]
    for c in range(nc):
        for d in range(2):
            rows = slice(c * nb, (c + 1) * nb) if d == 0 else slice((nc - 1 - c) * nb, (nc - c) * nb)
            hin_scr[2 * d, rows, :] = hr[d]
            hin_scr[2 * d + 1, rows, :] = hi[d]
            hr[d], hi[d] = (hr[d] * ar[d] - hi[d] * ai[d] + e_scr[2 * d, rows, :],
                            hi[d] * ar[d] + hr[d] * ai[d] + e_scr[2 * d + 1, rows, :])
    for d in range(2):
        hfin_ref[2 * d] = hr[d]
        hfin_ref[2 * d + 1] = hi[d]
    y = jnp.dot(u, tf_ref[...], preferred_element_type=F32) + jnp.dot(u, tb_ref[...], preferred_element_type=F32)
    for m in range(4):
        y += jnp.dot(hin_scr[m].astype(BF16), q_ref[m], preferred_element_type=F32)
    y_ref[...] = y


def _s5(su, ops, step, h0, *, nb, seq):
    nc = seq // S5_T
    nbp = max(nb, 8)
    lanes = S5_T * S5_GS
    x = su.reshape(nb, nc, S5_T, S5_GROUPS, S5_GS).transpose(3, 1, 0, 2, 4)
    if nbp != nb:
        x = jnp.pad(x, ((0, 0), (0, 0), (0, nbp - nb), (0, 0), (0, 0)))
        h0 = jnp.pad(h0, ((0, 0), (0, 0), (0, nbp - nb), (0, 0)))
    x = x.reshape(S5_GROUPS, nc * nbp, lanes).astype(BF16)
    rows = nc * nbp
    g3 = lambda a, b: pl.BlockSpec((None, a, b), lambda g: (g, 0, 0))
    g4 = lambda a, b: pl.BlockSpec((None, 4, a, b), lambda g: (g, 0, 0, 0))
    y, hfin = pl.pallas_call(
        functools.partial(_s5_kernel, nc=nc, nb=nbp),
        grid=(S5_GROUPS,),
        in_specs=[g3(rows, lanes), g3(lanes, lanes), g3(lanes, lanes), g4(lanes, S5_N), g4(S5_N, lanes),
                  g3(4, S5_N), g4(nbp, S5_N)],
        out_specs=[g3(rows, lanes), g4(nbp, S5_N)],
        out_shape=[jax.ShapeDtypeStruct((S5_GROUPS, rows, lanes), F32),
                   jax.ShapeDtypeStruct((S5_GROUPS, 4, nbp, S5_N), F32)],
        scratch_shapes=[pltpu.VMEM((4, rows, S5_N), F32), pltpu.VMEM((4, rows, S5_N), F32)],
        compiler_params=_cparams(1), name="s5",
    )(x, *ops, step, h0)
    y = y.reshape(S5_GROUPS, nc, nbp, S5_T, S5_GS)[:, :, :nb].transpose(2, 1, 3, 0, 4)
    return y.reshape(nb * seq, S5_W), hfin


def _glu_kernel(y_ref, su_ref, d_ref, w_ref, o_ref, wbf_ref):
    @pl.when(pl.program_id(0) == 0)
    def _():
        wbf_ref[...] = w_ref[...].astype(BF16)

    z = _gelu_tanh(y_ref[...] + d_ref[...] * su_ref[...])
    gate = jax.nn.sigmoid(jnp.dot(z.astype(BF16), wbf_ref[...], preferred_element_type=F32))
    o_ref[...] = (z * gate).astype(o_ref.dtype)


def _glu(y, su, d, w_glu, layer):
    row = pl.BlockSpec((TM, S5_W), lambda i: (i, 0))
    return pl.pallas_call(
        _glu_kernel,
        grid=(T_ALL // TM,),
        in_specs=[row, row, pl.BlockSpec((None, 1, S5_W), lambda i: (layer, 0, 0)),
                  pl.BlockSpec((None, S5_W, S5_W), lambda i: (layer, 0, 0))],
        out_specs=row,
        out_shape=jax.ShapeDtypeStruct((T_ALL, S5_W), BF16),
        scratch_shapes=[pltpu.VMEM((S5_W, S5_W), BF16)],
        compiler_params=_cparams(1), name="glu",
    )(y, su, d.reshape(DEPTH, 1, S5_W), w_glu)


def _merge_kernel(a_ref, b_ref, c_ref, ga_ref, gb_ref, gc_ref, wa_ref, wb_ref, wc_ref, o_ref, wbf_ref):
    @pl.when(pl.program_id(1) == 0)
    def _():
        wbf_ref[0] = wa_ref[...].astype(BF16)
        wbf_ref[1] = wb_ref[...].astype(BF16)
        wbf_ref[2] = wc_ref[...].astype(BF16)

    acc = ga_ref[...] * jnp.dot(a_ref[...], wbf_ref[0], preferred_element_type=F32)
    acc += gb_ref[...] * jnp.dot(b_ref[...], wbf_ref[1], preferred_element_type=F32)
    acc += gc_ref[...] * jnp.dot(c_ref[...], wbf_ref[2], preferred_element_type=F32)
    o_ref[...] = acc.astype(o_ref.dtype)


def _merge(o_na, o_gla, zz, gates, w_na, w_gla, w_s5, layer):
    tn = 1024
    ncb = D_MODEL // tn
    act = pl.BlockSpec((TM, NA_W), lambda j, i: (i, 0))
    gate = lambda k: pl.BlockSpec((TM, tn), lambda j, i: (i, k * ncb + j))
    w = pl.BlockSpec((None, NA_W, tn), lambda j, i: (layer, 0, j))
    return pl.pallas_call(
        _merge_kernel,
        grid=(ncb, T_ALL // TM),
        in_specs=[act, act, act, gate(0), gate(1), gate(2), w, w, w],
        out_specs=pl.BlockSpec((TM, tn), lambda j, i: (i, j)),
        out_shape=jax.ShapeDtypeStruct((T_ALL, D_MODEL), BF16),
        scratch_shapes=[pltpu.VMEM((3, NA_W, tn), BF16)],
        compiler_params=_cparams(2), name="merge",
    )(o_na, o_gla, zz, gates, gates, gates, w_na, w_gla, w_s5)


def _out_proj_kernel(m_ref, w_ref, x_ref, mod_ref, o_ref, wbf_ref, *, gate_row):
    @pl.when(pl.program_id(1) == 0)
    def _():
        wbf_ref[...] = w_ref[...].astype(BF16)

    y = jnp.dot(m_ref[...], wbf_ref[...], preferred_element_type=F32)
    o_ref[...] = x_ref[...] + mod_ref[gate_row:gate_row + 1, :] * y


def _out_proj(merged, w_out, x, mod, layer):
    tn = 1024
    return pl.pallas_call(
        functools.partial(_out_proj_kernel, gate_row=2),
        grid=(D_MODEL // tn, T_ALL // TM),
        in_specs=[pl.BlockSpec((TM, D_MODEL), lambda j, i: (i, 0)),
                  pl.BlockSpec((None, D_MODEL, tn), lambda j, i: (layer, 0, j)),
                  pl.BlockSpec((TM, tn), lambda j, i: (i, j)),
                  pl.BlockSpec((None, 6, tn), lambda j, i: (_group_of_tile(i, TM), 0, j))],
        out_specs=pl.BlockSpec((TM, tn), lambda j, i: (i, j)),
        out_shape=jax.ShapeDtypeStruct((T_ALL, D_MODEL), F32),
        scratch_shapes=[pltpu.VMEM((D_MODEL, tn), BF16)],
        compiler_params=_cparams(2), name="out_proj",
    )(merged, w_out, x, mod)


def _first_max(v, iota, n):
    m = jnp.max(v, axis=0, keepdims=True)
    first = jnp.min(jnp.where(v == m, iota, float(n)), axis=0, keepdims=True)
    return m, first


def _router_kernel(x_ref, wt_ref, bias_ref, idx_ref, w_ref):
    per = N_EXPERTS // N_GROUPS
    x_hi, x_lo = _split_bf16(x_ref[...])
    w_hi, w_lo = _split_bf16(wt_ref[...])
    logits = (lax.dot_general(w_hi, x_hi, NT, preferred_element_type=F32)
              + lax.dot_general(w_hi, x_lo, NT, preferred_element_type=F32)
              + lax.dot_general(w_lo, x_hi, NT, preferred_element_type=F32))
    scores = jax.nn.sigmoid(logits)
    sel = scores + bias_ref[...]
    tm = sel.shape[1]
    iota_g = lax.broadcasted_iota(jnp.int32, (per, tm), 0).astype(F32)
    grp_rows = []
    for g in range(N_GROUPS):
        v = sel[g * per:(g + 1) * per, :]
        m1, first = _first_max(v, iota_g, per)
        m2 = jnp.max(jnp.where(iota_g == first, -jnp.inf, v), axis=0, keepdims=True)
        grp_rows.append(m1 + m2)
    grp = jnp.concatenate(grp_rows, axis=0)
    iota_n = lax.broadcasted_iota(jnp.int32, (N_GROUPS, tm), 0).astype(F32)
    chosen = jnp.zeros((N_GROUPS, tm), F32)
    for _ in range(TOPK_GROUPS):
        _, first = _first_max(grp, iota_n, N_GROUPS)
        hit = iota_n == first
        chosen = jnp.where(hit, 1.0, chosen)
        grp = jnp.where(hit, -jnp.inf, grp)
    mask = jnp.concatenate([jnp.broadcast_to(chosen[g:g + 1, :], (per, tm)) for g in range(N_GROUPS)], axis=0)
    sel = jnp.where(mask > 0.5, sel, -jnp.inf)
    iota_e = lax.broadcasted_iota(jnp.int32, (N_EXPERTS, tm), 0).astype(F32)
    ids, ws = [], []
    for _ in range(TOP_K):
        _, first = _first_max(sel, iota_e, N_EXPERTS)
        hit = iota_e == first
        ids.append(first)
        ws.append(jnp.sum(jnp.where(hit, scores, 0.0), axis=0, keepdims=True))
        sel = jnp.where(hit, -jnp.inf, sel)
    w = jnp.concatenate(ws, axis=0)
    idx_ref[...] = jnp.concatenate(ids, axis=0).astype(jnp.int32)
    w_ref[...] = w / jnp.sum(w, axis=0, keepdims=True) * ROUTED_SCALE


def _router(h, router_w_t, router_bias):
    tm = 256
    return pl.pallas_call(
        _router_kernel,
        grid=(T_ALL // tm,),
        in_specs=[pl.BlockSpec((tm, D_MODEL), lambda i: (i, 0)),
                  pl.BlockSpec((N_EXPERTS, D_MODEL), lambda i: (0, 0)),
                  pl.BlockSpec((N_EXPERTS, 1), lambda i: (0, 0))],
        out_specs=[pl.BlockSpec((TOP_K, tm), lambda i: (0, i)), pl.BlockSpec((TOP_K, tm), lambda i: (0, i))],
        out_shape=[jax.ShapeDtypeStruct((TOP_K, T_ALL), jnp.int32), jax.ShapeDtypeStruct((TOP_K, T_ALL), F32)],
        compiler_params=_cparams(1), name="router",
    )(h, router_w_t, router_bias.reshape(N_EXPERTS, 1))


def _dispatch(idx):
    n_assign = T_ALL * TOP_K
    flat_e = idx.reshape(-1)
    experts = jnp.arange(N_EXPERTS, dtype=jnp.int32)
    sorted_e, order = lax.sort((flat_e, jnp.arange(n_assign, dtype=jnp.int32)), num_keys=1, is_stable=True)
    counts = jnp.sum((flat_e[:, None] == experts[None, :]).astype(jnp.int32), axis=0)
    cnt_end = jnp.cumsum(counts)
    cnt_start = cnt_end - counts
    padded = (counts + MOE_MB - 1) // MOE_MB * MOE_MB
    pad_end = jnp.cumsum(padded)
    pad_start = pad_end - padded
    onehot = (sorted_e[:, None] == experts[None, :]).astype(jnp.int32)
    slot_sorted = jnp.arange(n_assign, dtype=jnp.int32) + jnp.sum(onehot * (pad_start - cnt_start)[None, :], axis=1)
    _, slot_of = lax.sort((order, slot_sorted), num_keys=1)
    blk_first = jnp.arange(MOE_NB, dtype=jnp.int32) * MOE_MB
    block_e = jnp.minimum(jnp.sum((pad_end[None, :] <= blk_first[:, None]).astype(jnp.int32), axis=1),
                          N_EXPERTS - 1)
    blk_onehot = (block_e[:, None] == experts[None, :]).astype(jnp.int32)
    pick = lambda v: jnp.sum(blk_onehot * v[None, :], axis=1)
    src = jnp.clip(pick(cnt_start) + blk_first - pick(pad_start), 0, n_assign)
    sorted_tok = jnp.pad(order // TOP_K, (0, MOE_MB))
    n_used = (pad_end[-1] // MOE_MB).astype(jnp.int32).reshape(1)
    return sorted_tok, slot_of.reshape(T_ALL, TOP_K), block_e.astype(jnp.int32), src.astype(jnp.int32), n_used


def _expert_kernel(be_ref, nu_ref, src_ref, stok_ref, x_hbm, wg_ref, wu_ref, wd_ref, o_ref,
                   xbuf, wg_bf, wu_bf, wd_bf, sems):
    i = pl.program_id(0)
    n_used = nu_ref[0]
    last = MOE_NB - 1

    def row_copy(blk, slot, j):
        tok = stok_ref[src_ref[blk] + j]
        return pltpu.make_async_copy(x_hbm.at[pl.ds(tok, 1), :], xbuf.at[slot, pl.ds(j, 1), :], sems.at[slot])

    def wait_block(slot):
        pltpu.make_async_copy(x_hbm.at[pl.ds(0, MOE_MB), :], xbuf.at[slot], sems.at[slot]).wait()

    @pl.when(i == 0)
    def _():
        def body(j, carry):
            row_copy(0, 0, j).start()
            return carry
        lax.fori_loop(0, MOE_MB, body, 0, unroll=8)

    def compute(prefetch):
        @pl.when((i == 0) | (be_ref[i] != be_ref[jnp.maximum(i - 1, 0)]))
        def _():
            wg_bf[...] = wg_ref[...].astype(BF16)
            wu_bf[...] = wu_ref[...].astype(BF16)
            wd_bf[...] = wd_ref[...].astype(BF16)

        slot = i % 2
        wait_block(slot)
        if prefetch:
            for j in range(MOE_MB):
                row_copy(i + 1, 1 - slot, j).start()
        x = xbuf[slot].astype(BF16)
        g = jnp.dot(x, wg_bf[...], preferred_element_type=F32)
        u = jnp.dot(x, wu_bf[...], preferred_element_type=F32)
        a = (_silu(g) * u).astype(BF16)
        o_ref[...] = jnp.dot(a, wd_bf[...], preferred_element_type=F32)

    @pl.when((i < n_used) & (i < last))
    def _():
        compute(True)

    @pl.when((i < n_used) & (i == last))
    def _():
        compute(False)

    @pl.when(i >= n_used)
    def _():
        o_ref[...] = jnp.zeros_like(o_ref)

        @pl.when(i == n_used)
        def _():
            wait_block(i % 2)


def _experts(x, sorted_tok, block_e, src, n_used, wg, wu, wd, layer):
    w_idx = lambda i, be, nu, src, stok: (layer, be[i], 0, 0)
    w_up = pl.BlockSpec((None, None, D_MODEL, F_EXPERT), w_idx)
    return pl.pallas_call(
        _expert_kernel,
        grid_spec=pltpu.PrefetchScalarGridSpec(
            num_scalar_prefetch=4,
            grid=(MOE_NB,),
            in_specs=[pl.BlockSpec(memory_space=pl.ANY), w_up, w_up,
                      pl.BlockSpec((None, None, F_EXPERT, D_MODEL), w_idx)],
            out_specs=pl.BlockSpec((MOE_MB, D_MODEL), lambda i, be, nu, src, stok: (i, 0)),
            scratch_shapes=[pltpu.VMEM((2, MOE_MB, D_MODEL), F32),
                            pltpu.VMEM((D_MODEL, F_EXPERT), BF16), pltpu.VMEM((D_MODEL, F_EXPERT), BF16),
                            pltpu.VMEM((F_EXPERT, D_MODEL), BF16), pltpu.SemaphoreType.DMA((2,))]),
        out_shape=jax.ShapeDtypeStruct((MOE_SLOTS, D_MODEL), F32),
        compiler_params=_cparams(1), name="moe_experts",
    )(block_e, n_used, src, sorted_tok, x, wg, wu, wd)


def _shared_up_kernel(x_ref, wg_ref, wu_ref, o_ref, wg_bf, wu_bf):
    @pl.when(pl.program_id(0) == 0)
    def _():
        wg_bf[...] = wg_ref[...].astype(BF16)
        wu_bf[...] = wu_ref[...].astype(BF16)

    x = x_ref[...].astype(BF16)
    g = jnp.dot(x, wg_bf[...], preferred_element_type=F32)
    u = jnp.dot(x, wu_bf[...], preferred_element_type=F32)
    o_ref[...] = (_silu(g) * u).astype(o_ref.dtype)


def _shared_up(h, wg, wu, layer):
    w = pl.BlockSpec((None, D_MODEL, F_SHARED), lambda i: (layer, 0, 0))
    return pl.pallas_call(
        _shared_up_kernel,
        grid=(T_ALL // TM,),
        in_specs=[pl.BlockSpec((TM, D_MODEL), lambda i: (i, 0)), w, w],
        out_specs=pl.BlockSpec((TM, F_SHARED), lambda i: (i, 0)),
        out_shape=jax.ShapeDtypeStruct((T_ALL, F_SHARED), BF16),
        scratch_shapes=[pltpu.VMEM((D_MODEL, F_SHARED), BF16), pltpu.VMEM((D_MODEL, F_SHARED), BF16)],
        compiler_params=_cparams(1), name="shared_up",
    )(h, wg, wu)


def _combine_kernel(slot_ref, nxt_ref, y_hbm, rw_ref, act_ref, wd_ref, x_ref, mod_ref, o_ref, buf, wd_bf, sems, *,
                    gate_row):
    i = pl.program_id(0)
    n_rows = TOP_K * CMB_TM

    def row_copy(idx_ref, slot, j):
        return pltpu.make_async_copy(y_hbm.at[pl.ds(idx_ref[0, 0, j], 1), :], buf.at[slot, pl.ds(j, 1), :],
                                     sems.at[slot])

    @pl.when(i == 0)
    def _():
        wd_bf[...] = wd_ref[...].astype(BF16)

        def body(j, carry):
            row_copy(slot_ref, 0, j).start()
            return carry
        lax.fori_loop(0, n_rows, body, 0, unroll=8)

    def tile(prefetch):
        slot = i % 2
        pltpu.make_async_copy(y_hbm.at[pl.ds(0, n_rows), :], buf.at[slot], sems.at[slot]).wait()
        if prefetch:
            for j in range(n_rows):
                row_copy(nxt_ref, 1 - slot, j).start()
        y = jnp.dot(act_ref[...], wd_bf[...], preferred_element_type=F32)
        for k in range(TOP_K):
            y = y + rw_ref[:, k:k + 1] * buf[slot, k * CMB_TM:(k + 1) * CMB_TM, :]
        o_ref[...] = x_ref[...] + mod_ref[gate_row:gate_row + 1, :] * y

    @pl.when(i + 1 < pl.num_programs(0))
    def _():
        tile(True)

    @pl.when(i + 1 == pl.num_programs(0))
    def _():
        tile(False)


def _combine(y_sorted, slot_of, route_w, act, sh_wd, x, mod, layer):
    nt = T_ALL // CMB_TM
    slots = slot_of.reshape(nt, CMB_TM, TOP_K).transpose(0, 2, 1).reshape(nt, 1, TOP_K * CMB_TM)
    slot_spec = lambda nxt: pl.BlockSpec((1, 1, TOP_K * CMB_TM), lambda i: (jnp.minimum(i + nxt, nt - 1), 0, 0),
                                         memory_space=pltpu.SMEM)
    return pl.pallas_call(
        functools.partial(_combine_kernel, gate_row=5),
        grid=(nt,),
        in_specs=[slot_spec(0), slot_spec(1),
                  pl.BlockSpec(memory_space=pl.ANY),
                  pl.BlockSpec((CMB_TM, TOP_K), lambda i: (i, 0)),
                  pl.BlockSpec((CMB_TM, F_SHARED), lambda i: (i, 0)),
                  pl.BlockSpec((None, F_SHARED, D_MODEL), lambda i: (layer, 0, 0)),
                  pl.BlockSpec((CMB_TM, D_MODEL), lambda i: (i, 0)),
                  pl.BlockSpec((None, 6, D_MODEL), lambda i: (_group_of_tile(i, CMB_TM), 0, 0))],
        out_specs=pl.BlockSpec((CMB_TM, D_MODEL), lambda i: (i, 0)),
        out_shape=jax.ShapeDtypeStruct((T_ALL, D_MODEL), F32),
        scratch_shapes=[pltpu.VMEM((2, TOP_K * CMB_TM, D_MODEL), F32), pltpu.VMEM((F_SHARED, D_MODEL), BF16),
                        pltpu.SemaphoreType.DMA((2,))],
        compiler_params=_cparams(1), name="moe_combine",
    )(slots, slots, y_sorted, route_w, act, sh_wd, x, mod)


def kernel(x_prompt, x_sample, cache_na_k, cache_na_v, state_gla, state_s5, c, c_ctx, ada_w, ada_b, norm1_g, norm2_g, w_in, na_rpb, gla_wa, gla_ba, gla_norm_g, s5_a_re, s5_a_im, s5_log_dt, s5_b_re, s5_b_im, s5_c_re, s5_c_im, s5_d, s5_w_glu, w_br_na, w_br_gla, w_br_s5, w_merge, w_out, router_w, router_bias, exp_wg, exp_wu, exp_wd, sh_wg, sh_wu, sh_wd, final_norm_g):
    x = jnp.concatenate([x_prompt.reshape(T_CTX, D_MODEL), x_sample.reshape(T_LAT, D_MODEL)], axis=0)
    cvec = jnp.concatenate([c_ctx[None, :], c, jnp.zeros((MOD_ROWS - N_MOD, D_MODEL), F32)], axis=0)
    mods = _ada(cvec, ada_w, ada_b).reshape(DEPTH, MOD_ROWS, 6, D_MODEL)
    cos, sin = _rope_tables()
    zero_gla = jnp.zeros((BATCH, 2, GLA_HEADS, GLA_DV, GLA_DK), F32)
    zero_s5 = jnp.zeros((S5_GROUPS, 4, BATCH, S5_N), F32)
    new_k, new_v, new_gla, new_s5 = [], [], [], []
    for l in range(DEPTH):
        mod = mods[l]
        h = _norm(x, norm1_g[l], mod, (0, 1), BF16)
        u = _mm(h, w_in, l, 1024, n=IN_MAIN)
        lr = _mm(h, w_in[l, :, IN_MAIN:IN_MAIN + 2 * GLA_LR], None, 2 * GLA_LR)
        su = _mm(h, w_in[l, :, IN_MAIN + 2 * GLA_LR:], None, S5_W)
        gates = _mm(h, w_merge, l, 1024, act="sigmoid")

        o_na = jnp.concatenate([_ctx_attn(u),
                                _na_attn(u, cache_na_k, cache_na_v, _na_bias(na_rpb[l]), l)], axis=0)

        s0_lat = state_gla[:, l].transpose(0, 1, 2, 4, 3)
        og_ctx, sfin = _gla(u, lr, gla_wa, gla_ba, gla_norm_g, zero_gla, cos, sin, l, latent=False)
        og_lat, _ = _gla(u, lr, gla_wa, gla_ba, gla_norm_g, s0_lat, cos, sin, l, latent=True)
        o_gla = jnp.concatenate([og_ctx, og_lat], axis=0)

        ops, step = _s5_operators(s5_a_re[l], s5_a_im[l], s5_log_dt[l], s5_b_re[l], s5_b_im[l],
                                  s5_c_re[l], s5_c_im[l])
        st = state_s5[:, l].astype(F32)
        h0_lat = st.transpose(2, 1, 4, 0, 3).reshape(S5_GROUPS, 4, DEC_BATCH, S5_N)
        y_ctx, hfin = _s5(su[:T_CTX], ops, step, zero_s5, nb=BATCH, seq=SEQ)
        y_lat, _ = _s5(su[T_CTX:], ops, step, h0_lat, nb=DEC_BATCH, seq=DEC_SEQ)
        zz = _glu(jnp.concatenate([y_ctx, y_lat], axis=0), su, s5_d, s5_w_glu, l)

        merged = _merge(o_na, o_gla, zz, gates, w_br_na, w_br_gla, w_br_s5, l)
        x = _out_proj(merged, w_out, x, mod, l)

        h2 = _norm(x, norm2_g[l], mod, (3, 4), F32)
        idx_t, w_t = _router(h2, router_w[l].T, router_bias[l])
        sorted_tok, slot_of, block_e, src, n_used = _dispatch(idx_t.T)
        y_sorted = _experts(h2, sorted_tok, block_e, src, n_used, exp_wg, exp_wu, exp_wd, l)
        act = _shared_up(h2, sh_wg, sh_wu, l)
        x = _combine(y_sorted, slot_of, w_t.T, act, sh_wd, x, mod, l)

        new_k.append(u[:T_CTX, NA_W:2 * NA_W].reshape(BATCH, SEQ, NA_HEADS, NA_DH))
        new_v.append(u[:T_CTX, 2 * NA_W:3 * NA_W].reshape(BATCH, SEQ, NA_HEADS, NA_DH))
        new_gla.append(sfin.transpose(0, 1, 2, 4, 3))
        new_s5.append(hfin.reshape(S5_GROUPS, 2, 2, BATCH, S5_N).transpose(3, 1, 0, 4, 2))

    y = _norm(x, final_norm_g, None, None, F32)
    return (y[:T_CTX].reshape(BATCH, SEQ, D_MODEL), y[T_CTX:].reshape(DEC_BATCH, DEC_SEQ, D_MODEL),
            jnp.stack(new_k, axis=1), jnp.stack(new_v, axis=1),
            jnp.stack(new_gla, axis=1), jnp.stack(new_s5, axis=1))
```

```python
import functools
import math

import jax
import jax.numpy as jnp
from jax import lax
from jax.experimental import pallas as pl
from jax.experimental.pallas import tpu as pltpu

F32 = jnp.float32
BF16 = jnp.bfloat16

D_MODEL = 2048
BATCH = 16
SEQ = 256
DEPTH = 2
DEC_BATCH = 2
DEC_SEQ = 1024
PAST_LEN = 512
GRID_W = 64
NA_HEADS = 8
NA_DH = 128
NA_W = NA_HEADS * NA_DH
NA_ROWS = 8
NA_COLS = 16
GLA_HEADS = 4
GLA_DK = 128
GLA_DV = 256
GLA_QK_W = GLA_HEADS * GLA_DK
GLA_V_W = GLA_HEADS * GLA_DV
GLA_LR = 16
GLA_TAU = 16.0
GLA_CHUNK = 64
GLA_SUB = 16
GLA_UNROLL = 4
ROPE_BASE = 10000.0
S5_W = 1024
S5_GS = 16
S5_GROUPS = S5_W // S5_GS
S5_N = 64
S5_T = 16
IN_W = 3 * NA_W + 2 * GLA_QK_W + 2 * GLA_V_W + 2 * GLA_LR + S5_W
IN_MAIN = 3 * NA_W + 2 * GLA_QK_W + 2 * GLA_V_W
N_EXPERTS = 64
TOP_K = 8
N_GROUPS = 8
TOPK_GROUPS = 4
F_EXPERT = 512
F_SHARED = 512
ROUTED_SCALE = 2.5
EPS = 1e-6

T_CTX = BATCH * SEQ
T_LAT = DEC_BATCH * DEC_SEQ
T_ALL = T_CTX + T_LAT
N_MOD = 1 + DEC_BATCH
MOD_ROWS = 8

TM = 512
MOE_MB = 256
MOE_NB = T_ALL * TOP_K // MOE_MB + N_EXPERTS
MOE_SLOTS = MOE_NB * MOE_MB
CMB_TM = 128
NEG = -1e30

VMEM_LIMIT = 56 * 1024 * 1024

NT = (((1,), (1,)), ((), ()))
TN = (((0,), (0,)), ((), ()))


def _cparams(n_axes):
    return pltpu.CompilerParams(dimension_semantics=("arbitrary",) * n_axes,
                                vmem_limit_bytes=VMEM_LIMIT)


def _group_of_tile(i, tm):
    row = i * tm
    return jnp.where(row < T_CTX, 0, 1 + (row - T_CTX) // DEC_SEQ)


def _silu(x):
    return x * jax.nn.sigmoid(x)


def _gelu_tanh(x):
    return 0.5 * x * (1.0 + jnp.tanh(math.sqrt(2.0 / math.pi) * (x + 0.044715 * (x * x * x))))


def _ada_kernel(c_ref, w_ref, b_ref, o_ref):
    s = _silu(c_ref[...]).astype(BF16)
    o_ref[...] = jnp.dot(s, w_ref[...].astype(BF16), preferred_element_type=F32) + b_ref[...]


def _ada(cvec, ada_w, ada_b):
    tn = 1024
    return pl.pallas_call(
        _ada_kernel,
        grid=(DEPTH, 6 * D_MODEL // tn),
        in_specs=[pl.BlockSpec((MOD_ROWS, D_MODEL), lambda l, j: (0, 0)),
                  pl.BlockSpec((None, D_MODEL, tn), lambda l, j: (l, 0, j)),
                  pl.BlockSpec((None, 1, tn), lambda l, j: (l, 0, j))],
        out_specs=pl.BlockSpec((None, MOD_ROWS, tn), lambda l, j: (l, 0, j)),
        out_shape=jax.ShapeDtypeStruct((DEPTH, MOD_ROWS, 6 * D_MODEL), F32),
        compiler_params=_cparams(2), name="ada",
    )(cvec, ada_w, ada_b.reshape(DEPTH, 1, 6 * D_MODEL))


def _norm_kernel(x_ref, g_ref, *rest, rows):
    o_ref = rest[-1]
    x = x_ref[...]
    y = x * lax.rsqrt(jnp.mean(x * x, axis=-1, keepdims=True) + EPS) * g_ref[...]
    if rows is not None:
        mod_ref = rest[0]
        y = y * (1.0 + mod_ref[rows[1]:rows[1] + 1, :]) + mod_ref[rows[0]:rows[0] + 1, :]
    o_ref[...] = y.astype(o_ref.dtype)


def _norm(x, g, mod, rows, out_dtype):
    tm = 256
    in_specs = [pl.BlockSpec((tm, D_MODEL), lambda i: (i, 0)),
                pl.BlockSpec((1, D_MODEL), lambda i: (0, 0))]
    args = [x, g.reshape(1, D_MODEL)]
    if rows is not None:
        in_specs.append(pl.BlockSpec((None, 6, D_MODEL), lambda i: (_group_of_tile(i, tm), 0, 0)))
        args.append(mod)
    return pl.pallas_call(
        functools.partial(_norm_kernel, rows=rows),
        grid=(T_ALL // tm,),
        in_specs=in_specs,
        out_specs=pl.BlockSpec((tm, D_MODEL), lambda i: (i, 0)),
        out_shape=jax.ShapeDtypeStruct((T_ALL, D_MODEL), out_dtype),
        compiler_params=_cparams(1), name="norm",
    )(*args)


def _mm_kernel(x_ref, w_ref, o_ref, wbf_ref, *, act):
    @pl.when(pl.program_id(1) == 0)
    def _():
        wbf_ref[...] = w_ref[...].astype(BF16)

    acc = jnp.dot(x_ref[...].astype(BF16), wbf_ref[...], preferred_element_type=F32)
    if act == "sigmoid":
        acc = jax.nn.sigmoid(acc)
    o_ref[...] = acc.astype(o_ref.dtype)


def _mm(x, w, layer, tn, act=None, out_dtype=F32, n=None):
    k = x.shape[1]
    n = w.shape[-1] if n is None else n
    if layer is None:
        w_spec = pl.BlockSpec((k, tn), lambda j, i: (0, j))
    else:
        w_spec = pl.BlockSpec((None, k, tn), lambda j, i: (layer, 0, j))
    return pl.pallas_call(
        functools.partial(_mm_kernel, act=act),
        grid=(n // tn, T_ALL // TM),
        in_specs=[pl.BlockSpec((TM, k), lambda j, i: (i, 0)), w_spec],
        out_specs=pl.BlockSpec((TM, tn), lambda j, i: (i, j)),
        out_shape=jax.ShapeDtypeStruct((T_ALL, n), out_dtype),
        scratch_shapes=[pltpu.VMEM((k, tn), BF16)],
        compiler_params=_cparams(2), name="mm",
    )(x, w)


def _ctx_attn_kernel(q_ref, k_ref, v_ref, o_ref):
    scale = NA_DH ** -0.5
    for h in range(NA_HEADS):
        sl = slice(h * NA_DH, (h + 1) * NA_DH)
        q = q_ref[:, sl].astype(BF16)
        k = k_ref[:, sl].astype(BF16)
        v = v_ref[:, sl].astype(BF16)
        s = lax.dot_general(q, k, NT, preferred_element_type=F32) * scale
        p = jnp.exp(s - jnp.max(s, axis=-1, keepdims=True))
        o = jnp.dot(p.astype(BF16), v, preferred_element_type=F32) / jnp.sum(p, axis=-1, keepdims=True)
        o_ref[:, sl] = o.astype(o_ref.dtype)


def _ctx_attn(u):
    spec = lambda cb: pl.BlockSpec((SEQ, NA_W), lambda b: (b, cb))
    return pl.pallas_call(
        _ctx_attn_kernel,
        grid=(BATCH,),
        in_specs=[spec(0), spec(1), spec(2)],
        out_specs=pl.BlockSpec((SEQ, NA_W), lambda b: (b, 0)),
        out_shape=jax.ShapeDtypeStruct((T_CTX, NA_W), BF16),
        compiler_params=_cparams(1), name="ctx_attn",
    )(u, u, u)


NA_GRID_ROWS = DEC_SEQ // GRID_W
NA_KR = min(NA_ROWS, NA_GRID_ROWS)
NA_LOC = NA_KR * GRID_W


def _na_bias(rpb):
    n_off = 2 * NA_ROWS - 1
    pad = GRID_W - NA_COLS
    period = 2 * GRID_W
    p = jnp.pad(rpb, ((0, 0), (0, 0), (pad, period - pad - (2 * NA_COLS - 1))))
    hank = jnp.tile(p, (1, 1, GRID_W + 1))[..., :GRID_W * (period + 1)]
    hank = hank.reshape(NA_HEADS, n_off, GRID_W, period + 1)[..., :GRID_W]
    band = hank[:, :, ::-1, :]
    qc = jnp.arange(GRID_W)[:, None]
    kc = jnp.arange(GRID_W)[None, :]
    win = jnp.clip(qc - NA_COLS // 2, 0, GRID_W - NA_COLS)
    valid = (kc >= win) & (kc < win + NA_COLS)
    band = jnp.where(valid[None, None], band, NEG)
    return band.transpose(0, 2, 1, 3).reshape(NA_HEADS, GRID_W, n_off * GRID_W)


def _na_kernel(q_ref, k_ref, v_ref, ck_ref, cv_ref, bias_ref, o_ref):
    scale = NA_DH ** -0.5
    kb = k_ref[...].astype(BF16)
    vb = v_ref[...].astype(BF16)
    ck = ck_ref[...].astype(BF16)
    cv = cv_ref[...].astype(BF16)
    for r in range(NA_GRID_ROWS):
        first = min(max(r - NA_KR // 2, 0), NA_GRID_ROWS - NA_KR)
        off = first - r + NA_ROWS - 1
        rows = slice(r * GRID_W, (r + 1) * GRID_W)
        keys = slice(first * GRID_W, first * GRID_W + NA_LOC)
        q = q_ref[rows, :].astype(BF16)
        s1 = (lax.dot_general(q, kb[keys, :], NT, preferred_element_type=F32) * scale
              + bias_ref[:, off * GRID_W:off * GRID_W + NA_LOC])
        s2 = lax.dot_general(q, ck, NT, preferred_element_type=F32) * scale
        m = jnp.maximum(jnp.max(s1, axis=-1, keepdims=True), jnp.max(s2, axis=-1, keepdims=True))
        p1 = jnp.exp(s1 - m)
        p2 = jnp.exp(s2 - m)
        den = jnp.sum(p1, axis=-1, keepdims=True) + jnp.sum(p2, axis=-1, keepdims=True)
        o = (jnp.dot(p1.astype(BF16), vb[keys, :], preferred_element_type=F32)
             + jnp.dot(p2.astype(BF16), cv, preferred_element_type=F32))
        o_ref[rows, :] = (o / den).astype(o_ref.dtype)


def _na_attn(u, cache_k, cache_v, bias, layer):
    lat_sb = T_CTX // DEC_SEQ
    ck = cache_k.reshape(DEC_BATCH, DEPTH, PAST_LEN, NA_W)
    cv = cache_v.reshape(DEC_BATCH, DEPTH, PAST_LEN, NA_W)
    qkv_spec = lambda cb: pl.BlockSpec((DEC_SEQ, NA_DH), lambda b, h: (lat_sb + b, cb * NA_HEADS + h))
    c_spec = pl.BlockSpec((None, None, PAST_LEN, NA_DH), lambda b, h: (b, layer, 0, h))
    return pl.pallas_call(
        _na_kernel,
        grid=(DEC_BATCH, NA_HEADS),
        in_specs=[qkv_spec(0), qkv_spec(1), qkv_spec(2), c_spec, c_spec,
                  pl.BlockSpec((None, GRID_W, (2 * NA_ROWS - 1) * GRID_W), lambda b, h: (h, 0, 0))],
        out_specs=pl.BlockSpec((DEC_SEQ, NA_DH), lambda b, h: (b, h)),
        out_shape=jax.ShapeDtypeStruct((T_LAT, NA_W), BF16),
        compiler_params=_cparams(2), name="na_attn",
    )(u, u, u, ck, cv, bias)


def _rope_tables():
    half = GLA_DK // 2
    nf = half // 2
    t = jnp.arange(DEC_SEQ)
    freqs = ROPE_BASE ** (-jnp.arange(nf, dtype=F32) / nf)
    ang_r = (t // GRID_W).astype(F32)[:, None] * freqs
    ang_c = (t % GRID_W).astype(F32)[:, None] * freqs
    cos = jnp.concatenate([jnp.cos(ang_r), jnp.cos(ang_r), jnp.cos(ang_c), jnp.cos(ang_c)], axis=-1)
    sin = jnp.concatenate([-jnp.sin(ang_r), jnp.sin(ang_r), -jnp.sin(ang_c), jnp.sin(ang_c)], axis=-1)
    return cos, sin


def _split_bf16(x):
    hi = x.astype(BF16)
    return hi, (x - hi.astype(F32)).astype(BF16)


def _gla_kernel(q_ref, k_ref, v_ref, gg_ref, lr_ref, wa_ref, ba_ref, ng_ref, cos_ref, sin_ref, s0_ref,
                o_ref, sfin_ref, qs, ks, las, o_acc, st, *, seq, rope):
    nc = seq // GLA_CHUNK
    nq = GLA_DK // 4

    def rot(x):
        lane = lax.broadcasted_iota(jnp.int32, x.shape, 1)
        partner = jnp.where((lane % (2 * nq)) < nq,
                            pltpu.roll(x, GLA_DK - nq, axis=1), pltpu.roll(x, nq, axis=1))
        return x * cos_ref[...] + partner * sin_ref[...]

    q = q_ref[...] * (GLA_DK ** -0.5)
    k = k_ref[...]
    if rope:
        q = rot(q)
        k = rot(k)
    qs[...] = q
    ks[...] = k
    lr = lr_ref[...]
    for d in range(2):
        z = jnp.dot(lr[:, d * GLA_LR:(d + 1) * GLA_LR].astype(BF16), wa_ref[d].astype(BF16),
                    preferred_element_type=F32) + ba_ref[d:d + 1, :]
        las[d] = -(jnp.maximum(-z, 0.0) + jnp.log1p(jnp.exp(-jnp.abs(z)))) / GLA_TAU
        st[d] = s0_ref[d]

    row = lax.broadcasted_iota(jnp.int32, (GLA_CHUNK, GLA_CHUNK), 0)
    col = lax.broadcasted_iota(jnp.int32, (GLA_CHUNK, GLA_CHUNK), 1)
    key_row = lax.broadcasted_iota(jnp.int32, (GLA_CHUNK, 1), 0)

    def chunk(c, d):
        rev = d == 1
        rows = pl.ds(pl.multiple_of(c * GLA_CHUNK, GLA_CHUNK), GLA_CHUNK)
        qc = qs[rows, :]
        kc = ks[rows, :]
        vc = v_ref[rows, :].astype(BF16)
        la = las[d, rows, :]
        causal = (col >= row) if rev else (col <= row)
        tri = jnp.where(causal, 1.0, 0.0).astype(BF16)
        la_hi, la_lo = _split_bf16(la)
        b = (jnp.dot(tri, la_hi, preferred_element_type=F32)
             + jnp.dot(tri, la_lo, preferred_element_type=F32))
        bex = b - la
        b_last = b[0:1, :] if rev else b[GLA_CHUNK - 1:GLA_CHUNK, :]
        blocks = []
        for i in range(GLA_CHUNK // GLA_SUB):
            lo, hi = i * GLA_SUB, (i + 1) * GLA_SUB
            ref = bex[hi - 1:hi, :] if rev else bex[lo:lo + 1, :]
            qt = (qc[lo:hi, :] * jnp.exp(b[lo:hi, :] - ref)).astype(BF16)
            allowed = (key_row >= lo) if rev else (key_row < hi)
            kt = (kc * jnp.exp(jnp.where(allowed, ref - b, -jnp.inf))).astype(BF16)
            blocks.append(lax.dot_general(qt, kt, NT, preferred_element_type=F32))
        att = jnp.where(causal, jnp.concatenate(blocks, axis=0), 0.0)
        s_t = st[d]
        o = (jnp.dot(att.astype(BF16), vc, preferred_element_type=F32)
             + lax.dot_general((qc * jnp.exp(b)).astype(BF16), s_t.astype(BF16), NT,
                               preferred_element_type=F32))
        khat = (kc * jnp.exp(b_last - b)).astype(BF16)
        st[d] = s_t * jnp.exp(b_last) + lax.dot_general(vc, khat, TN, preferred_element_type=F32)
        o_acc[d, rows, :] = o

    def both(c, carry):
        chunk(c, 0)
        chunk(nc - 1 - c, 1)
        return carry

    lax.fori_loop(0, nc, both, 0, unroll=GLA_UNROLL)
    sfin_ref[...] = st[...]
    o = o_acc[0] + o_acc[1]
    o = o * lax.rsqrt(jnp.mean(o * o, axis=-1, keepdims=True) + EPS) * ng_ref[...]
    o_ref[...] = (o * _silu(gg_ref[...])).astype(o_ref.dtype)


def _gla(u, lr, wa, ba, norm_g, s0_t, cos, sin, layer, *, latent):
    seq, nb, first = (DEC_SEQ, DEC_BATCH, T_CTX // DEC_SEQ) if latent else (SEQ, BATCH, 0)
    qk_cb = 3 * NA_W // GLA_DK
    v_cb = (3 * NA_W + 2 * GLA_QK_W) // GLA_DV
    row = lambda w, cb: pl.BlockSpec((seq, w), lambda b, h: (first + b, cb + h))
    return pl.pallas_call(
        functools.partial(_gla_kernel, seq=seq, rope=latent),
        grid=(nb, GLA_HEADS),
        in_specs=[row(GLA_DK, qk_cb), row(GLA_DK, qk_cb + GLA_HEADS), row(GLA_DV, v_cb),
                  row(GLA_DV, v_cb + GLA_HEADS),
                  pl.BlockSpec((seq, 2 * GLA_LR), lambda b, h: (first + b, 0)),
                  pl.BlockSpec((None, 2, GLA_LR, GLA_DK), lambda b, h: (layer, 0, 0, h)),
                  pl.BlockSpec((None, 2, GLA_DK), lambda b, h: (layer, 0, h)),
                  pl.BlockSpec((None, 1, GLA_DV), lambda b, h: (layer, 0, h)),
                  pl.BlockSpec((seq, GLA_DK), lambda b, h: (0, 0)),
                  pl.BlockSpec((seq, GLA_DK), lambda b, h: (0, 0)),
                  pl.BlockSpec((None, 2, None, GLA_DV, GLA_DK), lambda b, h: (b, 0, h, 0, 0))],
        out_specs=[pl.BlockSpec((seq, GLA_DV), lambda b, h: (b, h)),
                   pl.BlockSpec((None, 2, None, GLA_DV, GLA_DK), lambda b, h: (b, 0, h, 0, 0))],
        out_shape=[jax.ShapeDtypeStruct((nb * seq, GLA_V_W), BF16),
                   jax.ShapeDtypeStruct((nb, 2, GLA_HEADS, GLA_DV, GLA_DK), F32)],
        scratch_shapes=[pltpu.VMEM((seq, GLA_DK), F32), pltpu.VMEM((seq, GLA_DK), F32),
                        pltpu.VMEM((2, seq, GLA_DK), F32), pltpu.VMEM((2, seq, GLA_DV), F32),
                        pltpu.VMEM((2, GLA_DV, GLA_DK), F32)],
        compiler_params=_cparams(2), name="gla",
    )(u, u, u, u, lr, wa, ba, norm_g.reshape(DEPTH, 1, GLA_V_W), cos[:seq], sin[:seq], s0_t)


def _s5_operators(a_re, a_im, log_dt, b_re, b_im, c_re, c_im):
    hp = lax.Precision.HIGHEST
    lam = lax.complex(a_re.astype(F32), a_im.astype(F32))
    lam_dt = lam * jnp.exp(log_dt.astype(F32))[..., None]
    a_bar = jnp.exp(lam_dt)
    b_bar = ((a_bar - 1.0) / lam)[..., None] * lax.complex(b_re.astype(F32), b_im.astype(F32))
    cc = lax.complex(c_re.astype(F32), c_im.astype(F32))
    taus = jnp.arange(S5_T + 1, dtype=F32)
    pw = jnp.exp(lam_dt[:, :, None, :] * taus[None, None, :, None])
    kern = jnp.real(jnp.einsum("dgjn,dgtn,dgni->dgitj", cc, pw[:, :, :S5_T], b_bar, precision=hp))
    lanes = S5_T * S5_GS
    def toeplitz(lags):
        rows = [lags[:, :, S5_T - 1 - s:2 * S5_T - 1 - s, :] for s in range(S5_T)]
        return jnp.stack(rows, axis=1).reshape(S5_GROUPS, lanes, lanes)

    toep_f = toeplitz(jnp.pad(kern[0], ((0, 0), (0, 0), (S5_T - 1, 0), (0, 0))))
    toep_b = toeplitz(jnp.pad(kern[1][:, :, ::-1, :], ((0, 0), (0, 0), (0, S5_T - 1), (0, 0))))

    def state_in(pwd, bd):
        m = pwd[:, :, None, :] * bd.transpose(0, 2, 1)[:, None, :, :]
        m = m.reshape(S5_GROUPS, lanes, S5_N)
        return [jnp.real(m), jnp.imag(m)]

    p = jnp.stack(state_in(pw[0, :, S5_T - 1::-1], b_bar[0])
                  + state_in(pw[1, :, :S5_T], b_bar[1]), axis=1)

    def state_out(pwd, cd):
        m = pwd.transpose(0, 2, 1)[:, :, :, None] * cd.transpose(0, 2, 1)[:, :, None, :]
        m = m.reshape(S5_GROUPS, S5_N, lanes)
        return [jnp.real(m), -jnp.imag(m)]

    q = jnp.stack(state_out(pw[0, :, 1:S5_T + 1], cc[0])
                  + state_out(pw[1, :, S5_T:0:-1], cc[1]), axis=1)
    ops = [m.astype(BF16) for m in (toep_f, toep_b, p, q)]
    a_t = pw[:, :, S5_T]
    step = jnp.stack([jnp.real(a_t[0]), jnp.imag(a_t[0]), jnp.real(a_t[1]), jnp.imag(a_t[1])], axis=1)
    return ops, step


def _s5_kernel(u_ref, tf_ref, tb_ref, p_ref, q_ref, step_ref, h0_ref, y_ref, hfin_ref, e_scr, hin_scr, *, nc, nb):
    u = u_ref[...]
    for m in range(4):
        e_scr[m] = jnp.dot(u, p_ref[m], preferred_element_type=F32)
    ar = [step_ref[0:1, :], step_ref[2:3, :]]
    ai = [step_ref[1:2, :], step_ref[3:4, :]]
    hr = [h0_ref[0], h0_ref[2]]
    hi = [h0_ref[1], h0_ref[3]]
    for c in range(nc):
        for d in range(2):
            rows = slice(c * nb, (c + 1) * nb) if d == 0 else slice((nc - 1 - c) * nb, (nc - c) * nb)
            hin_scr[2 * d, rows, :] = hr[d]
            hin_scr[2 * d + 1, rows, :] = hi[d]
            hr[d], hi[d] = (hr[d] * ar[d] - hi[d] * ai[d] + e_scr[2 * d, rows, :],
                            hi[d] * ar[d] + hr[d] * ai[d] + e_scr[2 * d + 1, rows, :])
    for d in range(2):
        hfin_ref[2 * d] = hr[d]
        hfin_ref[2 * d + 1] = hi[d]
    y = jnp.dot(u, tf_ref[...], preferred_element_type=F32) + jnp.dot(u, tb_ref[...], preferred_element_type=F32)
    for m in range(4):
        y += jnp.dot(hin_scr[m].astype(BF16), q_ref[m], preferred_element_type=F32)
    y_ref[...] = y


def _s5(su, ops, step, h0, *, nb, seq):
    nc = seq // S5_T
    nbp = max(nb, 8)
    lanes = S5_T * S5_GS
    x = su.reshape(nb, nc, S5_T, S5_GROUPS, S5_GS).transpose(3, 1, 0, 2, 4)
    if nbp != nb:
        x = jnp.pad(x, ((0, 0), (0, 0), (0, nbp - nb), (0, 0), (0, 0)))
        h0 = jnp.pad(h0, ((0, 0), (0, 0), (0, nbp - nb), (0, 0)))
    x = x.reshape(S5_GROUPS, nc * nbp, lanes).astype(BF16)
    rows = nc * nbp
    g3 = lambda a, b: pl.BlockSpec((None, a, b), lambda g: (g, 0, 0))
    g4 = lambda a, b: pl.BlockSpec((None, 4, a, b), lambda g: (g, 0, 0, 0))
    y, hfin = pl.pallas_call(
        functools.partial(_s5_kernel, nc=nc, nb=nbp),
        grid=(S5_GROUPS,),
        in_specs=[g3(rows, lanes), g3(lanes, lanes), g3(lanes, lanes), g4(lanes, S5_N), g4(S5_N, lanes),
                  g3(4, S5_N), g4(nbp, S5_N)],
        out_specs=[g3(rows, lanes), g4(nbp, S5_N)],
        out_shape=[jax.ShapeDtypeStruct((S5_GROUPS, rows, lanes), F32),
                   jax.ShapeDtypeStruct((S5_GROUPS, 4, nbp, S5_N), F32)],
        scratch_shapes=[pltpu.VMEM((4, rows, S5_N), F32), pltpu.VMEM((4, rows, S5_N), F32)],
        compiler_params=_cparams(1), name="s5",
    )(x, *ops, step, h0)
    y = y.reshape(S5_GROUPS, nc, nbp, S5_T, S5_GS)[:, :, :nb].transpose(2, 1, 3, 0, 4)
    return y.reshape(nb * seq, S5_W), hfin


def _glu_kernel(y_ref, su_ref, d_ref, w_ref, o_ref, wbf_ref):
    @pl.when(pl.program_id(0) == 0)
    def _():
        wbf_ref[...] = w_ref[...].astype(BF16)

    z = _gelu_tanh(y_ref[...] + d_ref[...] * su_ref[...])
    gate = jax.nn.sigmoid(jnp.dot(z.astype(BF16), wbf_ref[...], preferred_element_type=F32))
    o_ref[...] = (z * gate).astype(o_ref.dtype)


def _glu(y, su, d, w_glu, layer):
    row = pl.BlockSpec((TM, S5_W), lambda i: (i, 0))
    return pl.pallas_call(
        _glu_kernel,
        grid=(T_ALL // TM,),
        in_specs=[row, row, pl.BlockSpec((None, 1, S5_W), lambda i: (layer, 0, 0)),
                  pl.BlockSpec((None, S5_W, S5_W), lambda i: (layer, 0, 0))],
        out_specs=row,
        out_shape=jax.ShapeDtypeStruct((T_ALL, S5_W), BF16),
        scratch_shapes=[pltpu.VMEM((S5_W, S5_W), BF16)],
        compiler_params=_cparams(1), name="glu",
    )(y, su, d.reshape(DEPTH, 1, S5_W), w_glu)


def _merge_kernel(a_ref, b_ref, c_ref, ga_ref, gb_ref, gc_ref, wa_ref, wb_ref, wc_ref, o_ref, wbf_ref):
    @pl.when(pl.program_id(1) == 0)
    def _():
        wbf_ref[0] = wa_ref[...].astype(BF16)
        wbf_ref[1] = wb_ref[...].astype(BF16)
        wbf_ref[2] = wc_ref[...].astype(BF16)

    acc = ga_ref[...] * jnp.dot(a_ref[...], wbf_ref[0], preferred_element_type=F32)
    acc += gb_ref[...] * jnp.dot(b_ref[...], wbf_ref[1], preferred_element_type=F32)
    acc += gc_ref[...] * jnp.dot(c_ref[...], wbf_ref[2], preferred_element_type=F32)
    o_ref[...] = acc.astype(o_ref.dtype)


def _merge(o_na, o_gla, zz, gates, w_na, w_gla, w_s5, layer):
    tn = 1024
    ncb = D_MODEL // tn
    act = pl.BlockSpec((TM, NA_W), lambda j, i: (i, 0))
    gate = lambda k: pl.BlockSpec((TM, tn), lambda j, i: (i, k * ncb + j))
    w = pl.BlockSpec((None, NA_W, tn), lambda j, i: (layer, 0, j))
    return pl.pallas_call(
        _merge_kernel,
        grid=(ncb, T_ALL // TM),
        in_specs=[act, act, act, gate(0), gate(1), gate(2), w, w, w],
        out_specs=pl.BlockSpec((TM, tn), lambda j, i: (i, j)),
        out_shape=jax.ShapeDtypeStruct((T_ALL, D_MODEL), BF16),
        scratch_shapes=[pltpu.VMEM((3, NA_W, tn), BF16)],
        compiler_params=_cparams(2), name="merge",
    )(o_na, o_gla, zz, gates, gates, gates, w_na, w_gla, w_s5)


def _out_proj_kernel(m_ref, w_ref, x_ref, mod_ref, o_ref, wbf_ref, *, gate_row):
    @pl.when(pl.program_id(1) == 0)
    def _():
        wbf_ref[...] = w_ref[...].astype(BF16)

    y = jnp.dot(m_ref[...], wbf_ref[...], preferred_element_type=F32)
    o_ref[...] = x_ref[...] + mod_ref[gate_row:gate_row + 1, :] * y


def _out_proj(merged, w_out, x, mod, layer):
    tn = 1024
    return pl.pallas_call(
        functools.partial(_out_proj_kernel, gate_row=2),
        grid=(D_MODEL // tn, T_ALL // TM),
        in_specs=[pl.BlockSpec((TM, D_MODEL), lambda j, i: (i, 0)),
                  pl.BlockSpec((None, D_MODEL, tn), lambda j, i: (layer, 0, j)),
                  pl.BlockSpec((TM, tn), lambda j, i: (i, j)),
                  pl.BlockSpec((None, 6, tn), lambda j, i: (_group_of_tile(i, TM), 0, j))],
        out_specs=pl.BlockSpec((TM, tn), lambda j, i: (i, j)),
        out_shape=jax.ShapeDtypeStruct((T_ALL, D_MODEL), F32),
        scratch_shapes=[pltpu.VMEM((D_MODEL, tn), BF16)],
        compiler_params=_cparams(2), name="out_proj",
    )(merged, w_out, x, mod)


def _first_max(v, iota, n):
    m = jnp.max(v, axis=0, keepdims=True)
    first = jnp.min(jnp.where(v == m, iota, float(n)), axis=0, keepdims=True)
    return m, first


def _router_kernel(x_ref, wt_ref, bias_ref, idx_ref, w_ref):
    per = N_EXPERTS // N_GROUPS
    x_hi, x_lo = _split_bf16(x_ref[...])
    w_hi, w_lo = _split_bf16(wt_ref[...])
    logits = (lax.dot_general(w_hi, x_hi, NT, preferred_element_type=F32)
              + lax.dot_general(w_hi, x_lo, NT, preferred_element_type=F32)
              + lax.dot_general(w_lo, x_hi, NT, preferred_element_type=F32))
    scores = jax.nn.sigmoid(logits)
    sel = scores + bias_ref[...]
    tm = sel.shape[1]
    iota_g = lax.broadcasted_iota(jnp.int32, (per, tm), 0).astype(F32)
    grp_rows = []
    for g in range(N_GROUPS):
        v = sel[g * per:(g + 1) * per, :]
        m1, first = _first_max(v, iota_g, per)
        m2 = jnp.max(jnp.where(iota_g == first, -jnp.inf, v), axis=0, keepdims=True)
        grp_rows.append(m1 + m2)
    grp = jnp.concatenate(grp_rows, axis=0)
    iota_n = lax.broadcasted_iota(jnp.int32, (N_GROUPS, tm), 0).astype(F32)
    chosen = jnp.zeros((N_GROUPS, tm), F32)
    for _ in range(TOPK_GROUPS):
        _, first = _first_max(grp, iota_n, N_GROUPS)
        hit = iota_n == first
        chosen = jnp.where(hit, 1.0, chosen)
        grp = jnp.where(hit, -jnp.inf, grp)
    mask = jnp.concatenate([jnp.broadcast_to(chosen[g:g + 1, :], (per, tm)) for g in range(N_GROUPS)], axis=0)
    sel = jnp.where(mask > 0.5, sel, -jnp.inf)
    iota_e = lax.broadcasted_iota(jnp.int32, (N_EXPERTS, tm), 0).astype(F32)
    ids, ws = [], []
    for _ in range(TOP_K):
        _, first = _first_max(sel, iota_e, N_EXPERTS)
        hit = iota_e == first
        ids.append(first)
        ws.append(jnp.sum(jnp.where(hit, scores, 0.0), axis=0, keepdims=True))
        sel = jnp.where(hit, -jnp.inf, sel)
    w = jnp.concatenate(ws, axis=0)
    idx_ref[...] = jnp.concatenate(ids, axis=0).astype(jnp.int32)
    w_ref[...] = w / jnp.sum(w, axis=0, keepdims=True) * ROUTED_SCALE


def _router(h, router_w_t, router_bias):
    tm = 256
    return pl.pallas_call(
        _router_kernel,
        grid=(T_ALL // tm,),
        in_specs=[pl.BlockSpec((tm, D_MODEL), lambda i: (i, 0)),
                  pl.BlockSpec((N_EXPERTS, D_MODEL), lambda i: (0, 0)),
                  pl.BlockSpec((N_EXPERTS, 1), lambda i: (0, 0))],
        out_specs=[pl.BlockSpec((TOP_K, tm), lambda i: (0, i)), pl.BlockSpec((TOP_K, tm), lambda i: (0, i))],
        out_shape=[jax.ShapeDtypeStruct((TOP_K, T_ALL), jnp.int32), jax.ShapeDtypeStruct((TOP_K, T_ALL), F32)],
        compiler_params=_cparams(1), name="router",
    )(h, router_w_t, router_bias.reshape(N_EXPERTS, 1))


def _dispatch(idx):
    n_assign = T_ALL * TOP_K
    flat_e = idx.reshape(-1)
    experts = jnp.arange(N_EXPERTS, dtype=jnp.int32)
    sorted_e, order = lax.sort((flat_e, jnp.arange(n_assign, dtype=jnp.int32)), num_keys=1, is_stable=True)
    counts = jnp.sum((flat_e[:, None] == experts[None, :]).astype(jnp.int32), axis=0)
    cnt_end = jnp.cumsum(counts)
    cnt_start = cnt_end - counts
    padded = (counts + MOE_MB - 1) // MOE_MB * MOE_MB
    pad_end = jnp.cumsum(padded)
    pad_start = pad_end - padded
    onehot = (sorted_e[:, None] == experts[None, :]).astype(jnp.int32)
    slot_sorted = jnp.arange(n_assign, dtype=jnp.int32) + jnp.sum(onehot * (pad_start - cnt_start)[None, :], axis=1)
    _, slot_of = lax.sort((order, slot_sorted), num_keys=1)
    blk_first = jnp.arange(MOE_NB, dtype=jnp.int32) * MOE_MB
    block_e = jnp.minimum(jnp.sum((pad_end[None, :] <= blk_first[:, None]).astype(jnp.int32), axis=1),
                          N_EXPERTS - 1)
    blk_onehot = (block_e[:, None] == experts[None, :]).astype(jnp.int32)
    pick = lambda v: jnp.sum(blk_onehot * v[None, :], axis=1)
    src = jnp.clip(pick(cnt_start) + blk_first - pick(pad_start), 0, n_assign)
    sorted_tok = jnp.pad(order // TOP_K, (0, MOE_MB))
    n_used = (pad_end[-1] // MOE_MB).astype(jnp.int32).reshape(1)
    return sorted_tok, slot_of.reshape(T_ALL, TOP_K), block_e.astype(jnp.int32), src.astype(jnp.int32), n_used


def _expert_kernel(be_ref, nu_ref, src_ref, stok_ref, x_hbm, wg_ref, wu_ref, wd_ref, o_ref,
                   xbuf, x_bf, wg_bf, wu_bf, wd_bf, sems):
    i = pl.program_id(0)
    n_used = nu_ref[0]
    last = MOE_NB - 1

    def row_copy(blk, slot, j):
        tok = stok_ref[src_ref[blk] + j]
        return pltpu.make_async_copy(x_hbm.at[pl.ds(tok, 1), :], xbuf.at[slot, pl.ds(j, 1), :], sems.at[slot])

    def wait_block(slot):
        pltpu.make_async_copy(x_hbm.at[pl.ds(0, MOE_MB), :], xbuf.at[slot], sems.at[slot]).wait()

    @pl.when(i == 0)
    def _():
        def body(j, carry):
            row_copy(0, 0, j).start()
            return carry
        lax.fori_loop(0, MOE_MB, body, 0, unroll=8)

    def compute(prefetch):
        @pl.when((i == 0) | (be_ref[i] != be_ref[jnp.maximum(i - 1, 0)]))
        def _():
            wg_bf[...] = wg_ref[...].astype(BF16)
            wu_bf[...] = wu_ref[...].astype(BF16)
            wd_bf[...] = wd_ref[...].astype(BF16)

        slot = i % 2
        wait_block(slot)
        x_bf[...] = xbuf[slot].astype(BF16)
        if prefetch:
            for j in range(MOE_MB):
                row_copy(i + 1, 1 - slot, j).start()
        x = x_bf[...]
        g = jnp.dot(x, wg_bf[...], preferred_element_type=F32)
        u = jnp.dot(x, wu_bf[...], preferred_element_type=F32)
        a = (_silu(g) * u).astype(BF16)
        o_ref[...] = jnp.dot(a, wd_bf[...], preferred_element_type=F32)

    @pl.when((i < n_used) & (i < last))
    def _():
        compute(True)

    @pl.when((i < n_used) & (i == last))
    def _():
        compute(False)

    @pl.when(i >= n_used)
    def _():
        o_ref[...] = jnp.zeros_like(o_ref)

        @pl.when(i == n_used)
        def _():
            wait_block(i % 2)


def _experts(x, sorted_tok, block_e, src, n_used, wg, wu, wd, layer):
    w_idx = lambda i, be, nu, src, stok: (layer, be[i], 0, 0)
    w_up = pl.BlockSpec((None, None, D_MODEL, F_EXPERT), w_idx)
    return pl.pallas_call(
        _expert_kernel,
        grid_spec=pltpu.PrefetchScalarGridSpec(
            num_scalar_prefetch=4,
            grid=(MOE_NB,),
            in_specs=[pl.BlockSpec(memory_space=pl.ANY), w_up, w_up,
                      pl.BlockSpec((None, None, F_EXPERT, D_MODEL), w_idx)],
            out_specs=pl.BlockSpec((MOE_MB, D_MODEL), lambda i, be, nu, src, stok: (i, 0)),
            scratch_shapes=[pltpu.VMEM((2, MOE_MB, D_MODEL), F32), pltpu.VMEM((MOE_MB, D_MODEL), BF16),
                            pltpu.VMEM((D_MODEL, F_EXPERT), BF16), pltpu.VMEM((D_MODEL, F_EXPERT), BF16),
                            pltpu.VMEM((F_EXPERT, D_MODEL), BF16), pltpu.SemaphoreType.DMA((2,))]),
        out_shape=jax.ShapeDtypeStruct((MOE_SLOTS, D_MODEL), F32),
        compiler_params=_cparams(1), name="moe_experts",
    )(block_e, n_used, src, sorted_tok, x, wg, wu, wd)


def _shared_up_kernel(x_ref, wg_ref, wu_ref, o_ref, wg_bf, wu_bf):
    @pl.when(pl.program_id(0) == 0)
    def _():
        wg_bf[...] = wg_ref[...].astype(BF16)
        wu_bf[...] = wu_ref[...].astype(BF16)

    x = x_ref[...].astype(BF16)
    g = jnp.dot(x, wg_bf[...], preferred_element_type=F32)
    u = jnp.dot(x, wu_bf[...], preferred_element_type=F32)
    o_ref[...] = (_silu(g) * u).astype(o_ref.dtype)


def _shared_up(h, wg, wu, layer):
    w = pl.BlockSpec((None, D_MODEL, F_SHARED), lambda i: (layer, 0, 0))
    return pl.pallas_call(
        _shared_up_kernel,
        grid=(T_ALL // TM,),
        in_specs=[pl.BlockSpec((TM, D_MODEL), lambda i: (i, 0)), w, w],
        out_specs=pl.BlockSpec((TM, F_SHARED), lambda i: (i, 0)),
        out_shape=jax.ShapeDtypeStruct((T_ALL, F_SHARED), BF16),
        scratch_shapes=[pltpu.VMEM((D_MODEL, F_SHARED), BF16), pltpu.VMEM((D_MODEL, F_SHARED), BF16)],
        compiler_params=_cparams(1), name="shared_up",
    )(h, wg, wu)


def _combine_kernel(slot_ref, nxt_ref, y_hbm, rw_ref, act_ref, wd_ref, x_ref, mod_ref, o_ref, buf, wd_bf, sems, *,
                    gate_row):
    i = pl.program_id(0)
    n_rows = TOP_K * CMB_TM

    def row_copy(idx_ref, slot, j):
        return pltpu.make_async_copy(y_hbm.at[pl.ds(idx_ref[0, 0, j], 1), :], buf.at[slot, pl.ds(j, 1), :],
                                     sems.at[slot])

    @pl.when(i == 0)
    def _():
        wd_bf[...] = wd_ref[...].astype(BF16)

        def body(j, carry):
            row_copy(slot_ref, 0, j).start()
            return carry
        lax.fori_loop(0, n_rows, body, 0, unroll=8)

    def tile(prefetch):
        slot = i % 2
        pltpu.make_async_copy(y_hbm.at[pl.ds(0, n_rows), :], buf.at[slot], sems.at[slot]).wait()
        if prefetch:
            for j in range(n_rows):
                row_copy(nxt_ref, 1 - slot, j).start()
        y = jnp.dot(act_ref[...], wd_bf[...], preferred_element_type=F32)
        for k in range(TOP_K):
            y = y + rw_ref[:, k:k + 1] * buf[slot, k * CMB_TM:(k + 1) * CMB_TM, :]
        o_ref[...] = x_ref[...] + mod_ref[gate_row:gate_row + 1, :] * y

    @pl.when(i + 1 < pl.num_programs(0))
    def _():
        tile(True)

    @pl.when(i + 1 == pl.num_programs(0))
    def _():
        tile(False)


def _combine(y_sorted, slot_of, route_w, act, sh_wd, x, mod, layer):
    nt = T_ALL // CMB_TM
    slots = slot_of.reshape(nt, CMB_TM, TOP_K).transpose(0, 2, 1).reshape(nt, 1, TOP_K * CMB_TM)
    slot_spec = lambda nxt: pl.BlockSpec((1, 1, TOP_K * CMB_TM), lambda i: (jnp.minimum(i + nxt, nt - 1), 0, 0),
                                         memory_space=pltpu.SMEM)
    return pl.pallas_call(
        functools.partial(_combine_kernel, gate_row=5),
        grid=(nt,),
        in_specs=[slot_spec(0), slot_spec(1),
                  pl.BlockSpec(memory_space=pl.ANY),
                  pl.BlockSpec((CMB_TM, TOP_K), lambda i: (i, 0)),
                  pl.BlockSpec((CMB_TM, F_SHARED), lambda i: (i, 0)),
                  pl.BlockSpec((None, F_SHARED, D_MODEL), lambda i: (layer, 0, 0)),
                  pl.BlockSpec((CMB_TM, D_MODEL), lambda i: (i, 0)),
                  pl.BlockSpec((None, 6, D_MODEL), lambda i: (_group_of_tile(i, CMB_TM), 0, 0))],
        out_specs=pl.BlockSpec((CMB_TM, D_MODEL), lambda i: (i, 0)),
        out_shape=jax.ShapeDtypeStruct((T_ALL, D_MODEL), F32),
        scratch_shapes=[pltpu.VMEM((2, TOP_K * CMB_TM, D_MODEL), F32), pltpu.VMEM((F_SHARED, D_MODEL), BF16),
                        pltpu.SemaphoreType.DMA((2,))],
        compiler_params=_cparams(1), name="moe_combine",
    )(slots, slots, y_sorted, route_w, act, sh_wd, x, mod)


def kernel(x_prompt, x_sample, cache_na_k, cache_na_v, state_gla, state_s5, c, c_ctx, ada_w, ada_b, norm1_g, norm2_g, w_in, na_rpb, gla_wa, gla_ba, gla_norm_g, s5_a_re, s5_a_im, s5_log_dt, s5_b_re, s5_b_im, s5_c_re, s5_c_im, s5_d, s5_w_glu, w_br_na, w_br_gla, w_br_s5, w_merge, w_out, router_w, router_bias, exp_wg, exp_wu, exp_wd, sh_wg, sh_wu, sh_wd, final_norm_g):
    x = jnp.concatenate([x_prompt.reshape(T_CTX, D_MODEL), x_sample.reshape(T_LAT, D_MODEL)], axis=0)
    cvec = jnp.concatenate([c_ctx[None, :], c, jnp.zeros((MOD_ROWS - N_MOD, D_MODEL), F32)], axis=0)
    mods = _ada(cvec, ada_w, ada_b).reshape(DEPTH, MOD_ROWS, 6, D_MODEL)
    cos, sin = _rope_tables()
    zero_gla = jnp.zeros((BATCH, 2, GLA_HEADS, GLA_DV, GLA_DK), F32)
    zero_s5 = jnp.zeros((S5_GROUPS, 4, BATCH, S5_N), F32)
    new_k, new_v, new_gla, new_s5 = [], [], [], []
    for l in range(DEPTH):
        mod = mods[l]
        h = _norm(x, norm1_g[l], mod, (0, 1), BF16)
        u = _mm(h, w_in, l, 1024, n=IN_MAIN)
        lr = _mm(h, w_in[l, :, IN_MAIN:IN_MAIN + 2 * GLA_LR], None, 2 * GLA_LR)
        su = _mm(h, w_in[l, :, IN_MAIN + 2 * GLA_LR:], None, S5_W)
        gates = _mm(h, w_merge, l, 1024, act="sigmoid")

        o_na = jnp.concatenate([_ctx_attn(u),
                                _na_attn(u, cache_na_k, cache_na_v, _na_bias(na_rpb[l]), l)], axis=0)

        s0_lat = state_gla[:, l].transpose(0, 1, 2, 4, 3)
        og_ctx, sfin = _gla(u, lr, gla_wa, gla_ba, gla_norm_g, zero_gla, cos, sin, l, latent=False)
        og_lat, _ = _gla(u, lr, gla_wa, gla_ba, gla_norm_g, s0_lat, cos, sin, l, latent=True)
        o_gla = jnp.concatenate([og_ctx, og_lat], axis=0)

        ops, step = _s5_operators(s5_a_re[l], s5_a_im[l], s5_log_dt[l], s5_b_re[l], s5_b_im[l],
                                  s5_c_re[l], s5_c_im[l])
        st = state_s5[:, l].astype(F32)
        h0_lat = st.transpose(2, 1, 4, 0, 3).reshape(S5_GROUPS, 4, DEC_BATCH, S5_N)
        y_ctx, hfin = _s5(su[:T_CTX], ops, step, zero_s5, nb=BATCH, seq=SEQ)
        y_lat, _ = _s5(su[T_CTX:], ops, step, h0_lat, nb=DEC_BATCH, seq=DEC_SEQ)
        zz = _glu(jnp.concatenate([y_ctx, y_lat], axis=0), su, s5_d, s5_w_glu, l)

        merged = _merge(o_na, o_gla, zz, gates, w_br_na, w_br_gla, w_br_s5, l)
        x = _out_proj(merged, w_out, x, mod, l)

        h2 = _norm(x, norm2_g[l], mod, (3, 4), F32)
        idx_t, w_t = _router(h2, router_w[l].T, router_bias[l])
        sorted_tok, slot_of, block_e, src, n_used = _dispatch(idx_t.T)
        y_sorted = _experts(h2, sorted_tok, block_e, src, n_used, exp_wg, exp_wu, exp_wd, l)
        act = _shared_up(h2, sh_wg, sh_wu, l)
        x = _combine(y_sorted, slot_of, w_t.T, act, sh_wd, x, mod, l)

        new_k.append(u[:T_CTX, NA_W:2 * NA_W].reshape(BATCH, SEQ, NA_HEADS, NA_DH))
        new_v.append(u[:T_CTX, 2 * NA_W:3 * NA_W].reshape(BATCH, SEQ, NA_HEADS, NA_DH))
        new_gla.append(sfin.transpose(0, 1, 2, 4, 3))
        new_s5.append(hfin.reshape(S5_GROUPS, 2, 2, BATCH, S5_N).transpose(3, 1, 0, 4, 2))

    y = _norm(x, final_norm_g, None, None, F32)
    return (y[:T_CTX].reshape(BATCH, SEQ, D_MODEL), y[T_CTX:].reshape(DEC_BATCH, DEC_SEQ, D_MODEL),
            jnp.stack(new_k, axis=1), jnp.stack(new_v, axis=1),
            jnp.stack(new_gla, axis=1), jnp.stack(new_s5, axis=1))
```

```python
import functools
import math

import jax
import jax.numpy as jnp
from jax import lax
from jax.experimental import pallas as pl
from jax.experimental.pallas import tpu as pltpu

F32 = jnp.float32
BF16 = jnp.bfloat16

D_MODEL = 2048
BATCH = 16
SEQ = 256
DEPTH = 2
DEC_BATCH = 2
DEC_SEQ = 1024
PAST_LEN = 512
GRID_W = 64
NA_HEADS = 8
NA_DH = 128
NA_W = NA_HEADS * NA_DH
NA_ROWS = 8
NA_COLS = 16
GLA_HEADS = 4
GLA_DK = 128
GLA_DV = 256
GLA_QK_W = GLA_HEADS * GLA_DK
GLA_V_W = GLA_HEADS * GLA_DV
GLA_LR = 16
GLA_TAU = 16.0
GLA_CHUNK = 64
GLA_SUB = 16
GLA_UNROLL = 4
ROPE_BASE = 10000.0
S5_W = 1024
S5_GS = 16
S5_GROUPS = S5_W // S5_GS
S5_N = 64
S5_T = 16
IN_W = 3 * NA_W + 2 * GLA_QK_W + 2 * GLA_V_W + 2 * GLA_LR + S5_W
IN_MAIN = 3 * NA_W + 2 * GLA_QK_W + 2 * GLA_V_W
N_EXPERTS = 64
TOP_K = 8
N_GROUPS = 8
TOPK_GROUPS = 4
F_EXPERT = 512
F_SHARED = 512
ROUTED_SCALE = 2.5
EPS = 1e-6

T_CTX = BATCH * SEQ
T_LAT = DEC_BATCH * DEC_SEQ
T_ALL = T_CTX + T_LAT
N_MOD = 1 + DEC_BATCH
MOD_ROWS = 8

TM = 512
MOE_MB = 256
MOE_NB = T_ALL * TOP_K // MOE_MB + N_EXPERTS
MOE_SLOTS = MOE_NB * MOE_MB
CMB_TM = 128
NEG = -1e30

VMEM_LIMIT = 56 * 1024 * 1024

NT = (((1,), (1,)), ((), ()))
TN = (((0,), (0,)), ((), ()))


def _cparams(n_axes):
    return pltpu.CompilerParams(dimension_semantics=("arbitrary",) * n_axes,
                                vmem_limit_bytes=VMEM_LIMIT)


def _group_of_tile(i, tm):
    row = i * tm
    return jnp.where(row < T_CTX, 0, 1 + (row - T_CTX) // DEC_SEQ)


def _silu(x):
    return x * jax.nn.sigmoid(x)


def _gelu_tanh(x):
    return 0.5 * x * (1.0 + jnp.tanh(math.sqrt(2.0 / math.pi) * (x + 0.044715 * (x * x * x))))


def _ada_kernel(c_ref, w_ref, b_ref, o_ref):
    s = _silu(c_ref[...]).astype(BF16)
    o_ref[...] = jnp.dot(s, w_ref[...].astype(BF16), preferred_element_type=F32) + b_ref[...]


def _ada(cvec, ada_w, ada_b):
    tn = 1024
    return pl.pallas_call(
        _ada_kernel,
        grid=(DEPTH, 6 * D_MODEL // tn),
        in_specs=[pl.BlockSpec((MOD_ROWS, D_MODEL), lambda l, j: (0, 0)),
                  pl.BlockSpec((None, D_MODEL, tn), lambda l, j: (l, 0, j)),
                  pl.BlockSpec((None, 1, tn), lambda l, j: (l, 0, j))],
        out_specs=pl.BlockSpec((None, MOD_ROWS, tn), lambda l, j: (l, 0, j)),
        out_shape=jax.ShapeDtypeStruct((DEPTH, MOD_ROWS, 6 * D_MODEL), F32),
        compiler_params=_cparams(2), name="ada",
    )(cvec, ada_w, ada_b.reshape(DEPTH, 1, 6 * D_MODEL))


def _norm_kernel(x_ref, g_ref, *rest, rows):
    o_ref = rest[-1]
    x = x_ref[...]
    y = x * lax.rsqrt(jnp.mean(x * x, axis=-1, keepdims=True) + EPS) * g_ref[...]
    if rows is not None:
        mod_ref = rest[0]
        y = y * (1.0 + mod_ref[rows[1]:rows[1] + 1, :]) + mod_ref[rows[0]:rows[0] + 1, :]
    o_ref[...] = y.astype(o_ref.dtype)


def _norm(x, g, mod, rows, out_dtype):
    tm = 256
    in_specs = [pl.BlockSpec((tm, D_MODEL), lambda i: (i, 0)),
                pl.BlockSpec((1, D_MODEL), lambda i: (0, 0))]
    args = [x, g.reshape(1, D_MODEL)]
    if rows is not None:
        in_specs.append(pl.BlockSpec((None, 6, D_MODEL), lambda i: (_group_of_tile(i, tm), 0, 0)))
        args.append(mod)
    return pl.pallas_call(
        functools.partial(_norm_kernel, rows=rows),
        grid=(T_ALL // tm,),
        in_specs=in_specs,
        out_specs=pl.BlockSpec((tm, D_MODEL), lambda i: (i, 0)),
        out_shape=jax.ShapeDtypeStruct((T_ALL, D_MODEL), out_dtype),
        compiler_params=_cparams(1), name="norm",
    )(*args)


def _mm_kernel(x_ref, w_ref, o_ref, wbf_ref, *, act, w_rows_are_outputs):
    @pl.when(pl.program_id(1) == 0)
    def _():
        wbf_ref[...] = w_ref[...].astype(BF16)

    x = x_ref[...].astype(BF16)
    if w_rows_are_outputs:
        acc = lax.dot_general(x, wbf_ref[...], NT, preferred_element_type=F32)
    else:
        acc = jnp.dot(x, wbf_ref[...], preferred_element_type=F32)
    if act == "sigmoid":
        acc = jax.nn.sigmoid(acc)
    o_ref[...] = acc.astype(o_ref.dtype)


def _mm(x, w, layer, tn, act=None, out_dtype=F32, n=None, w_rows_are_outputs=False):
    k = x.shape[1]
    if n is None:
        n = w.shape[-2] if w_rows_are_outputs else w.shape[-1]
    blk, idx = ((tn, k), lambda j: (j, 0)) if w_rows_are_outputs else ((k, tn), lambda j: (0, j))
    if layer is None:
        w_spec = pl.BlockSpec(blk, lambda j, i: idx(j))
    else:
        w_spec = pl.BlockSpec((None,) + blk, lambda j, i: (layer,) + idx(j))
    return pl.pallas_call(
        functools.partial(_mm_kernel, act=act, w_rows_are_outputs=w_rows_are_outputs),
        grid=(n // tn, T_ALL // TM),
        in_specs=[pl.BlockSpec((TM, k), lambda j, i: (i, 0)), w_spec],
        out_specs=pl.BlockSpec((TM, tn), lambda j, i: (i, j)),
        out_shape=jax.ShapeDtypeStruct((T_ALL, n), out_dtype),
        scratch_shapes=[pltpu.VMEM(blk, BF16)],
        compiler_params=_cparams(2), name="mm",
    )(x, w)


def _ctx_attn_kernel(q_ref, k_ref, v_ref, o_ref):
    scale = NA_DH ** -0.5
    for h in range(NA_HEADS):
        sl = slice(h * NA_DH, (h + 1) * NA_DH)
        q = q_ref[:, sl].astype(BF16)
        k = k_ref[:, sl].astype(BF16)
        v = v_ref[:, sl].astype(BF16)
        s = lax.dot_general(q, k, NT, preferred_element_type=F32) * scale
        p = jnp.exp(s - jnp.max(s, axis=-1, keepdims=True))
        o = jnp.dot(p.astype(BF16), v, preferred_element_type=F32) / jnp.sum(p, axis=-1, keepdims=True)
        o_ref[:, sl] = o.astype(o_ref.dtype)


def _ctx_attn(u):
    spec = lambda cb: pl.BlockSpec((SEQ, NA_W), lambda b: (b, cb))
    return pl.pallas_call(
        _ctx_attn_kernel,
        grid=(BATCH,),
        in_specs=[spec(0), spec(1), spec(2)],
        out_specs=pl.BlockSpec((SEQ, NA_W), lambda b: (b, 0)),
        out_shape=jax.ShapeDtypeStruct((T_CTX, NA_W), BF16),
        compiler_params=_cparams(1), name="ctx_attn",
    )(u, u, u)


NA_GRID_ROWS = DEC_SEQ // GRID_W
NA_KR = min(NA_ROWS, NA_GRID_ROWS)
NA_LOC = NA_KR * GRID_W


def _na_bias(rpb):
    n_off = 2 * NA_ROWS - 1
    pad = GRID_W - NA_COLS
    period = 2 * GRID_W
    p = jnp.pad(rpb, ((0, 0), (0, 0), (pad, period - pad - (2 * NA_COLS - 1))))
    hank = jnp.tile(p, (1, 1, GRID_W + 1))[..., :GRID_W * (period + 1)]
    hank = hank.reshape(NA_HEADS, n_off, GRID_W, period + 1)[..., :GRID_W]
    band = hank[:, :, ::-1, :]
    qc = jnp.arange(GRID_W)[:, None]
    kc = jnp.arange(GRID_W)[None, :]
    win = jnp.clip(qc - NA_COLS // 2, 0, GRID_W - NA_COLS)
    valid = (kc >= win) & (kc < win + NA_COLS)
    band = jnp.where(valid[None, None], band, NEG)
    return band.transpose(0, 2, 1, 3).reshape(NA_HEADS, GRID_W, n_off * GRID_W)


def _na_kernel(q_ref, k_ref, v_ref, ck_ref, cv_ref, bias_ref, o_ref):
    scale = NA_DH ** -0.5
    kb = k_ref[...].astype(BF16)
    vb = v_ref[...].astype(BF16)
    ck = ck_ref[...].astype(BF16)
    cv = cv_ref[...].astype(BF16)
    for r in range(NA_GRID_ROWS):
        first = min(max(r - NA_KR // 2, 0), NA_GRID_ROWS - NA_KR)
        off = first - r + NA_ROWS - 1
        rows = slice(r * GRID_W, (r + 1) * GRID_W)
        keys = slice(first * GRID_W, first * GRID_W + NA_LOC)
        q = q_ref[rows, :].astype(BF16)
        s1 = (lax.dot_general(q, kb[keys, :], NT, preferred_element_type=F32) * scale
              + bias_ref[:, off * GRID_W:off * GRID_W + NA_LOC])
        s2 = lax.dot_general(q, ck, NT, preferred_element_type=F32) * scale
        m = jnp.maximum(jnp.max(s1, axis=-1, keepdims=True), jnp.max(s2, axis=-1, keepdims=True))
        p1 = jnp.exp(s1 - m)
        p2 = jnp.exp(s2 - m)
        den = jnp.sum(p1, axis=-1, keepdims=True) + jnp.sum(p2, axis=-1, keepdims=True)
        o = (jnp.dot(p1.astype(BF16), vb[keys, :], preferred_element_type=F32)
             + jnp.dot(p2.astype(BF16), cv, preferred_element_type=F32))
        o_ref[rows, :] = (o / den).astype(o_ref.dtype)


def _na_attn(u, cache_k, cache_v, bias, layer):
    lat_sb = T_CTX // DEC_SEQ
    ck = cache_k.reshape(DEC_BATCH, DEPTH, PAST_LEN, NA_W)
    cv = cache_v.reshape(DEC_BATCH, DEPTH, PAST_LEN, NA_W)
    qkv_spec = lambda cb: pl.BlockSpec((DEC_SEQ, NA_DH), lambda b, h: (lat_sb + b, cb * NA_HEADS + h))
    c_spec = pl.BlockSpec((None, None, PAST_LEN, NA_DH), lambda b, h: (b, layer, 0, h))
    return pl.pallas_call(
        _na_kernel,
        grid=(DEC_BATCH, NA_HEADS),
        in_specs=[qkv_spec(0), qkv_spec(1), qkv_spec(2), c_spec, c_spec,
                  pl.BlockSpec((None, GRID_W, (2 * NA_ROWS - 1) * GRID_W), lambda b, h: (h, 0, 0))],
        out_specs=pl.BlockSpec((DEC_SEQ, NA_DH), lambda b, h: (b, h)),
        out_shape=jax.ShapeDtypeStruct((T_LAT, NA_W), BF16),
        compiler_params=_cparams(2), name="na_attn",
    )(u, u, u, ck, cv, bias)


def _rope_tables():
    half = GLA_DK // 2
    nf = half // 2
    t = jnp.arange(DEC_SEQ)
    freqs = ROPE_BASE ** (-jnp.arange(nf, dtype=F32) / nf)
    ang_r = (t // GRID_W).astype(F32)[:, None] * freqs
    ang_c = (t % GRID_W).astype(F32)[:, None] * freqs
    cos = jnp.concatenate([jnp.cos(ang_r), jnp.cos(ang_r), jnp.cos(ang_c), jnp.cos(ang_c)], axis=-1)
    sin = jnp.concatenate([-jnp.sin(ang_r), jnp.sin(ang_r), -jnp.sin(ang_c), jnp.sin(ang_c)], axis=-1)
    return cos, sin


def _split_bf16(x):
    hi = x.astype(BF16)
    return hi, (x - hi.astype(F32)).astype(BF16)


def _gla_kernel(q_ref, k_ref, v_ref, gg_ref, lr_ref, wa_ref, ba_ref, ng_ref, cos_ref, sin_ref, s0_ref,
                o_ref, sfin_ref, qs, ks, las, o_acc, st, *, seq, rope):
    nc = seq // GLA_CHUNK
    nq = GLA_DK // 4

    def rot(x):
        lane = lax.broadcasted_iota(jnp.int32, x.shape, 1)
        partner = jnp.where((lane % (2 * nq)) < nq,
                            pltpu.roll(x, GLA_DK - nq, axis=1), pltpu.roll(x, nq, axis=1))
        return x * cos_ref[...] + partner * sin_ref[...]

    q = q_ref[...] * (GLA_DK ** -0.5)
    k = k_ref[...]
    if rope:
        q = rot(q)
        k = rot(k)
    qs[...] = q
    ks[...] = k
    lr = lr_ref[...]
    for d in range(2):
        z = jnp.dot(lr[:, d * GLA_LR:(d + 1) * GLA_LR].astype(BF16), wa_ref[d].astype(BF16),
                    preferred_element_type=F32) + ba_ref[d:d + 1, :]
        las[d] = -(jnp.maximum(-z, 0.0) + jnp.log1p(jnp.exp(-jnp.abs(z)))) / GLA_TAU
        st[d] = s0_ref[d]

    row = lax.broadcasted_iota(jnp.int32, (GLA_CHUNK, GLA_CHUNK), 0)
    col = lax.broadcasted_iota(jnp.int32, (GLA_CHUNK, GLA_CHUNK), 1)
    key_row = lax.broadcasted_iota(jnp.int32, (GLA_CHUNK, 1), 0)

    def chunk(c, d):
        rev = d == 1
        rows = pl.ds(pl.multiple_of(c * GLA_CHUNK, GLA_CHUNK), GLA_CHUNK)
        qc = qs[rows, :]
        kc = ks[rows, :]
        vc = v_ref[rows, :].astype(BF16)
        la = las[d, rows, :]
        causal = (col >= row) if rev else (col <= row)
        tri = jnp.where(causal, 1.0, 0.0).astype(BF16)
        la_hi, la_lo = _split_bf16(la)
        b = (jnp.dot(tri, la_hi, preferred_element_type=F32)
             + jnp.dot(tri, la_lo, preferred_element_type=F32))
        bex = b - la
        b_last = b[0:1, :] if rev else b[GLA_CHUNK - 1:GLA_CHUNK, :]
        blocks = []
        for i in range(GLA_CHUNK // GLA_SUB):
            lo, hi = i * GLA_SUB, (i + 1) * GLA_SUB
            ref = bex[hi - 1:hi, :] if rev else bex[lo:lo + 1, :]
            qt = (qc[lo:hi, :] * jnp.exp(b[lo:hi, :] - ref)).astype(BF16)
            allowed = (key_row >= lo) if rev else (key_row < hi)
            kt = (kc * jnp.exp(jnp.where(allowed, ref - b, -jnp.inf))).astype(BF16)
            blocks.append(lax.dot_general(qt, kt, NT, preferred_element_type=F32))
        att = jnp.where(causal, jnp.concatenate(blocks, axis=0), 0.0)
        s_t = st[d]
        o = (jnp.dot(att.astype(BF16), vc, preferred_element_type=F32)
             + lax.dot_general((qc * jnp.exp(b)).astype(BF16), s_t.astype(BF16), NT,
                               preferred_element_type=F32))
        khat = (kc * jnp.exp(b_last - b)).astype(BF16)
        st[d] = s_t * jnp.exp(b_last) + lax.dot_general(vc, khat, TN, preferred_element_type=F32)
        o_acc[d, rows, :] = o

    def both(c, carry):
        chunk(c, 0)
        chunk(nc - 1 - c, 1)
        return carry

    lax.fori_loop(0, nc, both, 0, unroll=GLA_UNROLL)
    sfin_ref[...] = st[...]
    o = o_acc[0] + o_acc[1]
    o = o * lax.rsqrt(jnp.mean(o * o, axis=-1, keepdims=True) + EPS) * ng_ref[...]
    o_ref[...] = (o * _silu(gg_ref[...])).astype(o_ref.dtype)


def _gla(u, lr, wa, ba, norm_g, s0_t, cos, sin, layer, *, latent):
    seq, nb, first = (DEC_SEQ, DEC_BATCH, T_CTX // DEC_SEQ) if latent else (SEQ, BATCH, 0)
    qk_cb = 3 * NA_W // GLA_DK
    v_cb = (3 * NA_W + 2 * GLA_QK_W) // GLA_DV
    row = lambda w, cb: pl.BlockSpec((seq, w), lambda b, h: (first + b, cb + h))
    return pl.pallas_call(
        functools.partial(_gla_kernel, seq=seq, rope=latent),
        grid=(nb, GLA_HEADS),
        in_specs=[row(GLA_DK, qk_cb), row(GLA_DK, qk_cb + GLA_HEADS), row(GLA_DV, v_cb),
                  row(GLA_DV, v_cb + GLA_HEADS),
                  pl.BlockSpec((seq, 2 * GLA_LR), lambda b, h: (first + b, 0)),
                  pl.BlockSpec((None, 2, GLA_LR, GLA_DK), lambda b, h: (layer, 0, 0, h)),
                  pl.BlockSpec((None, 2, GLA_DK), lambda b, h: (layer, 0, h)),
                  pl.BlockSpec((None, 1, GLA_DV), lambda b, h: (layer, 0, h)),
                  pl.BlockSpec((seq, GLA_DK), lambda b, h: (0, 0)),
                  pl.BlockSpec((seq, GLA_DK), lambda b, h: (0, 0)),
                  pl.BlockSpec((None, 2, None, GLA_DV, GLA_DK), lambda b, h: (b, 0, h, 0, 0))],
        out_specs=[pl.BlockSpec((seq, GLA_DV), lambda b, h: (b, h)),
                   pl.BlockSpec((None, 2, None, GLA_DV, GLA_DK), lambda b, h: (b, 0, h, 0, 0))],
        out_shape=[jax.ShapeDtypeStruct((nb * seq, GLA_V_W), BF16),
                   jax.ShapeDtypeStruct((nb, 2, GLA_HEADS, GLA_DV, GLA_DK), F32)],
        scratch_shapes=[pltpu.VMEM((seq, GLA_DK), F32), pltpu.VMEM((seq, GLA_DK), F32),
                        pltpu.VMEM((2, seq, GLA_DK), F32), pltpu.VMEM((2, seq, GLA_DV), F32),
                        pltpu.VMEM((2, GLA_DV, GLA_DK), F32)],
        compiler_params=_cparams(2), name="gla",
    )(u, u, u, u, lr, wa, ba, norm_g.reshape(DEPTH, 1, GLA_V_W), cos[:seq], sin[:seq], s0_t)


def _s5_operators(a_re, a_im, log_dt, b_re, b_im, c_re, c_im):
    hp = lax.Precision.HIGHEST
    lam = lax.complex(a_re.astype(F32), a_im.astype(F32))
    lam_dt = lam * jnp.exp(log_dt.astype(F32))[..., None]
    a_bar = jnp.exp(lam_dt)
    b_bar = ((a_bar - 1.0) / lam)[..., None] * lax.complex(b_re.astype(F32), b_im.astype(F32))
    cc = lax.complex(c_re.astype(F32), c_im.astype(F32))
    taus = jnp.arange(S5_T + 1, dtype=F32)
    pw = jnp.exp(lam_dt[:, :, None, :] * taus[None, None, :, None])
    lanes = S5_T * S5_GS

    def out_weights(pwd, cd):
        m = pwd.transpose(0, 2, 1)[:, :, :, None] * cd.transpose(0, 2, 1)[:, :, None, :]
        return m.reshape(S5_GROUPS, S5_N, lanes)

    def lag_response(d):
        r = out_weights(pw[d, :, :S5_T], cc[d])
        bt = b_bar[d].transpose(0, 2, 1)
        return (jnp.einsum("gin,gnx->gix", jnp.real(bt), jnp.real(r), precision=hp)
                - jnp.einsum("gin,gnx->gix", jnp.imag(bt), jnp.imag(r), precision=hp))

    def toeplitz(lags):
        rows = [lags[:, :, (S5_T - 1 - s) * S5_GS:(2 * S5_T - 1 - s) * S5_GS] for s in range(S5_T)]
        return jnp.stack(rows, axis=1).reshape(S5_GROUPS, lanes, lanes)

    no_lag = jnp.zeros((S5_GROUPS, S5_GS, (S5_T - 1) * S5_GS), F32)
    rev_lags = lag_response(1).reshape(S5_GROUPS, S5_GS, S5_T, S5_GS)[:, :, ::-1].reshape(S5_GROUPS, S5_GS, lanes)
    toep_f = toeplitz(jnp.concatenate([no_lag, lag_response(0)], axis=-1))
    toep_b = toeplitz(jnp.concatenate([rev_lags, no_lag], axis=-1))

    def state_in(pwd, bd):
        m = pwd[:, :, None, :] * bd.transpose(0, 2, 1)[:, None, :, :]
        m = m.reshape(S5_GROUPS, lanes, S5_N)
        return [jnp.real(m), jnp.imag(m)]

    p = jnp.stack(state_in(pw[0, :, S5_T - 1::-1], b_bar[0])
                  + state_in(pw[1, :, :S5_T], b_bar[1]), axis=1)

    def state_out(pwd, cd):
        m = out_weights(pwd, cd)
        return [jnp.real(m), -jnp.imag(m)]

    q = jnp.stack(state_out(pw[0, :, 1:S5_T + 1], cc[0])
                  + state_out(pw[1, :, S5_T:0:-1], cc[1]), axis=1)
    ops = [m.astype(BF16) for m in (toep_f, toep_b, p, q)]
    a_t = pw[:, :, S5_T]
    step = jnp.stack([jnp.real(a_t[0]), jnp.imag(a_t[0]), jnp.real(a_t[1]), jnp.imag(a_t[1])], axis=1)
    return ops, step


def _s5_kernel(u_ref, tf_ref, tb_ref, p_ref, q_ref, step_ref, h0_ref, y_ref, hfin_ref, e_scr, hin_scr, *, nc, nb):
    u = u_ref[...]
    for m in range(4):
        e_scr[m] = jnp.dot(u, p_ref[m], preferred_element_type=F32)
    ar = [step_ref[0:1, :], step_ref[2:3, :]]
    ai = [step_ref[1:2, :], step_ref[3:4, :]]
    hr = [h0_ref[0], h0_ref[2]]
    hi = [h0_ref[1], h0_ref[3]]
    for c in range(nc):
        for d in range(2):
            rows = slice(c * nb, (c + 1) * nb) if d == 0 else slice((nc - 1 - c) * nb, (nc - c) * nb)
            hin_scr[2 * d, rows, :] = hr[d]
            hin_scr[2 * d + 1, rows, :] = hi[d]
            hr[d], hi[d] = (hr[d] * ar[d] - hi[d] * ai[d] + e_scr[2 * d, rows, :],
                            hi[d] * ar[d] + hr[d] * ai[d] + e_scr[2 * d + 1, rows, :])
    for d in range(2):
        hfin_ref[2 * d] = hr[d]
        hfin_ref[2 * d + 1] = hi[d]
    y = jnp.dot(u, tf_ref[...], preferred_element_type=F32) + jnp.dot(u, tb_ref[...], preferred_element_type=F32)
    for m in range(4):
        y += jnp.dot(hin_scr[m].astype(BF16), q_ref[m], preferred_element_type=F32)
    y_ref[...] = y


def _s5(su, ops, step, h0, *, nb, seq):
    nc = seq // S5_T
    nbp = max(nb, 8)
    lanes = S5_T * S5_GS
    x = su.reshape(nb, nc, S5_T, S5_GROUPS, S5_GS).transpose(3, 1, 0, 2, 4)
    if nbp != nb:
        x = jnp.pad(x, ((0, 0), (0, 0), (0, nbp - nb), (0, 0), (0, 0)))
        h0 = jnp.pad(h0, ((0, 0), (0, 0), (0, nbp - nb), (0, 0)))
    x = x.reshape(S5_GROUPS, nc * nbp, lanes).astype(BF16)
    rows = nc * nbp
    g3 = lambda a, b: pl.BlockSpec((None, a, b), lambda g: (g, 0, 0))
    g4 = lambda a, b: pl.BlockSpec((None, 4, a, b), lambda g: (g, 0, 0, 0))
    y, hfin = pl.pallas_call(
        functools.partial(_s5_kernel, nc=nc, nb=nbp),
        grid=(S5_GROUPS,),
        in_specs=[g3(rows, lanes), g3(lanes, lanes), g3(lanes, lanes), g4(lanes, S5_N), g4(S5_N, lanes),
                  g3(4, S5_N), g4(nbp, S5_N)],
        out_specs=[g3(rows, lanes), g4(nbp, S5_N)],
        out_shape=[jax.ShapeDtypeStruct((S5_GROUPS, rows, lanes), F32),
                   jax.ShapeDtypeStruct((S5_GROUPS, 4, nbp, S5_N), F32)],
        scratch_shapes=[pltpu.VMEM((4, rows, S5_N), F32), pltpu.VMEM((4, rows, S5_N), F32)],
        compiler_params=_cparams(1), name="s5",
    )(x, *ops, step, h0)
    y = y.reshape(S5_GROUPS, nc, nbp, S5_T, S5_GS)[:, :, :nb].transpose(2, 1, 3, 0, 4)
    return y.reshape(nb * seq, S5_W), hfin


def _glu_kernel(y_ref, su_ref, d_ref, w_ref, o_ref, wbf_ref):
    @pl.when(pl.program_id(0) == 0)
    def _():
        wbf_ref[...] = w_ref[...].astype(BF16)

    z = _gelu_tanh(y_ref[...] + d_ref[...] * su_ref[...])
    gate = jax.nn.sigmoid(jnp.dot(z.astype(BF16), wbf_ref[...], preferred_element_type=F32))
    o_ref[...] = (z * gate).astype(o_ref.dtype)


def _glu(y, su, d, w_glu, layer):
    row = pl.BlockSpec((TM, S5_W), lambda i: (i, 0))
    return pl.pallas_call(
        _glu_kernel,
        grid=(T_ALL // TM,),
        in_specs=[row, row, pl.BlockSpec((None, 1, S5_W), lambda i: (layer, 0, 0)),
                  pl.BlockSpec((None, S5_W, S5_W), lambda i: (layer, 0, 0))],
        out_specs=row,
        out_shape=jax.ShapeDtypeStruct((T_ALL, S5_W), BF16),
        scratch_shapes=[pltpu.VMEM((S5_W, S5_W), BF16)],
        compiler_params=_cparams(1), name="glu",
    )(y, su, d.reshape(DEPTH, 1, S5_W), w_glu)


def _merge_kernel(a_ref, b_ref, c_ref, ga_ref, gb_ref, gc_ref, wa_ref, wb_ref, wc_ref, o_ref, wbf_ref):
    @pl.when(pl.program_id(1) == 0)
    def _():
        wbf_ref[0] = wa_ref[...].astype(BF16)
        wbf_ref[1] = wb_ref[...].astype(BF16)
        wbf_ref[2] = wc_ref[...].astype(BF16)

    acc = ga_ref[...] * jnp.dot(a_ref[...], wbf_ref[0], preferred_element_type=F32)
    acc += gb_ref[...] * jnp.dot(b_ref[...], wbf_ref[1], preferred_element_type=F32)
    acc += gc_ref[...] * jnp.dot(c_ref[...], wbf_ref[2], preferred_element_type=F32)
    o_ref[...] = acc.astype(o_ref.dtype)


def _merge(o_na, o_gla, zz, gates, w_na, w_gla, w_s5, layer):
    tn = 1024
    ncb = D_MODEL // tn
    act = pl.BlockSpec((TM, NA_W), lambda j, i: (i, 0))
    gate = lambda k: pl.BlockSpec((TM, tn), lambda j, i: (i, k * ncb + j))
    w = pl.BlockSpec((None, NA_W, tn), lambda j, i: (layer, 0, j))
    return pl.pallas_call(
        _merge_kernel,
        grid=(ncb, T_ALL // TM),
        in_specs=[act, act, act, gate(0), gate(1), gate(2), w, w, w],
        out_specs=pl.BlockSpec((TM, tn), lambda j, i: (i, j)),
        out_shape=jax.ShapeDtypeStruct((T_ALL, D_MODEL), BF16),
        scratch_shapes=[pltpu.VMEM((3, NA_W, tn), BF16)],
        compiler_params=_cparams(2), name="merge",
    )(o_na, o_gla, zz, gates, gates, gates, w_na, w_gla, w_s5)


def _out_proj_kernel(m_ref, w_ref, x_ref, mod_ref, o_ref, wbf_ref, *, gate_row):
    @pl.when(pl.program_id(1) == 0)
    def _():
        wbf_ref[...] = w_ref[...].astype(BF16)

    y = jnp.dot(m_ref[...], wbf_ref[...], preferred_element_type=F32)
    o_ref[...] = x_ref[...] + mod_ref[gate_row:gate_row + 1, :] * y


def _out_proj(merged, w_out, x, mod, layer):
    tn = 1024
    return pl.pallas_call(
        functools.partial(_out_proj_kernel, gate_row=2),
        grid=(D_MODEL // tn, T_ALL // TM),
        in_specs=[pl.BlockSpec((TM, D_MODEL), lambda j, i: (i, 0)),
                  pl.BlockSpec((None, D_MODEL, tn), lambda j, i: (layer, 0, j)),
                  pl.BlockSpec((TM, tn), lambda j, i: (i, j)),
                  pl.BlockSpec((None, 6, tn), lambda j, i: (_group_of_tile(i, TM), 0, j))],
        out_specs=pl.BlockSpec((TM, tn), lambda j, i: (i, j)),
        out_shape=jax.ShapeDtypeStruct((T_ALL, D_MODEL), F32),
        scratch_shapes=[pltpu.VMEM((D_MODEL, tn), BF16)],
        compiler_params=_cparams(2), name="out_proj",
    )(merged, w_out, x, mod)


def _first_max(v, iota, n):
    m = jnp.max(v, axis=0, keepdims=True)
    first = jnp.min(jnp.where(v == m, iota, float(n)), axis=0, keepdims=True)
    return m, first


def _router_kernel(x_ref, wt_ref, bias_ref, idx_ref, w_ref):
    per = N_EXPERTS // N_GROUPS
    x_hi, x_lo = _split_bf16(x_ref[...])
    w_hi, w_lo = _split_bf16(wt_ref[...])
    logits = (lax.dot_general(w_hi, x_hi, NT, preferred_element_type=F32)
              + lax.dot_general(w_hi, x_lo, NT, preferred_element_type=F32)
              + lax.dot_general(w_lo, x_hi, NT, preferred_element_type=F32))
    scores = jax.nn.sigmoid(logits)
    sel = scores + bias_ref[...]
    tm = sel.shape[1]
    iota_g = lax.broadcasted_iota(jnp.int32, (per, tm), 0).astype(F32)
    grp_rows = []
    for g in range(N_GROUPS):
        v = sel[g * per:(g + 1) * per, :]
        m1, first = _first_max(v, iota_g, per)
        m2 = jnp.max(jnp.where(iota_g == first, -jnp.inf, v), axis=0, keepdims=True)
        grp_rows.append(m1 + m2)
    grp = jnp.concatenate(grp_rows, axis=0)
    iota_n = lax.broadcasted_iota(jnp.int32, (N_GROUPS, tm), 0).astype(F32)
    chosen = jnp.zeros((N_GROUPS, tm), F32)
    for _ in range(TOPK_GROUPS):
        _, first = _first_max(grp, iota_n, N_GROUPS)
        hit = iota_n == first
        chosen = jnp.where(hit, 1.0, chosen)
        grp = jnp.where(hit, -jnp.inf, grp)
    mask = jnp.concatenate([jnp.broadcast_to(chosen[g:g + 1, :], (per, tm)) for g in range(N_GROUPS)], axis=0)
    sel = jnp.where(mask > 0.5, sel, -jnp.inf)
    iota_e = lax.broadcasted_iota(jnp.int32, (N_EXPERTS, tm), 0).astype(F32)
    ids, ws = [], []
    for _ in range(TOP_K):
        _, first = _first_max(sel, iota_e, N_EXPERTS)
        hit = iota_e == first
        ids.append(first)
        ws.append(jnp.sum(jnp.where(hit, scores, 0.0), axis=0, keepdims=True))
        sel = jnp.where(hit, -jnp.inf, sel)
    w = jnp.concatenate(ws, axis=0)
    idx_ref[...] = jnp.concatenate(ids, axis=0).astype(jnp.int32)
    w_ref[...] = w / jnp.sum(w, axis=0, keepdims=True) * ROUTED_SCALE


def _router(h, router_w_t, router_bias):
    tm = 256
    return pl.pallas_call(
        _router_kernel,
        grid=(T_ALL // tm,),
        in_specs=[pl.BlockSpec((tm, D_MODEL), lambda i: (i, 0)),
                  pl.BlockSpec((N_EXPERTS, D_MODEL), lambda i: (0, 0)),
                  pl.BlockSpec((N_EXPERTS, 1), lambda i: (0, 0))],
        out_specs=[pl.BlockSpec((TOP_K, tm), lambda i: (0, i)), pl.BlockSpec((TOP_K, tm), lambda i: (0, i))],
        out_shape=[jax.ShapeDtypeStruct((TOP_K, T_ALL), jnp.int32), jax.ShapeDtypeStruct((TOP_K, T_ALL), F32)],
        compiler_params=_cparams(1), name="router",
    )(h, router_w_t, router_bias.reshape(N_EXPERTS, 1))


def _dispatch(idx):
    n_assign = T_ALL * TOP_K
    flat_e = idx.reshape(-1)
    experts = jnp.arange(N_EXPERTS, dtype=jnp.int32)
    sorted_e, order = lax.sort((flat_e, jnp.arange(n_assign, dtype=jnp.int32)), num_keys=1, is_stable=True)
    counts = jnp.sum((flat_e[:, None] == experts[None, :]).astype(jnp.int32), axis=0)
    cnt_end = jnp.cumsum(counts)
    cnt_start = cnt_end - counts
    padded = (counts + MOE_MB - 1) // MOE_MB * MOE_MB
    pad_end = jnp.cumsum(padded)
    pad_start = pad_end - padded
    onehot = (sorted_e[:, None] == experts[None, :]).astype(jnp.int32)
    slot_sorted = jnp.arange(n_assign, dtype=jnp.int32) + jnp.sum(onehot * (pad_start - cnt_start)[None, :], axis=1)
    _, slot_of = lax.sort((order, slot_sorted), num_keys=1)
    blk_first = jnp.arange(MOE_NB, dtype=jnp.int32) * MOE_MB
    block_e = jnp.minimum(jnp.sum((pad_end[None, :] <= blk_first[:, None]).astype(jnp.int32), axis=1),
                          N_EXPERTS - 1)
    blk_onehot = (block_e[:, None] == experts[None, :]).astype(jnp.int32)
    pick = lambda v: jnp.sum(blk_onehot * v[None, :], axis=1)
    src = jnp.clip(pick(cnt_start) + blk_first - pick(pad_start), 0, n_assign)
    sorted_tok = jnp.pad(order // TOP_K, (0, MOE_MB))
    n_used = (pad_end[-1] // MOE_MB).astype(jnp.int32).reshape(1)
    return sorted_tok, slot_of.reshape(T_ALL, TOP_K), block_e.astype(jnp.int32), src.astype(jnp.int32), n_used


def _expert_kernel(be_ref, nu_ref, src_ref, stok_ref, x_hbm, wg_ref, wu_ref, wd_ref, o_ref,
                   xbuf, x_bf, wg_bf, wu_bf, wd_bf, sems):
    i = pl.program_id(0)
    n_used = nu_ref[0]
    last = MOE_NB - 1

    def row_copy(blk, slot, j):
        tok = stok_ref[src_ref[blk] + j]
        return pltpu.make_async_copy(x_hbm.at[pl.ds(tok, 1), :], xbuf.at[slot, pl.ds(j, 1), :], sems.at[slot])

    def wait_block(slot):
        pltpu.make_async_copy(x_hbm.at[pl.ds(0, MOE_MB), :], xbuf.at[slot], sems.at[slot]).wait()

    @pl.when(i == 0)
    def _():
        def body(j, carry):
            row_copy(0, 0, j).start()
            return carry
        lax.fori_loop(0, MOE_MB, body, 0, unroll=8)

    def compute(prefetch):
        @pl.when((i == 0) | (be_ref[i] != be_ref[jnp.maximum(i - 1, 0)]))
        def _():
            wg_bf[...] = wg_ref[...].astype(BF16)
            wu_bf[...] = wu_ref[...].astype(BF16)
            wd_bf[...] = wd_ref[...].astype(BF16)

        slot = i % 2
        wait_block(slot)
        x_bf[...] = xbuf[slot].astype(BF16)
        if prefetch:
            for j in range(MOE_MB):
                row_copy(i + 1, 1 - slot, j).start()
        x = x_bf[...]
        g = jnp.dot(x, wg_bf[...], preferred_element_type=F32)
        u = jnp.dot(x, wu_bf[...], preferred_element_type=F32)
        a = (_silu(g) * u).astype(BF16)
        o_ref[...] = jnp.dot(a, wd_bf[...], preferred_element_type=F32)

    @pl.when((i < n_used) & (i < last))
    def _():
        compute(True)

    @pl.when((i < n_used) & (i == last))
    def _():
        compute(False)

    @pl.when(i >= n_used)
    def _():
        o_ref[...] = jnp.zeros_like(o_ref)

        @pl.when(i == n_used)
        def _():
            wait_block(i % 2)


def _experts(x, sorted_tok, block_e, src, n_used, wg, wu, wd, layer):
    w_idx = lambda i, be, nu, src, stok: (layer, be[i], 0, 0)
    w_up = pl.BlockSpec((None, None, D_MODEL, F_EXPERT), w_idx)
    return pl.pallas_call(
        _expert_kernel,
        grid_spec=pltpu.PrefetchScalarGridSpec(
            num_scalar_prefetch=4,
            grid=(MOE_NB,),
            in_specs=[pl.BlockSpec(memory_space=pl.ANY), w_up, w_up,
                      pl.BlockSpec((None, None, F_EXPERT, D_MODEL), w_idx)],
            out_specs=pl.BlockSpec((MOE_MB, D_MODEL), lambda i, be, nu, src, stok: (i, 0)),
            scratch_shapes=[pltpu.VMEM((2, MOE_MB, D_MODEL), F32), pltpu.VMEM((MOE_MB, D_MODEL), BF16),
                            pltpu.VMEM((D_MODEL, F_EXPERT), BF16), pltpu.VMEM((D_MODEL, F_EXPERT), BF16),
                            pltpu.VMEM((F_EXPERT, D_MODEL), BF16), pltpu.SemaphoreType.DMA((2,))]),
        out_shape=jax.ShapeDtypeStruct((MOE_SLOTS, D_MODEL), F32),
        compiler_params=_cparams(1), name="moe_experts",
    )(block_e, n_used, src, sorted_tok, x, wg, wu, wd)


def _shared_up_kernel(x_ref, wg_ref, wu_ref, o_ref, wg_bf, wu_bf):
    @pl.when(pl.program_id(0) == 0)
    def _():
        wg_bf[...] = wg_ref[...].astype(BF16)
        wu_bf[...] = wu_ref[...].astype(BF16)

    x = x_ref[...].astype(BF16)
    g = jnp.dot(x, wg_bf[...], preferred_element_type=F32)
    u = jnp.dot(x, wu_bf[...], preferred_element_type=F32)
    o_ref[...] = (_silu(g) * u).astype(o_ref.dtype)


def _shared_up(h, wg, wu, layer):
    w = pl.BlockSpec((None, D_MODEL, F_SHARED), lambda i: (layer, 0, 0))
    return pl.pallas_call(
        _shared_up_kernel,
        grid=(T_ALL // TM,),
        in_specs=[pl.BlockSpec((TM, D_MODEL), lambda i: (i, 0)), w, w],
        out_specs=pl.BlockSpec((TM, F_SHARED), lambda i: (i, 0)),
        out_shape=jax.ShapeDtypeStruct((T_ALL, F_SHARED), BF16),
        scratch_shapes=[pltpu.VMEM((D_MODEL, F_SHARED), BF16), pltpu.VMEM((D_MODEL, F_SHARED), BF16)],
        compiler_params=_cparams(1), name="shared_up",
    )(h, wg, wu)


def _combine_kernel(slot_ref, nxt_ref, y_hbm, rw_ref, act_ref, wd_ref, x_ref, mod_ref, o_ref, buf, wd_bf, sems, *,
                    gate_row):
    i = pl.program_id(0)
    n_rows = TOP_K * CMB_TM

    def row_copy(idx_ref, slot, j):
        return pltpu.make_async_copy(y_hbm.at[pl.ds(idx_ref[0, 0, j], 1), :], buf.at[slot, pl.ds(j, 1), :],
                                     sems.at[slot])

    @pl.when(i == 0)
    def _():
        wd_bf[...] = wd_ref[...].astype(BF16)

        def body(j, carry):
            row_copy(slot_ref, 0, j).start()
            return carry
        lax.fori_loop(0, n_rows, body, 0, unroll=8)

    def tile(prefetch):
        slot = i % 2
        pltpu.make_async_copy(y_hbm.at[pl.ds(0, n_rows), :], buf.at[slot], sems.at[slot]).wait()
        if prefetch:
            for j in range(n_rows):
                row_copy(nxt_ref, 1 - slot, j).start()
        y = jnp.dot(act_ref[...], wd_bf[...], preferred_element_type=F32)
        for k in range(TOP_K):
            y = y + rw_ref[:, k:k + 1] * buf[slot, k * CMB_TM:(k + 1) * CMB_TM, :]
        o_ref[...] = x_ref[...] + mod_ref[gate_row:gate_row + 1, :] * y

    @pl.when(i + 1 < pl.num_programs(0))
    def _():
        tile(True)

    @pl.when(i + 1 == pl.num_programs(0))
    def _():
        tile(False)


def _combine(y_sorted, slot_of, route_w, act, sh_wd, x, mod, layer):
    nt = T_ALL // CMB_TM
    slots = slot_of.reshape(nt, CMB_TM, TOP_K).transpose(0, 2, 1).reshape(nt, 1, TOP_K * CMB_TM)
    slot_spec = lambda nxt: pl.BlockSpec((1, 1, TOP_K * CMB_TM), lambda i: (jnp.minimum(i + nxt, nt - 1), 0, 0),
                                         memory_space=pltpu.SMEM)
    return pl.pallas_call(
        functools.partial(_combine_kernel, gate_row=5),
        grid=(nt,),
        in_specs=[slot_spec(0), slot_spec(1),
                  pl.BlockSpec(memory_space=pl.ANY),
                  pl.BlockSpec((CMB_TM, TOP_K), lambda i: (i, 0)),
                  pl.BlockSpec((CMB_TM, F_SHARED), lambda i: (i, 0)),
                  pl.BlockSpec((None, F_SHARED, D_MODEL), lambda i: (layer, 0, 0)),
                  pl.BlockSpec((CMB_TM, D_MODEL), lambda i: (i, 0)),
                  pl.BlockSpec((None, 6, D_MODEL), lambda i: (_group_of_tile(i, CMB_TM), 0, 0))],
        out_specs=pl.BlockSpec((CMB_TM, D_MODEL), lambda i: (i, 0)),
        out_shape=jax.ShapeDtypeStruct((T_ALL, D_MODEL), F32),
        scratch_shapes=[pltpu.VMEM((2, TOP_K * CMB_TM, D_MODEL), F32), pltpu.VMEM((F_SHARED, D_MODEL), BF16),
                        pltpu.SemaphoreType.DMA((2,))],
        compiler_params=_cparams(1), name="moe_combine",
    )(slots, slots, y_sorted, route_w, act, sh_wd, x, mod)


def kernel(x_prompt, x_sample, cache_na_k, cache_na_v, state_gla, state_s5, c, c_ctx, ada_w, ada_b, norm1_g, norm2_g, w_in, na_rpb, gla_wa, gla_ba, gla_norm_g, s5_a_re, s5_a_im, s5_log_dt, s5_b_re, s5_b_im, s5_c_re, s5_c_im, s5_d, s5_w_glu, w_br_na, w_br_gla, w_br_s5, w_merge, w_out, router_w, router_bias, exp_wg, exp_wu, exp_wd, sh_wg, sh_wu, sh_wd, final_norm_g):
    x = jnp.concatenate([x_prompt.reshape(T_CTX, D_MODEL), x_sample.reshape(T_LAT, D_MODEL)], axis=0)
    cvec = jnp.concatenate([c_ctx[None, :], c, jnp.zeros((MOD_ROWS - N_MOD, D_MODEL), F32)], axis=0)
    mods = _ada(cvec, ada_w, ada_b).reshape(DEPTH, MOD_ROWS, 6, D_MODEL)
    cos, sin = _rope_tables()
    w_in_t = jnp.swapaxes(w_in, 1, 2)
    zero_gla = jnp.zeros((BATCH, 2, GLA_HEADS, GLA_DV, GLA_DK), F32)
    zero_s5 = jnp.zeros((S5_GROUPS, 4, BATCH, S5_N), F32)
    new_k, new_v, new_gla, new_s5 = [], [], [], []
    for l in range(DEPTH):
        mod = mods[l]
        h = _norm(x, norm1_g[l], mod, (0, 1), BF16)
        u = _mm(h, w_in_t, l, 1024, n=IN_MAIN, w_rows_are_outputs=True)
        lr = _mm(h, w_in_t[l, IN_MAIN:IN_MAIN + 2 * GLA_LR], None, 2 * GLA_LR, w_rows_are_outputs=True)
        su = _mm(h, w_in_t[l, IN_MAIN + 2 * GLA_LR:], None, S5_W, w_rows_are_outputs=True)
        gates = _mm(h, w_merge, l, 1024, act="sigmoid")

        o_na = jnp.concatenate([_ctx_attn(u),
                                _na_attn(u, cache_na_k, cache_na_v, _na_bias(na_rpb[l]), l)], axis=0)

        s0_lat = state_gla[:, l].transpose(0, 1, 2, 4, 3)
        og_ctx, sfin = _gla(u, lr, gla_wa, gla_ba, gla_norm_g, zero_gla, cos, sin, l, latent=False)
        og_lat, _ = _gla(u, lr, gla_wa, gla_ba, gla_norm_g, s0_lat, cos, sin, l, latent=True)
        o_gla = jnp.concatenate([og_ctx, og_lat], axis=0)

        ops, step = _s5_operators(s5_a_re[l], s5_a_im[l], s5_log_dt[l], s5_b_re[l], s5_b_im[l],
                                  s5_c_re[l], s5_c_im[l])
        st = state_s5[:, l].astype(F32)
        h0_lat = st.transpose(2, 1, 4, 0, 3).reshape(S5_GROUPS, 4, DEC_BATCH, S5_N)
        y_ctx, hfin = _s5(su[:T_CTX], ops, step, zero_s5, nb=BATCH, seq=SEQ)
        y_lat, _ = _s5(su[T_CTX:], ops, step, h0_lat, nb=DEC_BATCH, seq=DEC_SEQ)
        zz = _glu(jnp.concatenate([y_ctx, y_lat], axis=0), su, s5_d, s5_w_glu, l)

        merged = _merge(o_na, o_gla, zz, gates, w_br_na, w_br_gla, w_br_s5, l)
        x = _out_proj(merged, w_out, x, mod, l)

        h2 = _norm(x, norm2_g[l], mod, (3, 4), F32)
        idx_t, w_t = _router(h2, router_w[l].T, router_bias[l])
        sorted_tok, slot_of, block_e, src, n_used = _dispatch(idx_t.T)
        y_sorted = _experts(h2, sorted_tok, block_e, src, n_used, exp_wg, exp_wu, exp_wd, l)
        act = _shared_up(h2, sh_wg, sh_wu, l)
        x = _combine(y_sorted, slot_of, w_t.T, act, sh_wd, x, mod, l)

        new_k.append(u[:T_CTX, NA_W:2 * NA_W].reshape(BATCH, SEQ, NA_HEADS, NA_DH))
        new_v.append(u[:T_CTX, 2 * NA_W:3 * NA_W].reshape(BATCH, SEQ, NA_HEADS, NA_DH))
        new_gla.append(sfin.transpose(0, 1, 2, 4, 3))
        new_s5.append(hfin.reshape(S5_GROUPS, 2, 2, BATCH, S5_N).transpose(3, 1, 0, 4, 2))

    y = _norm(x, final_norm_g, None, None, F32)
    return (y[:T_CTX].reshape(BATCH, SEQ, D_MODEL), y[T_CTX:].reshape(DEC_BATCH, DEC_SEQ, D_MODEL),
            jnp.stack(new_k, axis=1), jnp.stack(new_v, axis=1),
            jnp.stack(new_gla, axis=1), jnp.stack(new_s5, axis=1))
```

```python
import functools
import math

import jax
import jax.numpy as jnp
from jax import lax
from jax.experimental import pallas as pl
from jax.experimental.pallas import tpu as pltpu

F32 = jnp.float32
BF16 = jnp.bfloat16

D_MODEL = 2048
BATCH = 16
SEQ = 256
DEPTH = 2
DEC_BATCH = 2
DEC_SEQ = 1024
PAST_LEN = 512
GRID_W = 64
NA_HEADS = 8
NA_DH = 128
NA_W = NA_HEADS * NA_DH
NA_ROWS = 8
NA_COLS = 16
GLA_HEADS = 4
GLA_DK = 128
GLA_DV = 256
GLA_QK_W = GLA_HEADS * GLA_DK
GLA_V_W = GLA_HEADS * GLA_DV
GLA_LR = 16
GLA_TAU = 16.0
GLA_CHUNK = 64
GLA_SUB = 16
GLA_UNROLL = 4
ROPE_BASE = 10000.0
S5_W = 1024
S5_GS = 16
S5_GROUPS = S5_W // S5_GS
S5_N = 64
S5_T = 16
IN_W = 3 * NA_W + 2 * GLA_QK_W + 2 * GLA_V_W + 2 * GLA_LR + S5_W
IN_MAIN = 3 * NA_W + 2 * GLA_QK_W + 2 * GLA_V_W
N_EXPERTS = 64
TOP_K = 8
N_GROUPS = 8
TOPK_GROUPS = 4
F_EXPERT = 512
F_SHARED = 512
ROUTED_SCALE = 2.5
EPS = 1e-6

T_CTX = BATCH * SEQ
T_LAT = DEC_BATCH * DEC_SEQ
T_ALL = T_CTX + T_LAT
N_MOD = 1 + DEC_BATCH
MOD_ROWS = 8

TM = 512
MOE_MB = 256
MOE_NB = T_ALL * TOP_K // MOE_MB + N_EXPERTS
MOE_SLOTS = MOE_NB * MOE_MB
CMB_TM = 128
NEG = -1e30

VMEM_LIMIT = 56 * 1024 * 1024

NT = (((1,), (1,)), ((), ()))
TN = (((0,), (0,)), ((), ()))


def _cparams(n_axes):
    return pltpu.CompilerParams(dimension_semantics=("arbitrary",) * n_axes,
                                vmem_limit_bytes=VMEM_LIMIT)


def _group_of_tile(i, tm):
    row = i * tm
    return jnp.where(row < T_CTX, 0, 1 + (row - T_CTX) // DEC_SEQ)


def _silu(x):
    return x * jax.nn.sigmoid(x)


def _gelu_tanh(x):
    return 0.5 * x * (1.0 + jnp.tanh(math.sqrt(2.0 / math.pi) * (x + 0.044715 * (x * x * x))))


def _ada_kernel(c_ref, w_ref, b_ref, o_ref):
    s = _silu(c_ref[...]).astype(BF16)
    o_ref[...] = jnp.dot(s, w_ref[...].astype(BF16), preferred_element_type=F32) + b_ref[...]


def _ada(cvec, ada_w, ada_b):
    tn = 1024
    return pl.pallas_call(
        _ada_kernel,
        grid=(DEPTH, 6 * D_MODEL // tn),
        in_specs=[pl.BlockSpec((MOD_ROWS, D_MODEL), lambda l, j: (0, 0)),
                  pl.BlockSpec((None, D_MODEL, tn), lambda l, j: (l, 0, j)),
                  pl.BlockSpec((None, 1, tn), lambda l, j: (l, 0, j))],
        out_specs=pl.BlockSpec((None, MOD_ROWS, tn), lambda l, j: (l, 0, j)),
        out_shape=jax.ShapeDtypeStruct((DEPTH, MOD_ROWS, 6 * D_MODEL), F32),
        compiler_params=_cparams(2), name="ada",
    )(cvec, ada_w, ada_b.reshape(DEPTH, 1, 6 * D_MODEL))


def _norm_kernel(x_ref, g_ref, *rest, rows):
    o_ref = rest[-1]
    x = x_ref[...]
    y = x * lax.rsqrt(jnp.mean(x * x, axis=-1, keepdims=True) + EPS) * g_ref[...]
    if rows is not None:
        mod_ref = rest[0]
        y = y * (1.0 + mod_ref[rows[1]:rows[1] + 1, :]) + mod_ref[rows[0]:rows[0] + 1, :]
    o_ref[...] = y.astype(o_ref.dtype)


def _norm(x, g, mod, rows, out_dtype):
    tm = 256
    in_specs = [pl.BlockSpec((tm, D_MODEL), lambda i: (i, 0)),
                pl.BlockSpec((1, D_MODEL), lambda i: (0, 0))]
    args = [x, g.reshape(1, D_MODEL)]
    if rows is not None:
        in_specs.append(pl.BlockSpec((None, 6, D_MODEL), lambda i: (_group_of_tile(i, tm), 0, 0)))
        args.append(mod)
    return pl.pallas_call(
        functools.partial(_norm_kernel, rows=rows),
        grid=(T_ALL // tm,),
        in_specs=in_specs,
        out_specs=pl.BlockSpec((tm, D_MODEL), lambda i: (i, 0)),
        out_shape=jax.ShapeDtypeStruct((T_ALL, D_MODEL), out_dtype),
        compiler_params=_cparams(1), name="norm",
    )(*args)


def _mm_kernel(x_ref, w_ref, o_ref, wbf_ref, *, act, w_rows_are_outputs):
    @pl.when(pl.program_id(1) == 0)
    def _():
        wbf_ref[...] = w_ref[...].astype(BF16)

    x = x_ref[...].astype(BF16)
    if w_rows_are_outputs:
        acc = lax.dot_general(x, wbf_ref[...], NT, preferred_element_type=F32)
    else:
        acc = jnp.dot(x, wbf_ref[...], preferred_element_type=F32)
    if act == "sigmoid":
        acc = jax.nn.sigmoid(acc)
    o_ref[...] = acc.astype(o_ref.dtype)


def _mm(x, w, layer, tn, act=None, out_dtype=F32, n=None, w_rows_are_outputs=False):
    k = x.shape[1]
    if n is None:
        n = w.shape[-2] if w_rows_are_outputs else w.shape[-1]
    blk, idx = ((tn, k), lambda j: (j, 0)) if w_rows_are_outputs else ((k, tn), lambda j: (0, j))
    if layer is None:
        w_spec = pl.BlockSpec(blk, lambda j, i: idx(j))
    else:
        w_spec = pl.BlockSpec((None,) + blk, lambda j, i: (layer,) + idx(j))
    return pl.pallas_call(
        functools.partial(_mm_kernel, act=act, w_rows_are_outputs=w_rows_are_outputs),
        grid=(n // tn, T_ALL // TM),
        in_specs=[pl.BlockSpec((TM, k), lambda j, i: (i, 0)), w_spec],
        out_specs=pl.BlockSpec((TM, tn), lambda j, i: (i, j)),
        out_shape=jax.ShapeDtypeStruct((T_ALL, n), out_dtype),
        scratch_shapes=[pltpu.VMEM(blk, BF16)],
        compiler_params=_cparams(2), name="mm",
    )(x, w)


def _ctx_attn_kernel(q_ref, k_ref, v_ref, o_ref):
    scale = NA_DH ** -0.5
    for h in range(NA_HEADS):
        sl = slice(h * NA_DH, (h + 1) * NA_DH)
        q = q_ref[:, sl].astype(BF16)
        k = k_ref[:, sl].astype(BF16)
        v = v_ref[:, sl].astype(BF16)
        s = lax.dot_general(q, k, NT, preferred_element_type=F32) * scale
        p = jnp.exp(s - jnp.max(s, axis=-1, keepdims=True))
        o = jnp.dot(p.astype(BF16), v, preferred_element_type=F32) / jnp.sum(p, axis=-1, keepdims=True)
        o_ref[:, sl] = o.astype(o_ref.dtype)


def _ctx_attn(u):
    spec = lambda cb: pl.BlockSpec((SEQ, NA_W), lambda b: (b, cb))
    return pl.pallas_call(
        _ctx_attn_kernel,
        grid=(BATCH,),
        in_specs=[spec(0), spec(1), spec(2)],
        out_specs=pl.BlockSpec((SEQ, NA_W), lambda b: (b, 0)),
        out_shape=jax.ShapeDtypeStruct((T_CTX, NA_W), BF16),
        compiler_params=_cparams(1), name="ctx_attn",
    )(u, u, u)


NA_GRID_ROWS = DEC_SEQ // GRID_W
NA_KR = min(NA_ROWS, NA_GRID_ROWS)
NA_LOC = NA_KR * GRID_W


def _na_bias(rpb):
    n_off = 2 * NA_ROWS - 1
    pad = GRID_W - NA_COLS
    period = 2 * GRID_W
    p = jnp.pad(rpb, ((0, 0), (0, 0), (pad, period - pad - (2 * NA_COLS - 1))))
    hank = jnp.tile(p, (1, 1, GRID_W + 1))[..., :GRID_W * (period + 1)]
    hank = hank.reshape(NA_HEADS, n_off, GRID_W, period + 1)[..., :GRID_W]
    band = hank[:, :, ::-1, :]
    qc = jnp.arange(GRID_W)[:, None]
    kc = jnp.arange(GRID_W)[None, :]
    win = jnp.clip(qc - NA_COLS // 2, 0, GRID_W - NA_COLS)
    valid = (kc >= win) & (kc < win + NA_COLS)
    band = jnp.where(valid[None, None], band, NEG)
    return band.transpose(0, 2, 1, 3).reshape(NA_HEADS, GRID_W, n_off * GRID_W)


def _na_kernel(q_ref, k_ref, v_ref, ck_ref, cv_ref, bias_ref, o_ref):
    scale = NA_DH ** -0.5
    kb = k_ref[...].astype(BF16)
    vb = v_ref[...].astype(BF16)
    ck = ck_ref[...].astype(BF16)
    cv = cv_ref[...].astype(BF16)
    for r in range(NA_GRID_ROWS):
        first = min(max(r - NA_KR // 2, 0), NA_GRID_ROWS - NA_KR)
        off = first - r + NA_ROWS - 1
        rows = slice(r * GRID_W, (r + 1) * GRID_W)
        keys = slice(first * GRID_W, first * GRID_W + NA_LOC)
        q = q_ref[rows, :].astype(BF16)
        s1 = (lax.dot_general(q, kb[keys, :], NT, preferred_element_type=F32) * scale
              + bias_ref[:, off * GRID_W:off * GRID_W + NA_LOC])
        s2 = lax.dot_general(q, ck, NT, preferred_element_type=F32) * scale
        m = jnp.maximum(jnp.max(s1, axis=-1, keepdims=True), jnp.max(s2, axis=-1, keepdims=True))
        p1 = jnp.exp(s1 - m)
        p2 = jnp.exp(s2 - m)
        den = jnp.sum(p1, axis=-1, keepdims=True) + jnp.sum(p2, axis=-1, keepdims=True)
        o = (jnp.dot(p1.astype(BF16), vb[keys, :], preferred_element_type=F32)
             + jnp.dot(p2.astype(BF16), cv, preferred_element_type=F32))
        o_ref[rows, :] = (o / den).astype(o_ref.dtype)


def _na_attn(u, cache_k, cache_v, bias, layer):
    lat_sb = T_CTX // DEC_SEQ
    ck = cache_k.reshape(DEC_BATCH, DEPTH, PAST_LEN, NA_W)
    cv = cache_v.reshape(DEC_BATCH, DEPTH, PAST_LEN, NA_W)
    qkv_spec = lambda cb: pl.BlockSpec((DEC_SEQ, NA_DH), lambda b, h: (lat_sb + b, cb * NA_HEADS + h))
    c_spec = pl.BlockSpec((None, None, PAST_LEN, NA_DH), lambda b, h: (b, layer, 0, h))
    return pl.pallas_call(
        _na_kernel,
        grid=(DEC_BATCH, NA_HEADS),
        in_specs=[qkv_spec(0), qkv_spec(1), qkv_spec(2), c_spec, c_spec,
                  pl.BlockSpec((None, GRID_W, (2 * NA_ROWS - 1) * GRID_W), lambda b, h: (h, 0, 0))],
        out_specs=pl.BlockSpec((DEC_SEQ, NA_DH), lambda b, h: (b, h)),
        out_shape=jax.ShapeDtypeStruct((T_LAT, NA_W), BF16),
        compiler_params=_cparams(2), name="na_attn",
    )(u, u, u, ck, cv, bias)


def _rope_tables():
    half = GLA_DK // 2
    nf = half // 2
    t = jnp.arange(DEC_SEQ)
    freqs = ROPE_BASE ** (-jnp.arange(nf, dtype=F32) / nf)
    ang_r = (t // GRID_W).astype(F32)[:, None] * freqs
    ang_c = (t % GRID_W).astype(F32)[:, None] * freqs
    cos = jnp.concatenate([jnp.cos(ang_r), jnp.cos(ang_r), jnp.cos(ang_c), jnp.cos(ang_c)], axis=-1)
    sin = jnp.concatenate([-jnp.sin(ang_r), jnp.sin(ang_r), -jnp.sin(ang_c), jnp.sin(ang_c)], axis=-1)
    return cos, sin


def _split_bf16(x):
    hi = x.astype(BF16)
    return hi, (x - hi.astype(F32)).astype(BF16)


def _gla_kernel(q_ref, k_ref, v_ref, gg_ref, lr_ref, wa_ref, ba_ref, ng_ref, cos_ref, sin_ref, s0_ref,
                o_ref, sfin_ref, qs, ks, las, o_acc, st, *, seq, rope):
    nc = seq // GLA_CHUNK
    nq = GLA_DK // 4

    def rot(x):
        lane = lax.broadcasted_iota(jnp.int32, x.shape, 1)
        partner = jnp.where((lane % (2 * nq)) < nq,
                            pltpu.roll(x, GLA_DK - nq, axis=1), pltpu.roll(x, nq, axis=1))
        return x * cos_ref[...] + partner * sin_ref[...]

    q = q_ref[...] * (GLA_DK ** -0.5)
    k = k_ref[...]
    if rope:
        q = rot(q)
        k = rot(k)
    qs[...] = q
    ks[...] = k
    lr = lr_ref[...]
    for d in range(2):
        z = jnp.dot(lr[:, d * GLA_LR:(d + 1) * GLA_LR].astype(BF16), wa_ref[d].astype(BF16),
                    preferred_element_type=F32) + ba_ref[d:d + 1, :]
        las[d] = -(jnp.maximum(-z, 0.0) + jnp.log1p(jnp.exp(-jnp.abs(z)))) / GLA_TAU
        st[d] = s0_ref[d]

    row = lax.broadcasted_iota(jnp.int32, (GLA_CHUNK, GLA_CHUNK), 0)
    col = lax.broadcasted_iota(jnp.int32, (GLA_CHUNK, GLA_CHUNK), 1)
    key_row = lax.broadcasted_iota(jnp.int32, (GLA_CHUNK, 1), 0)

    def chunk(c, d):
        rev = d == 1
        rows = pl.ds(pl.multiple_of(c * GLA_CHUNK, GLA_CHUNK), GLA_CHUNK)
        qc = qs[rows, :]
        kc = ks[rows, :]
        vc = v_ref[rows, :].astype(BF16)
        la = las[d, rows, :]
        causal = (col >= row) if rev else (col <= row)
        tri = jnp.where(causal, 1.0, 0.0).astype(BF16)
        la_hi, la_lo = _split_bf16(la)
        b = (jnp.dot(tri, la_hi, preferred_element_type=F32)
             + jnp.dot(tri, la_lo, preferred_element_type=F32))
        bex = b - la
        b_last = b[0:1, :] if rev else b[GLA_CHUNK - 1:GLA_CHUNK, :]
        blocks = []
        for i in range(GLA_CHUNK // GLA_SUB):
            lo, hi = i * GLA_SUB, (i + 1) * GLA_SUB
            ref = bex[hi - 1:hi, :] if rev else bex[lo:lo + 1, :]
            qt = (qc[lo:hi, :] * jnp.exp(b[lo:hi, :] - ref)).astype(BF16)
            allowed = (key_row >= lo) if rev else (key_row < hi)
            kt = (kc * jnp.exp(jnp.where(allowed, ref - b, -jnp.inf))).astype(BF16)
            blocks.append(lax.dot_general(qt, kt, NT, preferred_element_type=F32))
        att = jnp.where(causal, jnp.concatenate(blocks, axis=0), 0.0)
        s_t = st[d]
        o = (jnp.dot(att.astype(BF16), vc, preferred_element_type=F32)
             + lax.dot_general((qc * jnp.exp(b)).astype(BF16), s_t.astype(BF16), NT,
                               preferred_element_type=F32))
        khat = (kc * jnp.exp(b_last - b)).astype(BF16)
        st[d] = s_t * jnp.exp(b_last) + lax.dot_general(vc, khat, TN, preferred_element_type=F32)
        o_acc[d, rows, :] = o

    def both(c, carry):
        chunk(c, 0)
        chunk(nc - 1 - c, 1)
        return carry

    lax.fori_loop(0, nc, both, 0, unroll=GLA_UNROLL)
    sfin_ref[...] = st[...]
    o = o_acc[0] + o_acc[1]
    o = o * lax.rsqrt(jnp.mean(o * o, axis=-1, keepdims=True) + EPS) * ng_ref[...]
    o_ref[...] = (o * _silu(gg_ref[...])).astype(o_ref.dtype)


def _gla(u, lr, wa, ba, norm_g, s0_t, cos, sin, layer, *, latent):
    seq, nb, first = (DEC_SEQ, DEC_BATCH, T_CTX // DEC_SEQ) if latent else (SEQ, BATCH, 0)
    qk_cb = 3 * NA_W // GLA_DK
    v_cb = (3 * NA_W + 2 * GLA_QK_W) // GLA_DV
    row = lambda w, cb: pl.BlockSpec((seq, w), lambda b, h: (first + b, cb + h))
    return pl.pallas_call(
        functools.partial(_gla_kernel, seq=seq, rope=latent),
        grid=(nb, GLA_HEADS),
        in_specs=[row(GLA_DK, qk_cb), row(GLA_DK, qk_cb + GLA_HEADS), row(GLA_DV, v_cb),
                  row(GLA_DV, v_cb + GLA_HEADS),
                  pl.BlockSpec((seq, 2 * GLA_LR), lambda b, h: (first + b, 0)),
                  pl.BlockSpec((None, 2, GLA_LR, GLA_DK), lambda b, h: (layer, 0, 0, h)),
                  pl.BlockSpec((None, 2, GLA_DK), lambda b, h: (layer, 0, h)),
                  pl.BlockSpec((None, 1, GLA_DV), lambda b, h: (layer, 0, h)),
                  pl.BlockSpec((seq, GLA_DK), lambda b, h: (0, 0)),
                  pl.BlockSpec((seq, GLA_DK), lambda b, h: (0, 0)),
                  pl.BlockSpec((None, 2, None, GLA_DV, GLA_DK), lambda b, h: (b, 0, h, 0, 0))],
        out_specs=[pl.BlockSpec((seq, GLA_DV), lambda b, h: (b, h)),
                   pl.BlockSpec((None, 2, None, GLA_DV, GLA_DK), lambda b, h: (b, 0, h, 0, 0))],
        out_shape=[jax.ShapeDtypeStruct((nb * seq, GLA_V_W), BF16),
                   jax.ShapeDtypeStruct((nb, 2, GLA_HEADS, GLA_DV, GLA_DK), F32)],
        scratch_shapes=[pltpu.VMEM((seq, GLA_DK), F32), pltpu.VMEM((seq, GLA_DK), F32),
                        pltpu.VMEM((2, seq, GLA_DK), F32), pltpu.VMEM((2, seq, GLA_DV), F32),
                        pltpu.VMEM((2, GLA_DV, GLA_DK), F32)],
        compiler_params=_cparams(2), name="gla",
    )(u, u, u, u, lr, wa, ba, norm_g.reshape(DEPTH, 1, GLA_V_W), cos[:seq], sin[:seq], s0_t)


def _s5_operators(a_re, a_im, log_dt, b_re, b_im, c_re, c_im):
    hp = lax.Precision.HIGHEST
    lam = lax.complex(a_re.astype(F32), a_im.astype(F32))
    lam_dt = lam * jnp.exp(log_dt.astype(F32))[..., None]
    a_bar = jnp.exp(lam_dt)
    b_bar = ((a_bar - 1.0) / lam)[..., None] * lax.complex(b_re.astype(F32), b_im.astype(F32))
    cc = lax.complex(c_re.astype(F32), c_im.astype(F32))
    taus = jnp.arange(S5_T + 1, dtype=F32)
    pw = jnp.exp(lam_dt[:, :, None, :] * taus[None, None, :, None])
    lanes = S5_T * S5_GS

    def out_weights(pwd, cd):
        m = pwd.transpose(0, 2, 1)[:, :, :, None] * cd.transpose(0, 2, 1)[:, :, None, :]
        return m.reshape(S5_GROUPS, S5_N, lanes)

    def lag_response(d):
        r = out_weights(pw[d, :, :S5_T], cc[d])
        bt = b_bar[d].transpose(0, 2, 1)
        return (jnp.einsum("gin,gnx->gix", jnp.real(bt), jnp.real(r), precision=hp)
                - jnp.einsum("gin,gnx->gix", jnp.imag(bt), jnp.imag(r), precision=hp))

    def toeplitz(lags):
        rows = [lags[:, :, (S5_T - 1 - s) * S5_GS:(2 * S5_T - 1 - s) * S5_GS] for s in range(S5_T)]
        return jnp.stack(rows, axis=1).reshape(S5_GROUPS, lanes, lanes)

    no_lag = jnp.zeros((S5_GROUPS, S5_GS, (S5_T - 1) * S5_GS), F32)
    rev_lags = lag_response(1).reshape(S5_GROUPS, S5_GS, S5_T, S5_GS)[:, :, ::-1].reshape(S5_GROUPS, S5_GS, lanes)
    toep_f = toeplitz(jnp.concatenate([no_lag, lag_response(0)], axis=-1))
    toep_b = toeplitz(jnp.concatenate([rev_lags, no_lag], axis=-1))

    def state_in(pwd, bd):
        m = pwd[:, :, None, :] * bd.transpose(0, 2, 1)[:, None, :, :]
        m = m.reshape(S5_GROUPS, lanes, S5_N)
        return [jnp.real(m), jnp.imag(m)]

    p = jnp.stack(state_in(pw[0, :, S5_T - 1::-1], b_bar[0])
                  + state_in(pw[1, :, :S5_T], b_bar[1]), axis=1)

    def state_out(pwd, cd):
        m = out_weights(pwd, cd)
        return [jnp.real(m), -jnp.imag(m)]

    q = jnp.stack(state_out(pw[0, :, 1:S5_T + 1], cc[0])
                  + state_out(pw[1, :, S5_T:0:-1], cc[1]), axis=1)
    ops = [m.astype(BF16) for m in (toep_f, toep_b, p, q)]
    a_t = pw[:, :, S5_T]
    step = jnp.stack([jnp.real(a_t[0]), jnp.imag(a_t[0]), jnp.real(a_t[1]), jnp.imag(a_t[1])], axis=1)
    return ops, step


S5_GB = 128 // S5_GS
S5_SEG = 128 // S5_GS


def _s5_kernel(x_ref, tf_ref, tb_ref, p_ref, q_ref, step_ref, h0_ref, y_ref, hfin_ref,
               u_scr, y_scr, e_scr, hin_scr, *, nc, nb):
    n_chunks = nb * nc
    for s in range(S5_T):
        xs = x_ref[pl.ds(s, n_chunks, stride=S5_T), :]
        half, pos = divmod(s, S5_SEG)
        dst = slice(half * 128 + pos * S5_GS, half * 128 + (pos + 1) * S5_GS)
        for g in range(S5_GB):
            moved = pltpu.roll(xs, ((pos - g) % S5_SEG) * S5_GS, axis=1)
            u_scr[g, :, dst] = moved[:, pos * S5_GS:(pos + 1) * S5_GS].astype(BF16)

    def group(g, carry):
        u = u_scr[g]
        for m in range(4):
            e_scr[m] = jnp.dot(u, p_ref[g, m], preferred_element_type=F32)
        ar = [step_ref[g, 0:1, :], step_ref[g, 2:3, :]]
        ai = [step_ref[g, 1:2, :], step_ref[g, 3:4, :]]
        hr = [h0_ref[g, 0], h0_ref[g, 2]]
        hi = [h0_ref[g, 1], h0_ref[g, 3]]
        for c in range(nc):
            for d in range(2):
                rows = pl.ds(c if d == 0 else nc - 1 - c, nb, stride=nc)
                hin_scr[2 * d, rows, :] = hr[d]
                hin_scr[2 * d + 1, rows, :] = hi[d]
                hr[d], hi[d] = (hr[d] * ar[d] - hi[d] * ai[d] + e_scr[2 * d, rows, :],
                                hi[d] * ar[d] + hr[d] * ai[d] + e_scr[2 * d + 1, rows, :])
        for d in range(2):
            hfin_ref[g, 2 * d] = hr[d]
            hfin_ref[g, 2 * d + 1] = hi[d]
        y = (jnp.dot(u, tf_ref[g], preferred_element_type=F32)
             + jnp.dot(u, tb_ref[g], preferred_element_type=F32))
        for m in range(4):
            y += jnp.dot(hin_scr[m].astype(BF16), q_ref[g, m], preferred_element_type=F32)
        y_scr[g] = y
        return carry

    lax.fori_loop(0, S5_GB, group, 0)

    segment = lax.broadcasted_iota(jnp.int32, (n_chunks, 128), 1) // S5_GS
    for t in range(S5_T):
        half, pos = divmod(t, S5_SEG)
        rows = jnp.zeros((n_chunks, 128), F32)
        for g in range(S5_GB):
            moved = pltpu.roll(y_scr[g, :, half * 128:(half + 1) * 128], ((g - pos) % S5_SEG) * S5_GS, axis=1)
            rows = jnp.where(segment == g, moved, rows)
        y_ref[pl.ds(t, n_chunks, stride=S5_T), :] = rows


def _s5(su, ops, step, h0, *, nb, seq, first_row_block):
    nc = seq // S5_T
    lanes = S5_T * S5_GS
    n_chunks = nb * nc
    gb = lambda *shape: pl.BlockSpec((S5_GB,) + shape, lambda g: (g,) + (0,) * len(shape))
    return pl.pallas_call(
        functools.partial(_s5_kernel, nc=nc, nb=nb),
        grid=(S5_GROUPS // S5_GB,),
        in_specs=[pl.BlockSpec((nb * seq, S5_GB * S5_GS), lambda g: (first_row_block, g)),
                  gb(lanes, lanes), gb(lanes, lanes), gb(4, lanes, S5_N), gb(4, S5_N, lanes),
                  gb(4, S5_N), gb(4, nb, S5_N)],
        out_specs=[pl.BlockSpec((nb * seq, S5_GB * S5_GS), lambda g: (0, g)), gb(4, nb, S5_N)],
        out_shape=[jax.ShapeDtypeStruct((nb * seq, S5_W), F32),
                   jax.ShapeDtypeStruct((S5_GROUPS, 4, nb, S5_N), F32)],
        scratch_shapes=[pltpu.VMEM((S5_GB, n_chunks, lanes), BF16), pltpu.VMEM((S5_GB, n_chunks, lanes), F32),
                        pltpu.VMEM((4, n_chunks, S5_N), F32), pltpu.VMEM((4, n_chunks, S5_N), F32)],
        compiler_params=_cparams(1), name="s5",
    )(su, *ops, step, h0)


def _glu_kernel(y_ref, su_ref, d_ref, w_ref, o_ref, wbf_ref):
    @pl.when(pl.program_id(0) == 0)
    def _():
        wbf_ref[...] = w_ref[...].astype(BF16)

    z = _gelu_tanh(y_ref[...] + d_ref[...] * su_ref[...])
    gate = jax.nn.sigmoid(jnp.dot(z.astype(BF16), wbf_ref[...], preferred_element_type=F32))
    o_ref[...] = (z * gate).astype(o_ref.dtype)


def _glu(y, su, d, w_glu, layer):
    row = pl.BlockSpec((TM, S5_W), lambda i: (i, 0))
    return pl.pallas_call(
        _glu_kernel,
        grid=(T_ALL // TM,),
        in_specs=[row, row, pl.BlockSpec((None, 1, S5_W), lambda i: (layer, 0, 0)),
                  pl.BlockSpec((None, S5_W, S5_W), lambda i: (layer, 0, 0))],
        out_specs=row,
        out_shape=jax.ShapeDtypeStruct((T_ALL, S5_W), BF16),
        scratch_shapes=[pltpu.VMEM((S5_W, S5_W), BF16)],
        compiler_params=_cparams(1), name="glu",
    )(y, su, d.reshape(DEPTH, 1, S5_W), w_glu)


def _merge_kernel(a_ref, b_ref, c_ref, ga_ref, gb_ref, gc_ref, wa_ref, wb_ref, wc_ref, o_ref, wbf_ref):
    @pl.when(pl.program_id(1) == 0)
    def _():
        wbf_ref[0] = wa_ref[...].astype(BF16)
        wbf_ref[1] = wb_ref[...].astype(BF16)
        wbf_ref[2] = wc_ref[...].astype(BF16)

    acc = ga_ref[...] * jnp.dot(a_ref[...], wbf_ref[0], preferred_element_type=F32)
    acc += gb_ref[...] * jnp.dot(b_ref[...], wbf_ref[1], preferred_element_type=F32)
    acc += gc_ref[...] * jnp.dot(c_ref[...], wbf_ref[2], preferred_element_type=F32)
    o_ref[...] = acc.astype(o_ref.dtype)


def _merge(o_na, o_gla, zz, gates, w_na, w_gla, w_s5, layer):
    tn = 1024
    ncb = D_MODEL // tn
    act = pl.BlockSpec((TM, NA_W), lambda j, i: (i, 0))
    gate = lambda k: pl.BlockSpec((TM, tn), lambda j, i: (i, k * ncb + j))
    w = pl.BlockSpec((None, NA_W, tn), lambda j, i: (layer, 0, j))
    return pl.pallas_call(
        _merge_kernel,
        grid=(ncb, T_ALL // TM),
        in_specs=[act, act, act, gate(0), gate(1), gate(2), w, w, w],
        out_specs=pl.BlockSpec((TM, tn), lambda j, i: (i, j)),
        out_shape=jax.ShapeDtypeStruct((T_ALL, D_MODEL), BF16),
        scratch_shapes=[pltpu.VMEM((3, NA_W, tn), BF16)],
        compiler_params=_cparams(2), name="merge",
    )(o_na, o_gla, zz, gates, gates, gates, w_na, w_gla, w_s5)


def _out_proj_kernel(m_ref, w_ref, x_ref, mod_ref, o_ref, wbf_ref, *, gate_row):
    @pl.when(pl.program_id(1) == 0)
    def _():
        wbf_ref[...] = w_ref[...].astype(BF16)

    y = jnp.dot(m_ref[...], wbf_ref[...], preferred_element_type=F32)
    o_ref[...] = x_ref[...] + mod_ref[gate_row:gate_row + 1, :] * y


def _out_proj(merged, w_out, x, mod, layer):
    tn = 1024
    return pl.pallas_call(
        functools.partial(_out_proj_kernel, gate_row=2),
        grid=(D_MODEL // tn, T_ALL // TM),
        in_specs=[pl.BlockSpec((TM, D_MODEL), lambda j, i: (i, 0)),
                  pl.BlockSpec((None, D_MODEL, tn), lambda j, i: (layer, 0, j)),
                  pl.BlockSpec((TM, tn), lambda j, i: (i, j)),
                  pl.BlockSpec((None, 6, tn), lambda j, i: (_group_of_tile(i, TM), 0, j))],
        out_specs=pl.BlockSpec((TM, tn), lambda j, i: (i, j)),
        out_shape=jax.ShapeDtypeStruct((T_ALL, D_MODEL), F32),
        scratch_shapes=[pltpu.VMEM((D_MODEL, tn), BF16)],
        compiler_params=_cparams(2), name="out_proj",
    )(merged, w_out, x, mod)


def _first_max(v, iota, n):
    m = jnp.max(v, axis=0, keepdims=True)
    first = jnp.min(jnp.where(v == m, iota, float(n)), axis=0, keepdims=True)
    return m, first


def _router_kernel(x_ref, wt_ref, bias_ref, idx_ref, w_ref):
    per = N_EXPERTS // N_GROUPS
    x_hi, x_lo = _split_bf16(x_ref[...])
    w_hi, w_lo = _split_bf16(wt_ref[...])
    logits = (lax.dot_general(w_hi, x_hi, NT, preferred_element_type=F32)
              + lax.dot_general(w_hi, x_lo, NT, preferred_element_type=F32)
              + lax.dot_general(w_lo, x_hi, NT, preferred_element_type=F32))
    scores = jax.nn.sigmoid(logits)
    sel = scores + bias_ref[...]
    tm = sel.shape[1]
    iota_g = lax.broadcasted_iota(jnp.int32, (per, tm), 0).astype(F32)
    grp_rows = []
    for g in range(N_GROUPS):
        v = sel[g * per:(g + 1) * per, :]
        m1, first = _first_max(v, iota_g, per)
        m2 = jnp.max(jnp.where(iota_g == first, -jnp.inf, v), axis=0, keepdims=True)
        grp_rows.append(m1 + m2)
    grp = jnp.concatenate(grp_rows, axis=0)
    iota_n = lax.broadcasted_iota(jnp.int32, (N_GROUPS, tm), 0).astype(F32)
    chosen = jnp.zeros((N_GROUPS, tm), F32)
    for _ in range(TOPK_GROUPS):
        _, first = _first_max(grp, iota_n, N_GROUPS)
        hit = iota_n == first
        chosen = jnp.where(hit, 1.0, chosen)
        grp = jnp.where(hit, -jnp.inf, grp)
    mask = jnp.concatenate([jnp.broadcast_to(chosen[g:g + 1, :], (per, tm)) for g in range(N_GROUPS)], axis=0)
    sel = jnp.where(mask > 0.5, sel, -jnp.inf)
    iota_e = lax.broadcasted_iota(jnp.int32, (N_EXPERTS, tm), 0).astype(F32)
    ids, ws = [], []
    for _ in range(TOP_K):
        _, first = _first_max(sel, iota_e, N_EXPERTS)
        hit = iota_e == first
        ids.append(first)
        ws.append(jnp.sum(jnp.where(hit, scores, 0.0), axis=0, keepdims=True))
        sel = jnp.where(hit, -jnp.inf, sel)
    w = jnp.concatenate(ws, axis=0)
    idx_ref[...] = jnp.concatenate(ids, axis=0).astype(jnp.int32)
    w_ref[...] = w / jnp.sum(w, axis=0, keepdims=True) * ROUTED_SCALE


def _router(h, router_w_t, router_bias):
    tm = 256
    return pl.pallas_call(
        _router_kernel,
        grid=(T_ALL // tm,),
        in_specs=[pl.BlockSpec((tm, D_MODEL), lambda i: (i, 0)),
                  pl.BlockSpec((N_EXPERTS, D_MODEL), lambda i: (0, 0)),
                  pl.BlockSpec((N_EXPERTS, 1), lambda i: (0, 0))],
        out_specs=[pl.BlockSpec((TOP_K, tm), lambda i: (0, i)), pl.BlockSpec((TOP_K, tm), lambda i: (0, i))],
        out_shape=[jax.ShapeDtypeStruct((TOP_K, T_ALL), jnp.int32), jax.ShapeDtypeStruct((TOP_K, T_ALL), F32)],
        compiler_params=_cparams(1), name="router",
    )(h, router_w_t, router_bias.reshape(N_EXPERTS, 1))


def _dispatch(idx):
    n_assign = T_ALL * TOP_K
    flat_e = idx.reshape(-1)
    experts = jnp.arange(N_EXPERTS, dtype=jnp.int32)
    sorted_e, order = lax.sort((flat_e, jnp.arange(n_assign, dtype=jnp.int32)), num_keys=1, is_stable=True)
    counts = jnp.sum((flat_e[:, None] == experts[None, :]).astype(jnp.int32), axis=0)
    cnt_end = jnp.cumsum(counts)
    cnt_start = cnt_end - counts
    padded = (counts + MOE_MB - 1) // MOE_MB * MOE_MB
    pad_end = jnp.cumsum(padded)
    pad_start = pad_end - padded
    onehot = (sorted_e[:, None] == experts[None, :]).astype(jnp.int32)
    slot_sorted = jnp.arange(n_assign, dtype=jnp.int32) + jnp.sum(onehot * (pad_start - cnt_start)[None, :], axis=1)
    _, slot_of = lax.sort((order, slot_sorted), num_keys=1)
    blk_first = jnp.arange(MOE_NB, dtype=jnp.int32) * MOE_MB
    block_e = jnp.minimum(jnp.sum((pad_end[None, :] <= blk_first[:, None]).astype(jnp.int32), axis=1),
                          N_EXPERTS - 1)
    blk_onehot = (block_e[:, None] == experts[None, :]).astype(jnp.int32)
    pick = lambda v: jnp.sum(blk_onehot * v[None, :], axis=1)
    src = jnp.clip(pick(cnt_start) + blk_first - pick(pad_start), 0, n_assign)
    sorted_tok = jnp.pad(order // TOP_K, (0, MOE_MB))
    n_used = (pad_end[-1] // MOE_MB).astype(jnp.int32).reshape(1)
    return sorted_tok, slot_of.reshape(T_ALL, TOP_K), block_e.astype(jnp.int32), src.astype(jnp.int32), n_used


def _expert_kernel(be_ref, nu_ref, src_ref, stok_ref, x_hbm, wg_ref, wu_ref, wd_ref, o_ref,
                   xbuf, x_bf, wg_bf, wu_bf, wd_bf, sems):
    i = pl.program_id(0)
    n_used = nu_ref[0]
    last = MOE_NB - 1

    def row_copy(blk, slot, j):
        tok = stok_ref[src_ref[blk] + j]
        return pltpu.make_async_copy(x_hbm.at[pl.ds(tok, 1), :], xbuf.at[slot, pl.ds(j, 1), :], sems.at[slot])

    def wait_block(slot):
        pltpu.make_async_copy(x_hbm.at[pl.ds(0, MOE_MB), :], xbuf.at[slot], sems.at[slot]).wait()

    @pl.when(i == 0)
    def _():
        def body(j, carry):
            row_copy(0, 0, j).start()
            return carry
        lax.fori_loop(0, MOE_MB, body, 0, unroll=8)

    def compute(prefetch):
        @pl.when((i == 0) | (be_ref[i] != be_ref[jnp.maximum(i - 1, 0)]))
        def _():
            wg_bf[...] = wg_ref[...].astype(BF16)
            wu_bf[...] = wu_ref[...].astype(BF16)
            wd_bf[...] = wd_ref[...].astype(BF16)

        slot = i % 2
        wait_block(slot)
        x_bf[...] = xbuf[slot].astype(BF16)
        if prefetch:
            for j in range(MOE_MB):
                row_copy(i + 1, 1 - slot, j).start()
        x = x_bf[...]
        g = jnp.dot(x, wg_bf[...], preferred_element_type=F32)
        u = jnp.dot(x, wu_bf[...], preferred_element_type=F32)
        a = (_silu(g) * u).astype(BF16)
        o_ref[...] = jnp.dot(a, wd_bf[...], preferred_element_type=F32)

    @pl.when((i < n_used) & (i < last))
    def _():
        compute(True)

    @pl.when((i < n_used) & (i == last))
    def _():
        compute(False)

    @pl.when(i >= n_used)
    def _():
        o_ref[...] = jnp.zeros_like(o_ref)

        @pl.when(i == n_used)
        def _():
            wait_block(i % 2)


def _experts(x, sorted_tok, block_e, src, n_used, wg, wu, wd, layer):
    w_idx = lambda i, be, nu, src, stok: (layer, be[i], 0, 0)
    w_up = pl.BlockSpec((None, None, D_MODEL, F_EXPERT), w_idx)
    return pl.pallas_call(
        _expert_kernel,
        grid_spec=pltpu.PrefetchScalarGridSpec(
            num_scalar_prefetch=4,
            grid=(MOE_NB,),
            in_specs=[pl.BlockSpec(memory_space=pl.ANY), w_up, w_up,
                      pl.BlockSpec((None, None, F_EXPERT, D_MODEL), w_idx)],
            out_specs=pl.BlockSpec((MOE_MB, D_MODEL), lambda i, be, nu, src, stok: (i, 0)),
            scratch_shapes=[pltpu.VMEM((2, MOE_MB, D_MODEL), F32), pltpu.VMEM((MOE_MB, D_MODEL), BF16),
                            pltpu.VMEM((D_MODEL, F_EXPERT), BF16), pltpu.VMEM((D_MODEL, F_EXPERT), BF16),
                            pltpu.VMEM((F_EXPERT, D_MODEL), BF16), pltpu.SemaphoreType.DMA((2,))]),
        out_shape=jax.ShapeDtypeStruct((MOE_SLOTS, D_MODEL), F32),
        compiler_params=_cparams(1), name="moe_experts",
    )(block_e, n_used, src, sorted_tok, x, wg, wu, wd)


def _shared_up_kernel(x_ref, wg_ref, wu_ref, o_ref, wg_bf, wu_bf):
    @pl.when(pl.program_id(0) == 0)
    def _():
        wg_bf[...] = wg_ref[...].astype(BF16)
        wu_bf[...] = wu_ref[...].astype(BF16)

    x = x_ref[...].astype(BF16)
    g = jnp.dot(x, wg_bf[...], preferred_element_type=F32)
    u = jnp.dot(x, wu_bf[...], preferred_element_type=F32)
    o_ref[...] = (_silu(g) * u).astype(o_ref.dtype)


def _shared_up(h, wg, wu, layer):
    w = pl.BlockSpec((None, D_MODEL, F_SHARED), lambda i: (layer, 0, 0))
    return pl.pallas_call(
        _shared_up_kernel,
        grid=(T_ALL // TM,),
        in_specs=[pl.BlockSpec((TM, D_MODEL), lambda i: (i, 0)), w, w],
        out_specs=pl.BlockSpec((TM, F_SHARED), lambda i: (i, 0)),
        out_shape=jax.ShapeDtypeStruct((T_ALL, F_SHARED), BF16),
        scratch_shapes=[pltpu.VMEM((D_MODEL, F_SHARED), BF16), pltpu.VMEM((D_MODEL, F_SHARED), BF16)],
        compiler_params=_cparams(1), name="shared_up",
    )(h, wg, wu)


def _combine_kernel(slot_ref, nxt_ref, y_hbm, rw_ref, act_ref, wd_ref, x_ref, mod_ref, o_ref, buf, wd_bf, sems, *,
                    gate_row):
    i = pl.program_id(0)
    n_rows = TOP_K * CMB_TM

    def row_copy(idx_ref, slot, j):
        return pltpu.make_async_copy(y_hbm.at[pl.ds(idx_ref[0, 0, j], 1), :], buf.at[slot, pl.ds(j, 1), :],
                                     sems.at[slot])

    @pl.when(i == 0)
    def _():
        wd_bf[...] = wd_ref[...].astype(BF16)

        def body(j, carry):
            row_copy(slot_ref, 0, j).start()
            return carry
        lax.fori_loop(0, n_rows, body, 0, unroll=8)

    def tile(prefetch):
        slot = i % 2
        pltpu.make_async_copy(y_hbm.at[pl.ds(0, n_rows), :], buf.at[slot], sems.at[slot]).wait()
        if prefetch:
            for j in range(n_rows):
                row_copy(nxt_ref, 1 - slot, j).start()
        y = jnp.dot(act_ref[...], wd_bf[...], preferred_element_type=F32)
        for k in range(TOP_K):
            y = y + rw_ref[:, k:k + 1] * buf[slot, k * CMB_TM:(k + 1) * CMB_TM, :]
        o_ref[...] = x_ref[...] + mod_ref[gate_row:gate_row + 1, :] * y

    @pl.when(i + 1 < pl.num_programs(0))
    def _():
        tile(True)

    @pl.when(i + 1 == pl.num_programs(0))
    def _():
        tile(False)


def _combine(y_sorted, slot_of, route_w, act, sh_wd, x, mod, layer):
    nt = T_ALL // CMB_TM
    slots = slot_of.reshape(nt, CMB_TM, TOP_K).transpose(0, 2, 1).reshape(nt, 1, TOP_K * CMB_TM)
    slot_spec = lambda nxt: pl.BlockSpec((1, 1, TOP_K * CMB_TM), lambda i: (jnp.minimum(i + nxt, nt - 1), 0, 0),
                                         memory_space=pltpu.SMEM)
    return pl.pallas_call(
        functools.partial(_combine_kernel, gate_row=5),
        grid=(nt,),
        in_specs=[slot_spec(0), slot_spec(1),
                  pl.BlockSpec(memory_space=pl.ANY),
                  pl.BlockSpec((CMB_TM, TOP_K), lambda i: (i, 0)),
                  pl.BlockSpec((CMB_TM, F_SHARED), lambda i: (i, 0)),
                  pl.BlockSpec((None, F_SHARED, D_MODEL), lambda i: (layer, 0, 0)),
                  pl.BlockSpec((CMB_TM, D_MODEL), lambda i: (i, 0)),
                  pl.BlockSpec((None, 6, D_MODEL), lambda i: (_group_of_tile(i, CMB_TM), 0, 0))],
        out_specs=pl.BlockSpec((CMB_TM, D_MODEL), lambda i: (i, 0)),
        out_shape=jax.ShapeDtypeStruct((T_ALL, D_MODEL), F32),
        scratch_shapes=[pltpu.VMEM((2, TOP_K * CMB_TM, D_MODEL), F32), pltpu.VMEM((F_SHARED, D_MODEL), BF16),
                        pltpu.SemaphoreType.DMA((2,))],
        compiler_params=_cparams(1), name="moe_combine",
    )(slots, slots, y_sorted, route_w, act, sh_wd, x, mod)


def kernel(x_prompt, x_sample, cache_na_k, cache_na_v, state_gla, state_s5, c, c_ctx, ada_w, ada_b, norm1_g, norm2_g, w_in, na_rpb, gla_wa, gla_ba, gla_norm_g, s5_a_re, s5_a_im, s5_log_dt, s5_b_re, s5_b_im, s5_c_re, s5_c_im, s5_d, s5_w_glu, w_br_na, w_br_gla, w_br_s5, w_merge, w_out, router_w, router_bias, exp_wg, exp_wu, exp_wd, sh_wg, sh_wu, sh_wd, final_norm_g):
    x = jnp.concatenate([x_prompt.reshape(T_CTX, D_MODEL), x_sample.reshape(T_LAT, D_MODEL)], axis=0)
    cvec = jnp.concatenate([c_ctx[None, :], c, jnp.zeros((MOD_ROWS - N_MOD, D_MODEL), F32)], axis=0)
    mods = _ada(cvec, ada_w, ada_b).reshape(DEPTH, MOD_ROWS, 6, D_MODEL)
    cos, sin = _rope_tables()
    w_in_t = jnp.swapaxes(w_in, 1, 2)
    zero_gla = jnp.zeros((BATCH, 2, GLA_HEADS, GLA_DV, GLA_DK), F32)
    zero_s5 = jnp.zeros((S5_GROUPS, 4, BATCH, S5_N), F32)
    new_k, new_v, new_gla, new_s5 = [], [], [], []
    for l in range(DEPTH):
        mod = mods[l]
        h = _norm(x, norm1_g[l], mod, (0, 1), BF16)
        u = _mm(h, w_in_t, l, 1024, n=IN_MAIN, w_rows_are_outputs=True)
        lr = _mm(h, w_in_t[l, IN_MAIN:IN_MAIN + 2 * GLA_LR], None, 2 * GLA_LR, w_rows_are_outputs=True)
        su = _mm(h, w_in_t[l, IN_MAIN + 2 * GLA_LR:], None, S5_W, w_rows_are_outputs=True)
        gates = _mm(h, w_merge, l, 1024, act="sigmoid")

        o_na = jnp.concatenate([_ctx_attn(u),
                                _na_attn(u, cache_na_k, cache_na_v, _na_bias(na_rpb[l]), l)], axis=0)

        s0_lat = state_gla[:, l].transpose(0, 1, 2, 4, 3)
        og_ctx, sfin = _gla(u, lr, gla_wa, gla_ba, gla_norm_g, zero_gla, cos, sin, l, latent=False)
        og_lat, _ = _gla(u, lr, gla_wa, gla_ba, gla_norm_g, s0_lat, cos, sin, l, latent=True)
        o_gla = jnp.concatenate([og_ctx, og_lat], axis=0)

        ops, step = _s5_operators(s5_a_re[l], s5_a_im[l], s5_log_dt[l], s5_b_re[l], s5_b_im[l],
                                  s5_c_re[l], s5_c_im[l])
        st = state_s5[:, l].astype(F32)
        h0_lat = st.transpose(2, 1, 4, 0, 3).reshape(S5_GROUPS, 4, DEC_BATCH, S5_N)
        y_ctx, hfin = _s5(su, ops, step, zero_s5, nb=BATCH, seq=SEQ, first_row_block=0)
        y_lat, _ = _s5(su, ops, step, h0_lat, nb=DEC_BATCH, seq=DEC_SEQ, first_row_block=T_CTX // T_LAT)
        zz = _glu(jnp.concatenate([y_ctx, y_lat], axis=0), su, s5_d, s5_w_glu, l)

        merged = _merge(o_na, o_gla, zz, gates, w_br_na, w_br_gla, w_br_s5, l)
        x = _out_proj(merged, w_out, x, mod, l)

        h2 = _norm(x, norm2_g[l], mod, (3, 4), F32)
        idx_t, w_t = _router(h2, router_w[l].T, router_bias[l])
        sorted_tok, slot_of, block_e, src, n_used = _dispatch(idx_t.T)
        y_sorted = _experts(h2, sorted_tok, block_e, src, n_used, exp_wg, exp_wu, exp_wd, l)
        act = _shared_up(h2, sh_wg, sh_wu, l)
        x = _combine(y_sorted, slot_of, w_t.T, act, sh_wd, x, mod, l)

        new_k.append(u[:T_CTX, NA_W:2 * NA_W].reshape(BATCH, SEQ, NA_HEADS, NA_DH))
        new_v.append(u[:T_CTX, 2 * NA_W:3 * NA_W].reshape(BATCH, SEQ, NA_HEADS, NA_DH))
        new_gla.append(sfin.transpose(0, 1, 2, 4, 3))
        new_s5.append(hfin.reshape(S5_GROUPS, 2, 2, BATCH, S5_N).transpose(3, 1, 0, 4, 2))

    y = _norm(x, final_norm_g, None, None, F32)
    return (y[:T_CTX].reshape(BATCH, SEQ, D_MODEL), y[T_CTX:].reshape(DEC_BATCH, DEC_SEQ, D_MODEL),
            jnp.stack(new_k, axis=1), jnp.stack(new_v, axis=1),
            jnp.stack(new_gla, axis=1), jnp.stack(new_s5, axis=1))
```

```python
import functools
import math

import jax
import jax.numpy as jnp
from jax import lax
from jax.experimental import pallas as pl
from jax.experimental.pallas import tpu as pltpu

F32 = jnp.float32
BF16 = jnp.bfloat16

D_MODEL = 2048
BATCH = 16
SEQ = 256
DEPTH = 2
DEC_BATCH = 2
DEC_SEQ = 1024
PAST_LEN = 512
GRID_W = 64
NA_HEADS = 8
NA_DH = 128
NA_W = NA_HEADS * NA_DH
NA_ROWS = 8
NA_COLS = 16
GLA_HEADS = 4
GLA_DK = 128
GLA_DV = 256
GLA_QK_W = GLA_HEADS * GLA_DK
GLA_V_W = GLA_HEADS * GLA_DV
GLA_LR = 16
GLA_TAU = 16.0
GLA_CHUNK = 64
GLA_SUB = 16
GLA_UNROLL = 4
ROPE_BASE = 10000.0
S5_W = 1024
S5_GS = 16
S5_GROUPS = S5_W // S5_GS
S5_N = 64
S5_T = 16
IN_W = 3 * NA_W + 2 * GLA_QK_W + 2 * GLA_V_W + 2 * GLA_LR + S5_W
IN_MAIN = 3 * NA_W + 2 * GLA_QK_W + 2 * GLA_V_W
N_EXPERTS = 64
TOP_K = 8
N_GROUPS = 8
TOPK_GROUPS = 4
F_EXPERT = 512
F_SHARED = 512
ROUTED_SCALE = 2.5
EPS = 1e-6

T_CTX = BATCH * SEQ
T_LAT = DEC_BATCH * DEC_SEQ
T_ALL = T_CTX + T_LAT
N_MOD = 1 + DEC_BATCH
MOD_ROWS = 8

TM = 512
MOE_MB = 256
MOE_NB = T_ALL * TOP_K // MOE_MB + N_EXPERTS
MOE_SLOTS = MOE_NB * MOE_MB
CMB_TM = 128
NEG = -1e30

VMEM_LIMIT = 56 * 1024 * 1024

NT = (((1,), (1,)), ((), ()))
TN = (((0,), (0,)), ((), ()))


def _cparams(n_axes):
    return pltpu.CompilerParams(dimension_semantics=("arbitrary",) * n_axes,
                                vmem_limit_bytes=VMEM_LIMIT)


def _group_of_tile(i, tm):
    row = i * tm
    return jnp.where(row < T_CTX, 0, 1 + (row - T_CTX) // DEC_SEQ)


def _silu(x):
    return x * jax.nn.sigmoid(x)


def _gelu_tanh(x):
    return 0.5 * x * (1.0 + jnp.tanh(math.sqrt(2.0 / math.pi) * (x + 0.044715 * (x * x * x))))


def _ada_kernel(c_ref, w_ref, b_ref, o_ref):
    s = _silu(c_ref[...]).astype(BF16)
    o_ref[...] = jnp.dot(s, w_ref[...].astype(BF16), preferred_element_type=F32) + b_ref[...]


def _ada(cvec, ada_w, ada_b):
    tn = 1024
    return pl.pallas_call(
        _ada_kernel,
        grid=(DEPTH, 6 * D_MODEL // tn),
        in_specs=[pl.BlockSpec((MOD_ROWS, D_MODEL), lambda l, j: (0, 0)),
                  pl.BlockSpec((None, D_MODEL, tn), lambda l, j: (l, 0, j)),
                  pl.BlockSpec((None, 1, tn), lambda l, j: (l, 0, j))],
        out_specs=pl.BlockSpec((None, MOD_ROWS, tn), lambda l, j: (l, 0, j)),
        out_shape=jax.ShapeDtypeStruct((DEPTH, MOD_ROWS, 6 * D_MODEL), F32),
        compiler_params=_cparams(2), name="ada",
    )(cvec, ada_w, ada_b.reshape(DEPTH, 1, 6 * D_MODEL))


def _pack_bf16_pair(lo, hi):
    bits = lambda v: pltpu.bitcast(v.astype(BF16).astype(F32), jnp.uint32)
    return (bits(lo) >> 16) | (bits(hi) & jnp.uint32(0xFFFF0000))


def _unpack_bf16_pair(w):
    return pltpu.bitcast(w << 16, F32), pltpu.bitcast(w & jnp.uint32(0xFFFF0000), F32)


def _norm_kernel(x_ref, g_ref, *rest, rows, packed):
    x = x_ref[...]
    y = x * lax.rsqrt(jnp.mean(x * x, axis=-1, keepdims=True) + EPS) * g_ref[...]
    if rows is not None:
        mod_ref = rest[0]
        y = y * (1.0 + mod_ref[rows[1]:rows[1] + 1, :]) + mod_ref[rows[0]:rows[0] + 1, :]
    if packed:
        rest[-2][...] = y
        rest[-1][...] = _pack_bf16_pair(y[:, :D_MODEL // 2], y[:, D_MODEL // 2:])
    else:
        rest[-1][...] = y.astype(rest[-1].dtype)


def _norm(x, g, mod, rows, out_dtype, packed=False):
    tm = 256
    in_specs = [pl.BlockSpec((tm, D_MODEL), lambda i: (i, 0)),
                pl.BlockSpec((1, D_MODEL), lambda i: (0, 0))]
    args = [x, g.reshape(1, D_MODEL)]
    if rows is not None:
        in_specs.append(pl.BlockSpec((None, 6, D_MODEL), lambda i: (_group_of_tile(i, tm), 0, 0)))
        args.append(mod)
    out_specs = pl.BlockSpec((tm, D_MODEL), lambda i: (i, 0))
    out_shape = jax.ShapeDtypeStruct((T_ALL, D_MODEL), out_dtype)
    if packed:
        out_specs = [out_specs, pl.BlockSpec((tm, D_MODEL // 2), lambda i: (i, 0))]
        out_shape = [out_shape, jax.ShapeDtypeStruct((T_ALL, D_MODEL // 2), jnp.uint32)]
    return pl.pallas_call(
        functools.partial(_norm_kernel, rows=rows, packed=packed),
        grid=(T_ALL // tm,),
        in_specs=in_specs,
        out_specs=out_specs,
        out_shape=out_shape,
        compiler_params=_cparams(1), name="norm",
    )(*args)


def _mm_kernel(x_ref, w_ref, o_ref, wbf_ref, *, act, w_rows_are_outputs):
    @pl.when(pl.program_id(1) == 0)
    def _():
        wbf_ref[...] = w_ref[...].astype(BF16)

    x = x_ref[...].astype(BF16)
    if w_rows_are_outputs:
        acc = lax.dot_general(x, wbf_ref[...], NT, preferred_element_type=F32)
    else:
        acc = jnp.dot(x, wbf_ref[...], preferred_element_type=F32)
    if act == "sigmoid":
        acc = jax.nn.sigmoid(acc)
    o_ref[...] = acc.astype(o_ref.dtype)


def _mm(x, w, layer, tn, act=None, out_dtype=F32, n=None, w_rows_are_outputs=False):
    k = x.shape[1]
    if n is None:
        n = w.shape[-2] if w_rows_are_outputs else w.shape[-1]
    blk, idx = ((tn, k), lambda j: (j, 0)) if w_rows_are_outputs else ((k, tn), lambda j: (0, j))
    if layer is None:
        w_spec = pl.BlockSpec(blk, lambda j, i: idx(j))
    else:
        w_spec = pl.BlockSpec((None,) + blk, lambda j, i: (layer,) + idx(j))
    return pl.pallas_call(
        functools.partial(_mm_kernel, act=act, w_rows_are_outputs=w_rows_are_outputs),
        grid=(n // tn, T_ALL // TM),
        in_specs=[pl.BlockSpec((TM, k), lambda j, i: (i, 0)), w_spec],
        out_specs=pl.BlockSpec((TM, tn), lambda j, i: (i, j)),
        out_shape=jax.ShapeDtypeStruct((T_ALL, n), out_dtype),
        scratch_shapes=[pltpu.VMEM(blk, BF16)],
        compiler_params=_cparams(2), name="mm",
    )(x, w)


def _ctx_attn_kernel(q_ref, k_ref, v_ref, o_ref):
    scale = NA_DH ** -0.5
    for h in range(NA_HEADS):
        sl = slice(h * NA_DH, (h + 1) * NA_DH)
        q = q_ref[:, sl].astype(BF16)
        k = k_ref[:, sl].astype(BF16)
        v = v_ref[:, sl].astype(BF16)
        s = lax.dot_general(q, k, NT, preferred_element_type=F32) * scale
        p = jnp.exp(s - jnp.max(s, axis=-1, keepdims=True))
        o = jnp.dot(p.astype(BF16), v, preferred_element_type=F32) / jnp.sum(p, axis=-1, keepdims=True)
        o_ref[:, sl] = o.astype(o_ref.dtype)


def _ctx_attn(u):
    spec = lambda cb: pl.BlockSpec((SEQ, NA_W), lambda b: (b, cb))
    return pl.pallas_call(
        _ctx_attn_kernel,
        grid=(BATCH,),
        in_specs=[spec(0), spec(1), spec(2)],
        out_specs=pl.BlockSpec((SEQ, NA_W), lambda b: (b, 0)),
        out_shape=jax.ShapeDtypeStruct((T_CTX, NA_W), BF16),
        compiler_params=_cparams(1), name="ctx_attn",
    )(u, u, u)


NA_GRID_ROWS = DEC_SEQ // GRID_W
NA_KR = min(NA_ROWS, NA_GRID_ROWS)
NA_LOC = NA_KR * GRID_W


def _na_bias(rpb):
    n_off = 2 * NA_ROWS - 1
    pad = GRID_W - NA_COLS
    period = 2 * GRID_W
    p = jnp.pad(rpb, ((0, 0), (0, 0), (pad, period - pad - (2 * NA_COLS - 1))))
    hank = jnp.tile(p, (1, 1, GRID_W + 1))[..., :GRID_W * (period + 1)]
    hank = hank.reshape(NA_HEADS, n_off, GRID_W, period + 1)[..., :GRID_W]
    band = hank[:, :, ::-1, :]
    qc = jnp.arange(GRID_W)[:, None]
    kc = jnp.arange(GRID_W)[None, :]
    win = jnp.clip(qc - NA_COLS // 2, 0, GRID_W - NA_COLS)
    valid = (kc >= win) & (kc < win + NA_COLS)
    band = jnp.where(valid[None, None], band, NEG)
    return band.transpose(0, 2, 1, 3).reshape(NA_HEADS, GRID_W, n_off * GRID_W)


def _na_kernel(q_ref, k_ref, v_ref, ck_ref, cv_ref, bias_ref, o_ref):
    scale = NA_DH ** -0.5
    kb = k_ref[...].astype(BF16)
    vb = v_ref[...].astype(BF16)
    ck = ck_ref[...].astype(BF16)
    cv = cv_ref[...].astype(BF16)
    for r in range(NA_GRID_ROWS):
        first = min(max(r - NA_KR // 2, 0), NA_GRID_ROWS - NA_KR)
        off = first - r + NA_ROWS - 1
        rows = slice(r * GRID_W, (r + 1) * GRID_W)
        keys = slice(first * GRID_W, first * GRID_W + NA_LOC)
        q = q_ref[rows, :].astype(BF16)
        s1 = (lax.dot_general(q, kb[keys, :], NT, preferred_element_type=F32) * scale
              + bias_ref[:, off * GRID_W:off * GRID_W + NA_LOC])
        s2 = lax.dot_general(q, ck, NT, preferred_element_type=F32) * scale
        m = jnp.maximum(jnp.max(s1, axis=-1, keepdims=True), jnp.max(s2, axis=-1, keepdims=True))
        p1 = jnp.exp(s1 - m)
        p2 = jnp.exp(s2 - m)
        den = jnp.sum(p1, axis=-1, keepdims=True) + jnp.sum(p2, axis=-1, keepdims=True)
        o = (jnp.dot(p1.astype(BF16), vb[keys, :], preferred_element_type=F32)
             + jnp.dot(p2.astype(BF16), cv, preferred_element_type=F32))
        o_ref[rows, :] = (o / den).astype(o_ref.dtype)


def _na_attn(u, cache_k, cache_v, bias, layer):
    lat_sb = T_CTX // DEC_SEQ
    ck = cache_k.reshape(DEC_BATCH, DEPTH, PAST_LEN, NA_W)
    cv = cache_v.reshape(DEC_BATCH, DEPTH, PAST_LEN, NA_W)
    qkv_spec = lambda cb: pl.BlockSpec((DEC_SEQ, NA_DH), lambda b, h: (lat_sb + b, cb * NA_HEADS + h))
    c_spec = pl.BlockSpec((None, None, PAST_LEN, NA_DH), lambda b, h: (b, layer, 0, h))
    return pl.pallas_call(
        _na_kernel,
        grid=(DEC_BATCH, NA_HEADS),
        in_specs=[qkv_spec(0), qkv_spec(1), qkv_spec(2), c_spec, c_spec,
                  pl.BlockSpec((None, GRID_W, (2 * NA_ROWS - 1) * GRID_W), lambda b, h: (h, 0, 0))],
        out_specs=pl.BlockSpec((DEC_SEQ, NA_DH), lambda b, h: (b, h)),
        out_shape=jax.ShapeDtypeStruct((T_LAT, NA_W), BF16),
        compiler_params=_cparams(2), name="na_attn",
    )(u, u, u, ck, cv, bias)


def _rope_tables():
    half = GLA_DK // 2
    nf = half // 2
    t = jnp.arange(DEC_SEQ)
    freqs = ROPE_BASE ** (-jnp.arange(nf, dtype=F32) / nf)
    ang_r = (t // GRID_W).astype(F32)[:, None] * freqs
    ang_c = (t % GRID_W).astype(F32)[:, None] * freqs
    cos = jnp.concatenate([jnp.cos(ang_r), jnp.cos(ang_r), jnp.cos(ang_c), jnp.cos(ang_c)], axis=-1)
    sin = jnp.concatenate([-jnp.sin(ang_r), jnp.sin(ang_r), -jnp.sin(ang_c), jnp.sin(ang_c)], axis=-1)
    return cos, sin


def _split_bf16(x):
    hi = x.astype(BF16)
    return hi, (x - hi.astype(F32)).astype(BF16)


def _gla_kernel(q_ref, k_ref, v_ref, gg_ref, lr_ref, wa_ref, ba_ref, ng_ref, cos_ref, sin_ref, s0_ref,
                o_ref, sfin_ref, qs, ks, las, o_acc, st, *, seq, rope):
    nc = seq // GLA_CHUNK
    nq = GLA_DK // 4

    def rot(x):
        lane = lax.broadcasted_iota(jnp.int32, x.shape, 1)
        partner = jnp.where((lane % (2 * nq)) < nq,
                            pltpu.roll(x, GLA_DK - nq, axis=1), pltpu.roll(x, nq, axis=1))
        return x * cos_ref[...] + partner * sin_ref[...]

    q = q_ref[...] * (GLA_DK ** -0.5)
    k = k_ref[...]
    if rope:
        q = rot(q)
        k = rot(k)
    qs[...] = q
    ks[...] = k
    lr = lr_ref[...]
    for d in range(2):
        z = jnp.dot(lr[:, d * GLA_LR:(d + 1) * GLA_LR].astype(BF16), wa_ref[d].astype(BF16),
                    preferred_element_type=F32) + ba_ref[d:d + 1, :]
        las[d] = -(jnp.maximum(-z, 0.0) + jnp.log1p(jnp.exp(-jnp.abs(z)))) / GLA_TAU
        st[d] = s0_ref[d]

    row = lax.broadcasted_iota(jnp.int32, (GLA_CHUNK, GLA_CHUNK), 0)
    col = lax.broadcasted_iota(jnp.int32, (GLA_CHUNK, GLA_CHUNK), 1)
    key_row = lax.broadcasted_iota(jnp.int32, (GLA_CHUNK, 1), 0)

    def chunk(c, d):
        rev = d == 1
        rows = pl.ds(pl.multiple_of(c * GLA_CHUNK, GLA_CHUNK), GLA_CHUNK)
        qc = qs[rows, :]
        kc = ks[rows, :]
        vc = v_ref[rows, :].astype(BF16)
        la = las[d, rows, :]
        causal = (col >= row) if rev else (col <= row)
        tri = jnp.where(causal, 1.0, 0.0).astype(BF16)
        la_hi, la_lo = _split_bf16(la)
        b = (jnp.dot(tri, la_hi, preferred_element_type=F32)
             + jnp.dot(tri, la_lo, preferred_element_type=F32))
        bex = b - la
        b_last = b[0:1, :] if rev else b[GLA_CHUNK - 1:GLA_CHUNK, :]
        blocks = []
        for i in range(GLA_CHUNK // GLA_SUB):
            lo, hi = i * GLA_SUB, (i + 1) * GLA_SUB
            ref = bex[hi - 1:hi, :] if rev else bex[lo:lo + 1, :]
            qt = (qc[lo:hi, :] * jnp.exp(b[lo:hi, :] - ref)).astype(BF16)
            allowed = (key_row >= lo) if rev else (key_row < hi)
            kt = (kc * jnp.exp(jnp.where(allowed, ref - b, -jnp.inf))).astype(BF16)
            blocks.append(lax.dot_general(qt, kt, NT, preferred_element_type=F32))
        att = jnp.where(causal, jnp.concatenate(blocks, axis=0), 0.0)
        s_t = st[d]
        o = (jnp.dot(att.astype(BF16), vc, preferred_element_type=F32)
             + lax.dot_general((qc * jnp.exp(b)).astype(BF16), s_t.astype(BF16), NT,
                               preferred_element_type=F32))
        khat = (kc * jnp.exp(b_last - b)).astype(BF16)
        st[d] = s_t * jnp.exp(b_last) + lax.dot_general(vc, khat, TN, preferred_element_type=F32)
        o_acc[d, rows, :] = o

    def both(c, carry):
        chunk(c, 0)
        chunk(nc - 1 - c, 1)
        return carry

    lax.fori_loop(0, nc, both, 0, unroll=GLA_UNROLL)
    sfin_ref[...] = st[...]
    o = o_acc[0] + o_acc[1]
    o = o * lax.rsqrt(jnp.mean(o * o, axis=-1, keepdims=True) + EPS) * ng_ref[...]
    o_ref[...] = (o * _silu(gg_ref[...])).astype(o_ref.dtype)


def _gla(u, lr, wa, ba, norm_g, s0_t, cos, sin, layer, *, latent):
    seq, nb, first = (DEC_SEQ, DEC_BATCH, T_CTX // DEC_SEQ) if latent else (SEQ, BATCH, 0)
    qk_cb = 3 * NA_W // GLA_DK
    v_cb = (3 * NA_W + 2 * GLA_QK_W) // GLA_DV
    row = lambda w, cb: pl.BlockSpec((seq, w), lambda b, h: (first + b, cb + h))
    return pl.pallas_call(
        functools.partial(_gla_kernel, seq=seq, rope=latent),
        grid=(nb, GLA_HEADS),
        in_specs=[row(GLA_DK, qk_cb), row(GLA_DK, qk_cb + GLA_HEADS), row(GLA_DV, v_cb),
                  row(GLA_DV, v_cb + GLA_HEADS),
                  pl.BlockSpec((seq, 2 * GLA_LR), lambda b, h: (first + b, 0)),
                  pl.BlockSpec((None, 2, GLA_LR, GLA_DK), lambda b, h: (layer, 0, 0, h)),
                  pl.BlockSpec((None, 2, GLA_DK), lambda b, h: (layer, 0, h)),
                  pl.BlockSpec((None, 1, GLA_DV), lambda b, h: (layer, 0, h)),
                  pl.BlockSpec((seq, GLA_DK), lambda b, h: (0, 0)),
                  pl.BlockSpec((seq, GLA_DK), lambda b, h: (0, 0)),
                  pl.BlockSpec((None, 2, None, GLA_DV, GLA_DK), lambda b, h: (b, 0, h, 0, 0))],
        out_specs=[pl.BlockSpec((seq, GLA_DV), lambda b, h: (b, h)),
                   pl.BlockSpec((None, 2, None, GLA_DV, GLA_DK), lambda b, h: (b, 0, h, 0, 0))],
        out_shape=[jax.ShapeDtypeStruct((nb * seq, GLA_V_W), BF16),
                   jax.ShapeDtypeStruct((nb, 2, GLA_HEADS, GLA_DV, GLA_DK), F32)],
        scratch_shapes=[pltpu.VMEM((seq, GLA_DK), F32), pltpu.VMEM((seq, GLA_DK), F32),
                        pltpu.VMEM((2, seq, GLA_DK), F32), pltpu.VMEM((2, seq, GLA_DV), F32),
                        pltpu.VMEM((2, GLA_DV, GLA_DK), F32)],
        compiler_params=_cparams(2), name="gla",
    )(u, u, u, u, lr, wa, ba, norm_g.reshape(DEPTH, 1, GLA_V_W), cos[:seq], sin[:seq], s0_t)


def _s5_operators(a_re, a_im, log_dt, b_re, b_im, c_re, c_im):
    hp = lax.Precision.HIGHEST
    lam = lax.complex(a_re.astype(F32), a_im.astype(F32))
    lam_dt = lam * jnp.exp(log_dt.astype(F32))[..., None]
    a_bar = jnp.exp(lam_dt)
    b_bar = ((a_bar - 1.0) / lam)[..., None] * lax.complex(b_re.astype(F32), b_im.astype(F32))
    cc = lax.complex(c_re.astype(F32), c_im.astype(F32))
    taus = jnp.arange(S5_T + 1, dtype=F32)
    pw = jnp.exp(lam_dt[:, :, None, :] * taus[None, None, :, None])
    lanes = S5_T * S5_GS

    def out_weights(pwd, cd):
        m = pwd.transpose(0, 2, 1)[:, :, :, None] * cd.transpose(0, 2, 1)[:, :, None, :]
        return m.reshape(S5_GROUPS, S5_N, lanes)

    def lag_response(d):
        r = out_weights(pw[d, :, :S5_T], cc[d])
        bt = b_bar[d].transpose(0, 2, 1)
        return (jnp.einsum("gin,gnx->gix", jnp.real(bt), jnp.real(r), precision=hp)
                - jnp.einsum("gin,gnx->gix", jnp.imag(bt), jnp.imag(r), precision=hp))

    def toeplitz(lags):
        rows = [lags[:, :, (S5_T - 1 - s) * S5_GS:(2 * S5_T - 1 - s) * S5_GS] for s in range(S5_T)]
        return jnp.stack(rows, axis=1).reshape(S5_GROUPS, lanes, lanes)

    no_lag = jnp.zeros((S5_GROUPS, S5_GS, (S5_T - 1) * S5_GS), F32)
    rev_lags = lag_response(1).reshape(S5_GROUPS, S5_GS, S5_T, S5_GS)[:, :, ::-1].reshape(S5_GROUPS, S5_GS, lanes)
    toep_f = toeplitz(jnp.concatenate([no_lag, lag_response(0)], axis=-1))
    toep_b = toeplitz(jnp.concatenate([rev_lags, no_lag], axis=-1))

    def state_in(pwd, bd):
        m = pwd[:, :, None, :] * bd.transpose(0, 2, 1)[:, None, :, :]
        m = m.reshape(S5_GROUPS, lanes, S5_N)
        return [jnp.real(m), jnp.imag(m)]

    p = jnp.stack(state_in(pw[0, :, S5_T - 1::-1], b_bar[0])
                  + state_in(pw[1, :, :S5_T], b_bar[1]), axis=1)

    def state_out(pwd, cd):
        m = out_weights(pwd, cd)
        return [jnp.real(m), -jnp.imag(m)]

    q = jnp.stack(state_out(pw[0, :, 1:S5_T + 1], cc[0])
                  + state_out(pw[1, :, S5_T:0:-1], cc[1]), axis=1)
    ops = [m.astype(BF16) for m in (toep_f, toep_b, p, q)]
    a_t = pw[:, :, S5_T]
    step = jnp.stack([jnp.real(a_t[0]), jnp.imag(a_t[0]), jnp.real(a_t[1]), jnp.imag(a_t[1])], axis=1)
    return ops, step


S5_GB = 128 // S5_GS
S5_SEG = 128 // S5_GS


def _s5_kernel(x_ref, tf_ref, tb_ref, p_ref, q_ref, step_ref, h0_ref, y_ref, hfin_ref,
               u_scr, y_scr, e_scr, hin_scr, *, nc, nb):
    n_chunks = nb * nc
    for s in range(S5_T):
        xs = x_ref[pl.ds(s, n_chunks, stride=S5_T), :]
        half, pos = divmod(s, S5_SEG)
        dst = slice(half * 128 + pos * S5_GS, half * 128 + (pos + 1) * S5_GS)
        for g in range(S5_GB):
            moved = pltpu.roll(xs, ((pos - g) % S5_SEG) * S5_GS, axis=1)
            u_scr[g, :, dst] = moved[:, pos * S5_GS:(pos + 1) * S5_GS].astype(BF16)

    def group(g, carry):
        u = u_scr[g]
        for m in range(4):
            e_scr[m] = jnp.dot(u, p_ref[g, m], preferred_element_type=F32)
        ar = [step_ref[g, 0:1, :], step_ref[g, 2:3, :]]
        ai = [step_ref[g, 1:2, :], step_ref[g, 3:4, :]]
        hr = [h0_ref[g, 0], h0_ref[g, 2]]
        hi = [h0_ref[g, 1], h0_ref[g, 3]]
        for c in range(nc):
            for d in range(2):
                rows = pl.ds(c if d == 0 else nc - 1 - c, nb, stride=nc)
                hin_scr[2 * d, rows, :] = hr[d]
                hin_scr[2 * d + 1, rows, :] = hi[d]
                hr[d], hi[d] = (hr[d] * ar[d] - hi[d] * ai[d] + e_scr[2 * d, rows, :],
                                hi[d] * ar[d] + hr[d] * ai[d] + e_scr[2 * d + 1, rows, :])
        for d in range(2):
            hfin_ref[g, 2 * d] = hr[d]
            hfin_ref[g, 2 * d + 1] = hi[d]
        y = (jnp.dot(u, tf_ref[g], preferred_element_type=F32)
             + jnp.dot(u, tb_ref[g], preferred_element_type=F32))
        for m in range(4):
            y += jnp.dot(hin_scr[m].astype(BF16), q_ref[g, m], preferred_element_type=F32)
        y_scr[g] = y
        return carry

    lax.fori_loop(0, S5_GB, group, 0)

    segment = lax.broadcasted_iota(jnp.int32, (n_chunks, 128), 1) // S5_GS
    for t in range(S5_T):
        half, pos = divmod(t, S5_SEG)
        rows = jnp.zeros((n_chunks, 128), F32)
        for g in range(S5_GB):
            moved = pltpu.roll(y_scr[g, :, half * 128:(half + 1) * 128], ((g - pos) % S5_SEG) * S5_GS, axis=1)
            rows = jnp.where(segment == g, moved, rows)
        y_ref[pl.ds(t, n_chunks, stride=S5_T), :] = rows


def _s5(su, ops, step, h0, *, nb, seq, first_row_block):
    nc = seq // S5_T
    lanes = S5_T * S5_GS
    n_chunks = nb * nc
    gb = lambda *shape: pl.BlockSpec((S5_GB,) + shape, lambda g: (g,) + (0,) * len(shape))
    return pl.pallas_call(
        functools.partial(_s5_kernel, nc=nc, nb=nb),
        grid=(S5_GROUPS // S5_GB,),
        in_specs=[pl.BlockSpec((nb * seq, S5_GB * S5_GS), lambda g: (first_row_block, g)),
                  gb(lanes, lanes), gb(lanes, lanes), gb(4, lanes, S5_N), gb(4, S5_N, lanes),
                  gb(4, S5_N), gb(4, nb, S5_N)],
        out_specs=[pl.BlockSpec((nb * seq, S5_GB * S5_GS), lambda g: (0, g)), gb(4, nb, S5_N)],
        out_shape=[jax.ShapeDtypeStruct((nb * seq, S5_W), F32),
                   jax.ShapeDtypeStruct((S5_GROUPS, 4, nb, S5_N), F32)],
        scratch_shapes=[pltpu.VMEM((S5_GB, n_chunks, lanes), BF16), pltpu.VMEM((S5_GB, n_chunks, lanes), F32),
                        pltpu.VMEM((4, n_chunks, S5_N), F32), pltpu.VMEM((4, n_chunks, S5_N), F32)],
        compiler_params=_cparams(1), name="s5",
    )(su, *ops, step, h0)


def _glu_kernel(y_ref, su_ref, d_ref, w_ref, o_ref, wbf_ref):
    @pl.when(pl.program_id(0) == 0)
    def _():
        wbf_ref[...] = w_ref[...].astype(BF16)

    z = _gelu_tanh(y_ref[...] + d_ref[...] * su_ref[...])
    gate = jax.nn.sigmoid(jnp.dot(z.astype(BF16), wbf_ref[...], preferred_element_type=F32))
    o_ref[...] = (z * gate).astype(o_ref.dtype)


def _glu(y, su, d, w_glu, layer):
    row = pl.BlockSpec((TM, S5_W), lambda i: (i, 0))
    return pl.pallas_call(
        _glu_kernel,
        grid=(T_ALL // TM,),
        in_specs=[row, row, pl.BlockSpec((None, 1, S5_W), lambda i: (layer, 0, 0)),
                  pl.BlockSpec((None, S5_W, S5_W), lambda i: (layer, 0, 0))],
        out_specs=row,
        out_shape=jax.ShapeDtypeStruct((T_ALL, S5_W), BF16),
        scratch_shapes=[pltpu.VMEM((S5_W, S5_W), BF16)],
        compiler_params=_cparams(1), name="glu",
    )(y, su, d.reshape(DEPTH, 1, S5_W), w_glu)


def _merge_kernel(a_ref, b_ref, c_ref, ga_ref, gb_ref, gc_ref, wa_ref, wb_ref, wc_ref, o_ref, wbf_ref):
    @pl.when(pl.program_id(1) == 0)
    def _():
        wbf_ref[0] = wa_ref[...].astype(BF16)
        wbf_ref[1] = wb_ref[...].astype(BF16)
        wbf_ref[2] = wc_ref[...].astype(BF16)

    acc = ga_ref[...] * jnp.dot(a_ref[...], wbf_ref[0], preferred_element_type=F32)
    acc += gb_ref[...] * jnp.dot(b_ref[...], wbf_ref[1], preferred_element_type=F32)
    acc += gc_ref[...] * jnp.dot(c_ref[...], wbf_ref[2], preferred_element_type=F32)
    o_ref[...] = acc.astype(o_ref.dtype)


def _merge(o_na, o_gla, zz, gates, w_na, w_gla, w_s5, layer):
    tn = 1024
    ncb = D_MODEL // tn
    act = pl.BlockSpec((TM, NA_W), lambda j, i: (i, 0))
    gate = lambda k: pl.BlockSpec((TM, tn), lambda j, i: (i, k * ncb + j))
    w = pl.BlockSpec((None, NA_W, tn), lambda j, i: (layer, 0, j))
    return pl.pallas_call(
        _merge_kernel,
        grid=(ncb, T_ALL // TM),
        in_specs=[act, act, act, gate(0), gate(1), gate(2), w, w, w],
        out_specs=pl.BlockSpec((TM, tn), lambda j, i: (i, j)),
        out_shape=jax.ShapeDtypeStruct((T_ALL, D_MODEL), BF16),
        scratch_shapes=[pltpu.VMEM((3, NA_W, tn), BF16)],
        compiler_params=_cparams(2), name="merge",
    )(o_na, o_gla, zz, gates, gates, gates, w_na, w_gla, w_s5)


def _out_proj_kernel(m_ref, w_ref, x_ref, mod_ref, o_ref, wbf_ref, *, gate_row):
    @pl.when(pl.program_id(1) == 0)
    def _():
        wbf_ref[...] = w_ref[...].astype(BF16)

    y = jnp.dot(m_ref[...], wbf_ref[...], preferred_element_type=F32)
    o_ref[...] = x_ref[...] + mod_ref[gate_row:gate_row + 1, :] * y


def _out_proj(merged, w_out, x, mod, layer):
    tn = 1024
    return pl.pallas_call(
        functools.partial(_out_proj_kernel, gate_row=2),
        grid=(D_MODEL // tn, T_ALL // TM),
        in_specs=[pl.BlockSpec((TM, D_MODEL), lambda j, i: (i, 0)),
                  pl.BlockSpec((None, D_MODEL, tn), lambda j, i: (layer, 0, j)),
                  pl.BlockSpec((TM, tn), lambda j, i: (i, j)),
                  pl.BlockSpec((None, 6, tn), lambda j, i: (_group_of_tile(i, TM), 0, j))],
        out_specs=pl.BlockSpec((TM, tn), lambda j, i: (i, j)),
        out_shape=jax.ShapeDtypeStruct((T_ALL, D_MODEL), F32),
        scratch_shapes=[pltpu.VMEM((D_MODEL, tn), BF16)],
        compiler_params=_cparams(2), name="out_proj",
    )(merged, w_out, x, mod)


def _first_max(v, iota, n):
    m = jnp.max(v, axis=0, keepdims=True)
    first = jnp.min(jnp.where(v == m, iota, float(n)), axis=0, keepdims=True)
    return m, first


def _router_kernel(x_ref, wt_ref, bias_ref, idx_ref, w_ref):
    per = N_EXPERTS // N_GROUPS
    x_hi, x_lo = _split_bf16(x_ref[...])
    w_hi, w_lo = _split_bf16(wt_ref[...])
    logits = (lax.dot_general(w_hi, x_hi, NT, preferred_element_type=F32)
              + lax.dot_general(w_hi, x_lo, NT, preferred_element_type=F32)
              + lax.dot_general(w_lo, x_hi, NT, preferred_element_type=F32))
    scores = jax.nn.sigmoid(logits)
    sel = scores + bias_ref[...]
    tm = sel.shape[1]
    iota_g = lax.broadcasted_iota(jnp.int32, (per, tm), 0).astype(F32)
    grp_rows = []
    for g in range(N_GROUPS):
        v = sel[g * per:(g + 1) * per, :]
        m1, first = _first_max(v, iota_g, per)
        m2 = jnp.max(jnp.where(iota_g == first, -jnp.inf, v), axis=0, keepdims=True)
        grp_rows.append(m1 + m2)
    grp = jnp.concatenate(grp_rows, axis=0)
    iota_n = lax.broadcasted_iota(jnp.int32, (N_GROUPS, tm), 0).astype(F32)
    chosen = jnp.zeros((N_GROUPS, tm), F32)
    for _ in range(TOPK_GROUPS):
        _, first = _first_max(grp, iota_n, N_GROUPS)
        hit = iota_n == first
        chosen = jnp.where(hit, 1.0, chosen)
        grp = jnp.where(hit, -jnp.inf, grp)
    mask = jnp.concatenate([jnp.broadcast_to(chosen[g:g + 1, :], (per, tm)) for g in range(N_GROUPS)], axis=0)
    sel = jnp.where(mask > 0.5, sel, -jnp.inf)
    iota_e = lax.broadcasted_iota(jnp.int32, (N_EXPERTS, tm), 0).astype(F32)
    ids, ws = [], []
    for _ in range(TOP_K):
        _, first = _first_max(sel, iota_e, N_EXPERTS)
        hit = iota_e == first
        ids.append(first)
        ws.append(jnp.sum(jnp.where(hit, scores, 0.0), axis=0, keepdims=True))
        sel = jnp.where(hit, -jnp.inf, sel)
    w = jnp.concatenate(ws, axis=0)
    idx_ref[...] = jnp.concatenate(ids, axis=0).astype(jnp.int32)
    w_ref[...] = w / jnp.sum(w, axis=0, keepdims=True) * ROUTED_SCALE


def _router(h, router_w_t, router_bias):
    tm = 256
    return pl.pallas_call(
        _router_kernel,
        grid=(T_ALL // tm,),
        in_specs=[pl.BlockSpec((tm, D_MODEL), lambda i: (i, 0)),
                  pl.BlockSpec((N_EXPERTS, D_MODEL), lambda i: (0, 0)),
                  pl.BlockSpec((N_EXPERTS, 1), lambda i: (0, 0))],
        out_specs=[pl.BlockSpec((TOP_K, tm), lambda i: (0, i)), pl.BlockSpec((TOP_K, tm), lambda i: (0, i))],
        out_shape=[jax.ShapeDtypeStruct((TOP_K, T_ALL), jnp.int32), jax.ShapeDtypeStruct((TOP_K, T_ALL), F32)],
        compiler_params=_cparams(1), name="router",
    )(h, router_w_t, router_bias.reshape(N_EXPERTS, 1))


def _dispatch(idx):
    n_assign = T_ALL * TOP_K
    flat_e = idx.reshape(-1)
    experts = jnp.arange(N_EXPERTS, dtype=jnp.int32)
    sorted_e, order = lax.sort((flat_e, jnp.arange(n_assign, dtype=jnp.int32)), num_keys=1, is_stable=True)
    counts = jnp.sum((flat_e[:, None] == experts[None, :]).astype(jnp.int32), axis=0)
    cnt_start = jnp.cumsum(counts) - counts
    padded = (counts + MOE_MB - 1) // MOE_MB * MOE_MB
    pad_end = jnp.cumsum(padded)
    pad_start = pad_end - padded
    onehot = (sorted_e[:, None] == experts[None, :]).astype(jnp.int32)
    slot_sorted = jnp.arange(n_assign, dtype=jnp.int32) + jnp.sum(onehot * (pad_start - cnt_start)[None, :], axis=1)
    _, slot_of = lax.sort((order, slot_sorted), num_keys=1)
    blk_first = jnp.arange(MOE_NB, dtype=jnp.int32) * MOE_MB
    block_e = jnp.minimum(jnp.sum((pad_end[None, :] <= blk_first[:, None]).astype(jnp.int32), axis=1),
                          N_EXPERTS - 1)
    blk_onehot = (block_e[:, None] == experts[None, :]).astype(jnp.int32)
    pick = lambda v: jnp.sum(blk_onehot * v[None, :], axis=1)
    src = jnp.clip(pick(cnt_start) + blk_first - pick(pad_start), 0, n_assign)
    sorted_tok = jnp.pad(order // TOP_K, (0, MOE_MB))
    n_used = (pad_end[-1] // MOE_MB).astype(jnp.int32).reshape(1)
    return sorted_tok, slot_of.reshape(T_ALL, TOP_K), block_e.astype(jnp.int32), src.astype(jnp.int32), n_used


MOE_PACKED_W = D_MODEL // 2


def _expert_kernel(be_ref, nu_ref, src_ref, stok_ref, x_hbm, wg_ref, wu_ref, wd_ref, o_ref,
                   xbuf, x_bf, wg_bf, wu_bf, wd_bf, sems):
    i = pl.program_id(0)
    n_used = nu_ref[0]
    last = MOE_NB - 1
    half = MOE_PACKED_W

    def row_copy(blk, slot, j):
        tok = stok_ref[src_ref[blk] + j]
        return pltpu.make_async_copy(x_hbm.at[pl.ds(tok, 1), :], xbuf.at[slot, pl.ds(j, 1), :], sems.at[slot])

    def wait_block(slot):
        pltpu.make_async_copy(x_hbm.at[pl.ds(0, MOE_MB), :], xbuf.at[slot], sems.at[slot]).wait()

    @pl.when(i == 0)
    def _():
        def body(j, carry):
            row_copy(0, 0, j).start()
            return carry
        lax.fori_loop(0, MOE_MB, body, 0, unroll=8)

    def compute(prefetch):
        @pl.when((i == 0) | (be_ref[i] != be_ref[jnp.maximum(i - 1, 0)]))
        def _():
            wg_bf[...] = wg_ref[...].astype(BF16)
            wu_bf[...] = wu_ref[...].astype(BF16)
            wd_bf[...] = wd_ref[...].astype(BF16)

        slot = i % 2
        wait_block(slot)
        x_lo, x_hi = _unpack_bf16_pair(xbuf[slot])
        x_bf[0] = x_lo.astype(BF16)
        x_bf[1] = x_hi.astype(BF16)
        if prefetch:
            for j in range(MOE_MB):
                row_copy(i + 1, 1 - slot, j).start()
        up = lambda w: (jnp.dot(x_bf[0], w[:half, :], preferred_element_type=F32)
                        + jnp.dot(x_bf[1], w[half:, :], preferred_element_type=F32))
        a = (_silu(up(wg_bf)) * up(wu_bf)).astype(BF16)
        y = jnp.dot(a, wd_bf[...], preferred_element_type=F32)
        o_ref[...] = _pack_bf16_pair(y[:, :half], y[:, half:])

    @pl.when((i < n_used) & (i < last))
    def _():
        compute(True)

    @pl.when((i < n_used) & (i == last))
    def _():
        compute(False)

    @pl.when(i >= n_used)
    def _():
        o_ref[...] = jnp.zeros_like(o_ref)

        @pl.when(i == n_used)
        def _():
            wait_block(i % 2)


def _experts(x_packed, sorted_tok, block_e, src, n_used, wg, wu, wd, layer):
    w_idx = lambda i, be, nu, src, stok: (layer, be[i], 0, 0)
    w_up = pl.BlockSpec((None, None, D_MODEL, F_EXPERT), w_idx)
    return pl.pallas_call(
        _expert_kernel,
        grid_spec=pltpu.PrefetchScalarGridSpec(
            num_scalar_prefetch=4,
            grid=(MOE_NB,),
            in_specs=[pl.BlockSpec(memory_space=pl.ANY), w_up, w_up,
                      pl.BlockSpec((None, None, F_EXPERT, D_MODEL), w_idx)],
            out_specs=pl.BlockSpec((MOE_MB, MOE_PACKED_W), lambda i, be, nu, src, stok: (i, 0)),
            scratch_shapes=[pltpu.VMEM((2, MOE_MB, MOE_PACKED_W), jnp.uint32),
                            pltpu.VMEM((2, MOE_MB, MOE_PACKED_W), BF16),
                            pltpu.VMEM((D_MODEL, F_EXPERT), BF16), pltpu.VMEM((D_MODEL, F_EXPERT), BF16),
                            pltpu.VMEM((F_EXPERT, D_MODEL), BF16), pltpu.SemaphoreType.DMA((2,))]),
        out_shape=jax.ShapeDtypeStruct((MOE_SLOTS, MOE_PACKED_W), jnp.uint32),
        compiler_params=_cparams(1), name="moe_experts",
    )(block_e, n_used, src, sorted_tok, x_packed, wg, wu, wd)


def _shared_up_kernel(x_ref, wg_ref, wu_ref, o_ref, wg_bf, wu_bf):
    @pl.when(pl.program_id(0) == 0)
    def _():
        wg_bf[...] = wg_ref[...].astype(BF16)
        wu_bf[...] = wu_ref[...].astype(BF16)

    x = x_ref[...].astype(BF16)
    g = jnp.dot(x, wg_bf[...], preferred_element_type=F32)
    u = jnp.dot(x, wu_bf[...], preferred_element_type=F32)
    o_ref[...] = (_silu(g) * u).astype(o_ref.dtype)


def _shared_up(h, wg, wu, layer):
    w = pl.BlockSpec((None, D_MODEL, F_SHARED), lambda i: (layer, 0, 0))
    return pl.pallas_call(
        _shared_up_kernel,
        grid=(T_ALL // TM,),
        in_specs=[pl.BlockSpec((TM, D_MODEL), lambda i: (i, 0)), w, w],
        out_specs=pl.BlockSpec((TM, F_SHARED), lambda i: (i, 0)),
        out_shape=jax.ShapeDtypeStruct((T_ALL, F_SHARED), BF16),
        scratch_shapes=[pltpu.VMEM((D_MODEL, F_SHARED), BF16), pltpu.VMEM((D_MODEL, F_SHARED), BF16)],
        compiler_params=_cparams(1), name="shared_up",
    )(h, wg, wu)


def _combine_kernel(slot_ref, nxt_ref, y_hbm, rw_ref, act_ref, wd_ref, x_ref, mod_ref, o_ref, buf, wd_bf, sems, *,
                    gate_row):
    i = pl.program_id(0)
    n_rows = TOP_K * CMB_TM
    half = MOE_PACKED_W

    def row_copy(idx_ref, slot, j):
        return pltpu.make_async_copy(y_hbm.at[pl.ds(idx_ref[0, 0, j], 1), :], buf.at[slot, pl.ds(j, 1), :],
                                     sems.at[slot])

    @pl.when(i == 0)
    def _():
        wd_bf[...] = wd_ref[...].astype(BF16)

        def body(j, carry):
            row_copy(slot_ref, 0, j).start()
            return carry
        lax.fori_loop(0, n_rows, body, 0, unroll=8)

    def tile(prefetch):
        slot = i % 2
        pltpu.make_async_copy(y_hbm.at[pl.ds(0, n_rows), :], buf.at[slot], sems.at[slot]).wait()
        if prefetch:
            for j in range(n_rows):
                row_copy(nxt_ref, 1 - slot, j).start()
        shared = jnp.dot(act_ref[...], wd_bf[...], preferred_element_type=F32)
        lo = shared[:, :half]
        hi = shared[:, half:]
        for k in range(TOP_K):
            row_lo, row_hi = _unpack_bf16_pair(buf[slot, k * CMB_TM:(k + 1) * CMB_TM, :])
            lo = lo + rw_ref[:, k:k + 1] * row_lo
            hi = hi + rw_ref[:, k:k + 1] * row_hi
        gate = mod_ref[gate_row:gate_row + 1, :]
        o_ref[:, :half] = x_ref[:, :half] + gate[:, :half] * lo
        o_ref[:, half:] = x_ref[:, half:] + gate[:, half:] * hi

    @pl.when(i + 1 < pl.num_programs(0))
    def _():
        tile(True)

    @pl.when(i + 1 == pl.num_programs(0))
    def _():
        tile(False)


def _combine(y_sorted, slot_of, route_w, act, sh_wd, x, mod, layer):
    nt = T_ALL // CMB_TM
    slots = slot_of.reshape(nt, CMB_TM, TOP_K).transpose(0, 2, 1).reshape(nt, 1, TOP_K * CMB_TM)
    slot_spec = lambda nxt: pl.BlockSpec((1, 1, TOP_K * CMB_TM), lambda i: (jnp.minimum(i + nxt, nt - 1), 0, 0),
                                         memory_space=pltpu.SMEM)
    return pl.pallas_call(
        functools.partial(_combine_kernel, gate_row=5),
        grid=(nt,),
        in_specs=[slot_spec(0), slot_spec(1),
                  pl.BlockSpec(memory_space=pl.ANY),
                  pl.BlockSpec((CMB_TM, TOP_K), lambda i: (i, 0)),
                  pl.BlockSpec((CMB_TM, F_SHARED), lambda i: (i, 0)),
                  pl.BlockSpec((None, F_SHARED, D_MODEL), lambda i: (layer, 0, 0)),
                  pl.BlockSpec((CMB_TM, D_MODEL), lambda i: (i, 0)),
                  pl.BlockSpec((None, 6, D_MODEL), lambda i: (_group_of_tile(i, CMB_TM), 0, 0))],
        out_specs=pl.BlockSpec((CMB_TM, D_MODEL), lambda i: (i, 0)),
        out_shape=jax.ShapeDtypeStruct((T_ALL, D_MODEL), F32),
        scratch_shapes=[pltpu.VMEM((2, TOP_K * CMB_TM, MOE_PACKED_W), jnp.uint32),
                        pltpu.VMEM((F_SHARED, D_MODEL), BF16), pltpu.SemaphoreType.DMA((2,))],
        compiler_params=_cparams(1), name="moe_combine",
    )(slots, slots, y_sorted, route_w, act, sh_wd, x, mod)


def kernel(x_prompt, x_sample, cache_na_k, cache_na_v, state_gla, state_s5, c, c_ctx, ada_w, ada_b, norm1_g, norm2_g, w_in, na_rpb, gla_wa, gla_ba, gla_norm_g, s5_a_re, s5_a_im, s5_log_dt, s5_b_re, s5_b_im, s5_c_re, s5_c_im, s5_d, s5_w_glu, w_br_na, w_br_gla, w_br_s5, w_merge, w_out, router_w, router_bias, exp_wg, exp_wu, exp_wd, sh_wg, sh_wu, sh_wd, final_norm_g):
    x = jnp.concatenate([x_prompt.reshape(T_CTX, D_MODEL), x_sample.reshape(T_LAT, D_MODEL)], axis=0)
    cvec = jnp.concatenate([c_ctx[None, :], c, jnp.zeros((MOD_ROWS - N_MOD, D_MODEL), F32)], axis=0)
    mods = _ada(cvec, ada_w, ada_b).reshape(DEPTH, MOD_ROWS, 6, D_MODEL)
    cos, sin = _rope_tables()
    w_in_t = jnp.swapaxes(w_in, 1, 2)
    zero_gla = jnp.zeros((BATCH, 2, GLA_HEADS, GLA_DV, GLA_DK), F32)
    zero_s5 = jnp.zeros((S5_GROUPS, 4, BATCH, S5_N), F32)
    new_k, new_v, new_gla, new_s5 = [], [], [], []
    for l in range(DEPTH):
        mod = mods[l]
        h = _norm(x, norm1_g[l], mod, (0, 1), BF16)
        u = _mm(h, w_in_t, l, 1024, n=IN_MAIN, w_rows_are_outputs=True)
        lr = _mm(h, w_in_t[l, IN_MAIN:IN_MAIN + 2 * GLA_LR], None, 2 * GLA_LR, w_rows_are_outputs=True)
        su = _mm(h, w_in_t[l, IN_MAIN + 2 * GLA_LR:], None, S5_W, w_rows_are_outputs=True)
        gates = _mm(h, w_merge, l, 1024, act="sigmoid")

        o_na = jnp.concatenate([_ctx_attn(u),
                                _na_attn(u, cache_na_k, cache_na_v, _na_bias(na_rpb[l]), l)], axis=0)

        s0_lat = state_gla[:, l].transpose(0, 1, 2, 4, 3)
        og_ctx, sfin = _gla(u, lr, gla_wa, gla_ba, gla_norm_g, zero_gla, cos, sin, l, latent=False)
        og_lat, _ = _gla(u, lr, gla_wa, gla_ba, gla_norm_g, s0_lat, cos, sin, l, latent=True)
        o_gla = jnp.concatenate([og_ctx, og_lat], axis=0)

        ops, step = _s5_operators(s5_a_re[l], s5_a_im[l], s5_log_dt[l], s5_b_re[l], s5_b_im[l],
                                  s5_c_re[l], s5_c_im[l])
        st = state_s5[:, l].astype(F32)
        h0_lat = st.transpose(2, 1, 4, 0, 3).reshape(S5_GROUPS, 4, DEC_BATCH, S5_N)
        y_ctx, hfin = _s5(su, ops, step, zero_s5, nb=BATCH, seq=SEQ, first_row_block=0)
        y_lat, _ = _s5(su, ops, step, h0_lat, nb=DEC_BATCH, seq=DEC_SEQ, first_row_block=T_CTX // T_LAT)
        zz = _glu(jnp.concatenate([y_ctx, y_lat], axis=0), su, s5_d, s5_w_glu, l)

        merged = _merge(o_na, o_gla, zz, gates, w_br_na, w_br_gla, w_br_s5, l)
        x = _out_proj(merged, w_out, x, mod, l)

        h2, h2_packed = _norm(x, norm2_g[l], mod, (3, 4), F32, packed=True)
        idx_t, w_t = _router(h2, router_w[l].T, router_bias[l])
        sorted_tok, slot_of, block_e, src, n_used = _dispatch(idx_t.T)
        y_sorted = _experts(h2_packed, sorted_tok, block_e, src, n_used, exp_wg, exp_wu, exp_wd, l)
        act = _shared_up(h2, sh_wg, sh_wu, l)
        x = _combine(y_sorted, slot_of, w_t.T, act, sh_wd, x, mod, l)

        new_k.append(u[:T_CTX, NA_W:2 * NA_W].reshape(BATCH, SEQ, NA_HEADS, NA_DH))
        new_v.append(u[:T_CTX, 2 * NA_W:3 * NA_W].reshape(BATCH, SEQ, NA_HEADS, NA_DH))
        new_gla.append(sfin.transpose(0, 1, 2, 4, 3))
        new_s5.append(hfin.reshape(S5_GROUPS, 2, 2, BATCH, S5_N).transpose(3, 1, 0, 4, 2))

    y = _norm(x, final_norm_g, None, None, F32)
    return (y[:T_CTX].reshape(BATCH, SEQ, D_MODEL), y[T_CTX:].reshape(DEC_BATCH, DEC_SEQ, D_MODEL),
            jnp.stack(new_k, axis=1), jnp.stack(new_v, axis=1),
            jnp.stack(new_gla, axis=1), jnp.stack(new_s5, axis=1))
```

```python
import functools
import math

import jax
import jax.numpy as jnp
from jax import lax
from jax.experimental import pallas as pl
from jax.experimental.pallas import tpu as pltpu

F32 = jnp.float32
BF16 = jnp.bfloat16

D_MODEL = 2048
BATCH = 16
SEQ = 256
DEPTH = 2
DEC_BATCH = 2
DEC_SEQ = 1024
PAST_LEN = 512
GRID_W = 64
NA_HEADS = 8
NA_DH = 128
NA_W = NA_HEADS * NA_DH
NA_ROWS = 8
NA_COLS = 16
GLA_HEADS = 4
GLA_DK = 128
GLA_DV = 256
GLA_QK_W = GLA_HEADS * GLA_DK
GLA_V_W = GLA_HEADS * GLA_DV
GLA_LR = 16
GLA_TAU = 16.0
GLA_CHUNK = 64
GLA_SUB = 16
GLA_UNROLL = 4
ROPE_BASE = 10000.0
S5_W = 1024
S5_GS = 16
S5_GROUPS = S5_W // S5_GS
S5_N = 64
S5_T = 16
IN_W = 3 * NA_W + 2 * GLA_QK_W + 2 * GLA_V_W + 2 * GLA_LR + S5_W
IN_MAIN = 3 * NA_W + 2 * GLA_QK_W + 2 * GLA_V_W
N_EXPERTS = 64
TOP_K = 8
N_GROUPS = 8
TOPK_GROUPS = 4
F_EXPERT = 512
F_SHARED = 512
ROUTED_SCALE = 2.5
EPS = 1e-6

T_CTX = BATCH * SEQ
T_LAT = DEC_BATCH * DEC_SEQ
T_ALL = T_CTX + T_LAT
N_MOD = 1 + DEC_BATCH
MOD_ROWS = 8

TM = 512
MM_TM = 1024
MOE_MB = 256
MOE_NB = T_ALL * TOP_K // MOE_MB + N_EXPERTS
MOE_SLOTS = MOE_NB * MOE_MB
CMB_TM = 128
NEG = -1e30

VMEM_LIMIT = 56 * 1024 * 1024

NT = (((1,), (1,)), ((), ()))
TN = (((0,), (0,)), ((), ()))


def _cparams(n_axes):
    return pltpu.CompilerParams(dimension_semantics=("arbitrary",) * n_axes,
                                vmem_limit_bytes=VMEM_LIMIT)


def _group_of_tile(i, tm):
    row = i * tm
    return jnp.where(row < T_CTX, 0, 1 + (row - T_CTX) // DEC_SEQ)


def _silu(x):
    return x * jax.nn.sigmoid(x)


def _gelu_tanh(x):
    return 0.5 * x * (1.0 + jnp.tanh(math.sqrt(2.0 / math.pi) * (x + 0.044715 * (x * x * x))))


def _ada_kernel(c_ref, w_ref, b_ref, o_ref):
    s = _silu(c_ref[...]).astype(BF16)
    o_ref[...] = jnp.dot(s, w_ref[...].astype(BF16), preferred_element_type=F32) + b_ref[...]


def _ada(cvec, ada_w, ada_b):
    tn = 1024
    return pl.pallas_call(
        _ada_kernel,
        grid=(DEPTH, 6 * D_MODEL // tn),
        in_specs=[pl.BlockSpec((MOD_ROWS, D_MODEL), lambda l, j: (0, 0)),
                  pl.BlockSpec((None, D_MODEL, tn), lambda l, j: (l, 0, j)),
                  pl.BlockSpec((None, 1, tn), lambda l, j: (l, 0, j))],
        out_specs=pl.BlockSpec((None, MOD_ROWS, tn), lambda l, j: (l, 0, j)),
        out_shape=jax.ShapeDtypeStruct((DEPTH, MOD_ROWS, 6 * D_MODEL), F32),
        compiler_params=_cparams(2), name="ada",
    )(cvec, ada_w, ada_b.reshape(DEPTH, 1, 6 * D_MODEL))


def _pack_bf16_pair(lo, hi):
    bits = lambda v: pltpu.bitcast(v.astype(BF16).astype(F32), jnp.uint32)
    return (bits(lo) >> 16) | (bits(hi) & jnp.uint32(0xFFFF0000))


def _unpack_bf16_pair(w):
    return pltpu.bitcast(w << 16, F32), pltpu.bitcast(w & jnp.uint32(0xFFFF0000), F32)


def _norm_kernel(x_ref, g_ref, *rest, rows, packed):
    x = x_ref[...]
    y = x * lax.rsqrt(jnp.mean(x * x, axis=-1, keepdims=True) + EPS) * g_ref[...]
    if rows is not None:
        mod_ref = rest[0]
        y = y * (1.0 + mod_ref[rows[1]:rows[1] + 1, :]) + mod_ref[rows[0]:rows[0] + 1, :]
    if packed:
        rest[-2][...] = y
        rest[-1][...] = _pack_bf16_pair(y[:, :D_MODEL // 2], y[:, D_MODEL // 2:])
    else:
        rest[-1][...] = y.astype(rest[-1].dtype)


def _norm(x, g, mod, rows, out_dtype, packed=False):
    tm = 256
    in_specs = [pl.BlockSpec((tm, D_MODEL), lambda i: (i, 0)),
                pl.BlockSpec((1, D_MODEL), lambda i: (0, 0))]
    args = [x, g.reshape(1, D_MODEL)]
    if rows is not None:
        in_specs.append(pl.BlockSpec((None, 6, D_MODEL), lambda i: (_group_of_tile(i, tm), 0, 0)))
        args.append(mod)
    out_specs = pl.BlockSpec((tm, D_MODEL), lambda i: (i, 0))
    out_shape = jax.ShapeDtypeStruct((T_ALL, D_MODEL), out_dtype)
    if packed:
        out_specs = [out_specs, pl.BlockSpec((tm, D_MODEL // 2), lambda i: (i, 0))]
        out_shape = [out_shape, jax.ShapeDtypeStruct((T_ALL, D_MODEL // 2), jnp.uint32)]
    return pl.pallas_call(
        functools.partial(_norm_kernel, rows=rows, packed=packed),
        grid=(T_ALL // tm,),
        in_specs=in_specs,
        out_specs=out_specs,
        out_shape=out_shape,
        compiler_params=_cparams(1), name="norm",
    )(*args)


def _mm_kernel(x_ref, w_ref, o_ref, wbf_ref, *, act, w_rows_are_outputs):
    @pl.when(pl.program_id(1) == 0)
    def _():
        wbf_ref[...] = w_ref[...].astype(BF16)

    x = x_ref[...].astype(BF16)
    if w_rows_are_outputs:
        acc = lax.dot_general(x, wbf_ref[...], NT, preferred_element_type=F32)
    else:
        acc = jnp.dot(x, wbf_ref[...], preferred_element_type=F32)
    if act == "sigmoid":
        acc = jax.nn.sigmoid(acc)
    o_ref[...] = acc.astype(o_ref.dtype)


def _mm(x, w, layer, tn, act=None, out_dtype=F32, n=None, w_rows_are_outputs=False):
    k = x.shape[1]
    if n is None:
        n = w.shape[-2] if w_rows_are_outputs else w.shape[-1]
    blk, idx = ((tn, k), lambda j: (j, 0)) if w_rows_are_outputs else ((k, tn), lambda j: (0, j))
    if layer is None:
        w_spec = pl.BlockSpec(blk, lambda j, i: idx(j))
    else:
        w_spec = pl.BlockSpec((None,) + blk, lambda j, i: (layer,) + idx(j))
    return pl.pallas_call(
        functools.partial(_mm_kernel, act=act, w_rows_are_outputs=w_rows_are_outputs),
        grid=(n // tn, T_ALL // MM_TM),
        in_specs=[pl.BlockSpec((MM_TM, k), lambda j, i: (i, 0)), w_spec],
        out_specs=pl.BlockSpec((MM_TM, tn), lambda j, i: (i, j)),
        out_shape=jax.ShapeDtypeStruct((T_ALL, n), out_dtype),
        scratch_shapes=[pltpu.VMEM(blk, BF16)],
        compiler_params=_cparams(2), name="mm",
    )(x, w)


def _ctx_attn_kernel(q_ref, k_ref, v_ref, o_ref):
    scale = NA_DH ** -0.5
    for h in range(NA_HEADS):
        sl = slice(h * NA_DH, (h + 1) * NA_DH)
        q = q_ref[:, sl].astype(BF16)
        k = k_ref[:, sl].astype(BF16)
        v = v_ref[:, sl].astype(BF16)
        s = lax.dot_general(q, k, NT, preferred_element_type=F32) * scale
        p = jnp.exp(s - jnp.max(s, axis=-1, keepdims=True))
        o = jnp.dot(p.astype(BF16), v, preferred_element_type=F32) / jnp.sum(p, axis=-1, keepdims=True)
        o_ref[:, sl] = o.astype(o_ref.dtype)


def _ctx_attn(u):
    spec = lambda cb: pl.BlockSpec((SEQ, NA_W), lambda b: (b, cb))
    return pl.pallas_call(
        _ctx_attn_kernel,
        grid=(BATCH,),
        in_specs=[spec(0), spec(1), spec(2)],
        out_specs=pl.BlockSpec((SEQ, NA_W), lambda b: (b, 0)),
        out_shape=jax.ShapeDtypeStruct((T_CTX, NA_W), BF16),
        compiler_params=_cparams(1), name="ctx_attn",
    )(u, u, u)


NA_GRID_ROWS = DEC_SEQ // GRID_W
NA_KR = min(NA_ROWS, NA_GRID_ROWS)
NA_LOC = NA_KR * GRID_W


def _na_bias(rpb):
    n_off = 2 * NA_ROWS - 1
    pad = GRID_W - NA_COLS
    period = 2 * GRID_W
    p = jnp.pad(rpb, ((0, 0), (0, 0), (pad, period - pad - (2 * NA_COLS - 1))))
    hank = jnp.tile(p, (1, 1, GRID_W + 1))[..., :GRID_W * (period + 1)]
    hank = hank.reshape(NA_HEADS, n_off, GRID_W, period + 1)[..., :GRID_W]
    band = hank[:, :, ::-1, :]
    qc = jnp.arange(GRID_W)[:, None]
    kc = jnp.arange(GRID_W)[None, :]
    win = jnp.clip(qc - NA_COLS // 2, 0, GRID_W - NA_COLS)
    valid = (kc >= win) & (kc < win + NA_COLS)
    band = jnp.where(valid[None, None], band, NEG)
    return band.transpose(0, 2, 1, 3).reshape(NA_HEADS, GRID_W, n_off * GRID_W)


def _na_kernel(q_ref, k_ref, v_ref, ck_ref, cv_ref, bias_ref, o_ref):
    scale = NA_DH ** -0.5
    kb = k_ref[...].astype(BF16)
    vb = v_ref[...].astype(BF16)
    ck = ck_ref[...].astype(BF16)
    cv = cv_ref[...].astype(BF16)
    for r in range(NA_GRID_ROWS):
        first = min(max(r - NA_KR // 2, 0), NA_GRID_ROWS - NA_KR)
        off = first - r + NA_ROWS - 1
        rows = slice(r * GRID_W, (r + 1) * GRID_W)
        keys = slice(first * GRID_W, first * GRID_W + NA_LOC)
        q = q_ref[rows, :].astype(BF16)
        s1 = (lax.dot_general(q, kb[keys, :], NT, preferred_element_type=F32) * scale
              + bias_ref[:, off * GRID_W:off * GRID_W + NA_LOC])
        s2 = lax.dot_general(q, ck, NT, preferred_element_type=F32) * scale
        m = jnp.maximum(jnp.max(s1, axis=-1, keepdims=True), jnp.max(s2, axis=-1, keepdims=True))
        p1 = jnp.exp(s1 - m)
        p2 = jnp.exp(s2 - m)
        den = jnp.sum(p1, axis=-1, keepdims=True) + jnp.sum(p2, axis=-1, keepdims=True)
        o = (jnp.dot(p1.astype(BF16), vb[keys, :], preferred_element_type=F32)
             + jnp.dot(p2.astype(BF16), cv, preferred_element_type=F32))
        o_ref[rows, :] = (o / den).astype(o_ref.dtype)


def _na_attn(u, cache_k, cache_v, bias, layer):
    lat_sb = T_CTX // DEC_SEQ
    ck = cache_k.reshape(DEC_BATCH, DEPTH, PAST_LEN, NA_W)
    cv = cache_v.reshape(DEC_BATCH, DEPTH, PAST_LEN, NA_W)
    qkv_spec = lambda cb: pl.BlockSpec((DEC_SEQ, NA_DH), lambda b, h: (lat_sb + b, cb * NA_HEADS + h))
    c_spec = pl.BlockSpec((None, None, PAST_LEN, NA_DH), lambda b, h: (b, layer, 0, h))
    return pl.pallas_call(
        _na_kernel,
        grid=(DEC_BATCH, NA_HEADS),
        in_specs=[qkv_spec(0), qkv_spec(1), qkv_spec(2), c_spec, c_spec,
                  pl.BlockSpec((None, GRID_W, (2 * NA_ROWS - 1) * GRID_W), lambda b, h: (h, 0, 0))],
        out_specs=pl.BlockSpec((DEC_SEQ, NA_DH), lambda b, h: (b, h)),
        out_shape=jax.ShapeDtypeStruct((T_LAT, NA_W), BF16),
        compiler_params=_cparams(2), name="na_attn",
    )(u, u, u, ck, cv, bias)


def _rope_tables():
    half = GLA_DK // 2
    nf = half // 2
    t = jnp.arange(DEC_SEQ)
    freqs = ROPE_BASE ** (-jnp.arange(nf, dtype=F32) / nf)
    ang_r = (t // GRID_W).astype(F32)[:, None] * freqs
    ang_c = (t % GRID_W).astype(F32)[:, None] * freqs
    cos = jnp.concatenate([jnp.cos(ang_r), jnp.cos(ang_r), jnp.cos(ang_c), jnp.cos(ang_c)], axis=-1)
    sin = jnp.concatenate([-jnp.sin(ang_r), jnp.sin(ang_r), -jnp.sin(ang_c), jnp.sin(ang_c)], axis=-1)
    return cos, sin


def _split_bf16(x):
    hi = x.astype(BF16)
    return hi, (x - hi.astype(F32)).astype(BF16)


def _gla_kernel(q_ref, k_ref, v_ref, gg_ref, lr_ref, wa_ref, ba_ref, ng_ref, cos_ref, sin_ref, s0_ref,
                o_ref, sfin_ref, qs, ks, las, o_acc, st, *, seq, rope):
    nc = seq // GLA_CHUNK
    nq = GLA_DK // 4

    def rot(x):
        lane = lax.broadcasted_iota(jnp.int32, x.shape, 1)
        partner = jnp.where((lane % (2 * nq)) < nq,
                            pltpu.roll(x, GLA_DK - nq, axis=1), pltpu.roll(x, nq, axis=1))
        return x * cos_ref[...] + partner * sin_ref[...]

    q = q_ref[...] * (GLA_DK ** -0.5)
    k = k_ref[...]
    if rope:
        q = rot(q)
        k = rot(k)
    qs[...] = q
    ks[...] = k
    lr = lr_ref[...]
    for d in range(2):
        z = jnp.dot(lr[:, d * GLA_LR:(d + 1) * GLA_LR].astype(BF16), wa_ref[d].astype(BF16),
                    preferred_element_type=F32) + ba_ref[d:d + 1, :]
        las[d] = -(jnp.maximum(-z, 0.0) + jnp.log1p(jnp.exp(-jnp.abs(z)))) / GLA_TAU
        st[d] = s0_ref[d]

    row = lax.broadcasted_iota(jnp.int32, (GLA_CHUNK, GLA_CHUNK), 0)
    col = lax.broadcasted_iota(jnp.int32, (GLA_CHUNK, GLA_CHUNK), 1)
    key_row = lax.broadcasted_iota(jnp.int32, (GLA_CHUNK, 1), 0)

    def chunk(c, d):
        rev = d == 1
        rows = pl.ds(pl.multiple_of(c * GLA_CHUNK, GLA_CHUNK), GLA_CHUNK)
        qc = qs[rows, :]
        kc = ks[rows, :]
        vc = v_ref[rows, :].astype(BF16)
        la = las[d, rows, :]
        causal = (col >= row) if rev else (col <= row)
        tri = jnp.where(causal, 1.0, 0.0).astype(BF16)
        la_hi, la_lo = _split_bf16(la)
        b = (jnp.dot(tri, la_hi, preferred_element_type=F32)
             + jnp.dot(tri, la_lo, preferred_element_type=F32))
        bex = b - la
        b_last = b[0:1, :] if rev else b[GLA_CHUNK - 1:GLA_CHUNK, :]
        blocks = []
        for i in range(GLA_CHUNK // GLA_SUB):
            lo, hi = i * GLA_SUB, (i + 1) * GLA_SUB
            ref = bex[hi - 1:hi, :] if rev else bex[lo:lo + 1, :]
            qt = (qc[lo:hi, :] * jnp.exp(b[lo:hi, :] - ref)).astype(BF16)
            allowed = (key_row >= lo) if rev else (key_row < hi)
            kt = (kc * jnp.exp(jnp.where(allowed, ref - b, -jnp.inf))).astype(BF16)
            blocks.append(lax.dot_general(qt, kt, NT, preferred_element_type=F32))
        att = jnp.where(causal, jnp.concatenate(blocks, axis=0), 0.0)
        s_t = st[d]
        o = (jnp.dot(att.astype(BF16), vc, preferred_element_type=F32)
             + lax.dot_general((qc * jnp.exp(b)).astype(BF16), s_t.astype(BF16), NT,
                               preferred_element_type=F32))
        khat = (kc * jnp.exp(b_last - b)).astype(BF16)
        st[d] = s_t * jnp.exp(b_last) + lax.dot_general(vc, khat, TN, preferred_element_type=F32)
        o_acc[d, rows, :] = o

    def both(c, carry):
        chunk(c, 0)
        chunk(nc - 1 - c, 1)
        return carry

    lax.fori_loop(0, nc, both, 0, unroll=GLA_UNROLL)
    sfin_ref[...] = st[...]
    o = o_acc[0] + o_acc[1]
    o = o * lax.rsqrt(jnp.mean(o * o, axis=-1, keepdims=True) + EPS) * ng_ref[...]
    o_ref[...] = (o * _silu(gg_ref[...])).astype(o_ref.dtype)


def _gla(u, lr, wa, ba, norm_g, s0_t, cos, sin, layer, *, latent):
    seq, nb, first = (DEC_SEQ, DEC_BATCH, T_CTX // DEC_SEQ) if latent else (SEQ, BATCH, 0)
    qk_cb = 3 * NA_W // GLA_DK
    v_cb = (3 * NA_W + 2 * GLA_QK_W) // GLA_DV
    row = lambda w, cb: pl.BlockSpec((seq, w), lambda b, h: (first + b, cb + h))
    return pl.pallas_call(
        functools.partial(_gla_kernel, seq=seq, rope=latent),
        grid=(nb, GLA_HEADS),
        in_specs=[row(GLA_DK, qk_cb), row(GLA_DK, qk_cb + GLA_HEADS), row(GLA_DV, v_cb),
                  row(GLA_DV, v_cb + GLA_HEADS),
                  pl.BlockSpec((seq, 2 * GLA_LR), lambda b, h: (first + b, 0)),
                  pl.BlockSpec((None, 2, GLA_LR, GLA_DK), lambda b, h: (layer, 0, 0, h)),
                  pl.BlockSpec((None, 2, GLA_DK), lambda b, h: (layer, 0, h)),
                  pl.BlockSpec((None, 1, GLA_DV), lambda b, h: (layer, 0, h)),
                  pl.BlockSpec((seq, GLA_DK), lambda b, h: (0, 0)),
                  pl.BlockSpec((seq, GLA_DK), lambda b, h: (0, 0)),
                  pl.BlockSpec((None, 2, None, GLA_DV, GLA_DK), lambda b, h: (b, 0, h, 0, 0))],
        out_specs=[pl.BlockSpec((seq, GLA_DV), lambda b, h: (b, h)),
                   pl.BlockSpec((None, 2, None, GLA_DV, GLA_DK), lambda b, h: (b, 0, h, 0, 0))],
        out_shape=[jax.ShapeDtypeStruct((nb * seq, GLA_V_W), BF16),
                   jax.ShapeDtypeStruct((nb, 2, GLA_HEADS, GLA_DV, GLA_DK), F32)],
        scratch_shapes=[pltpu.VMEM((seq, GLA_DK), F32), pltpu.VMEM((seq, GLA_DK), F32),
                        pltpu.VMEM((2, seq, GLA_DK), F32), pltpu.VMEM((2, seq, GLA_DV), F32),
                        pltpu.VMEM((2, GLA_DV, GLA_DK), F32)],
        compiler_params=_cparams(2), name="gla",
    )(u, u, u, u, lr, wa, ba, norm_g.reshape(DEPTH, 1, GLA_V_W), cos[:seq], sin[:seq], s0_t)


def _s5_operators(a_re, a_im, log_dt, b_re, b_im, c_re, c_im):
    hp = lax.Precision.HIGHEST
    lam = lax.complex(a_re.astype(F32), a_im.astype(F32))
    lam_dt = lam * jnp.exp(log_dt.astype(F32))[..., None]
    a_bar = jnp.exp(lam_dt)
    b_bar = ((a_bar - 1.0) / lam)[..., None] * lax.complex(b_re.astype(F32), b_im.astype(F32))
    cc = lax.complex(c_re.astype(F32), c_im.astype(F32))
    taus = jnp.arange(S5_T + 1, dtype=F32)
    pw = jnp.exp(lam_dt[:, :, None, :] * taus[None, None, :, None])
    lanes = S5_T * S5_GS

    def out_weights(pwd, cd):
        m = pwd.transpose(0, 2, 1)[:, :, :, None] * cd.transpose(0, 2, 1)[:, :, None, :]
        return m.reshape(S5_GROUPS, S5_N, lanes)

    def lag_response(d):
        r = out_weights(pw[d, :, :S5_T], cc[d])
        bt = b_bar[d].transpose(0, 2, 1)
        return (jnp.einsum("gin,gnx->gix", jnp.real(bt), jnp.real(r), precision=hp)
                - jnp.einsum("gin,gnx->gix", jnp.imag(bt), jnp.imag(r), precision=hp))

    def toeplitz(lags):
        rows = [lags[:, :, (S5_T - 1 - s) * S5_GS:(2 * S5_T - 1 - s) * S5_GS] for s in range(S5_T)]
        return jnp.stack(rows, axis=1).reshape(S5_GROUPS, lanes, lanes)

    no_lag = jnp.zeros((S5_GROUPS, S5_GS, (S5_T - 1) * S5_GS), F32)
    rev_lags = lag_response(1).reshape(S5_GROUPS, S5_GS, S5_T, S5_GS)[:, :, ::-1].reshape(S5_GROUPS, S5_GS, lanes)
    toep_f = toeplitz(jnp.concatenate([no_lag, lag_response(0)], axis=-1))
    toep_b = toeplitz(jnp.concatenate([rev_lags, no_lag], axis=-1))

    def state_in(pwd, bd):
        m = pwd[:, :, None, :] * bd.transpose(0, 2, 1)[:, None, :, :]
        m = m.reshape(S5_GROUPS, lanes, S5_N)
        return [jnp.real(m), jnp.imag(m)]

    p = jnp.stack(state_in(pw[0, :, S5_T - 1::-1], b_bar[0])
                  + state_in(pw[1, :, :S5_T], b_bar[1]), axis=1)

    def state_out(pwd, cd):
        m = out_weights(pwd, cd)
        return [jnp.real(m), -jnp.imag(m)]

    q = jnp.stack(state_out(pw[0, :, 1:S5_T + 1], cc[0])
                  + state_out(pw[1, :, S5_T:0:-1], cc[1]), axis=1)
    ops = [m.astype(BF16) for m in (toep_f, toep_b, p, q)]
    a_t = pw[:, :, S5_T]
    step = jnp.stack([jnp.real(a_t[0]), jnp.imag(a_t[0]), jnp.real(a_t[1]), jnp.imag(a_t[1])], axis=1)
    return ops, step


S5_GB = 128 // S5_GS
S5_SEG = 128 // S5_GS


def _s5_kernel(x_ref, tf_ref, tb_ref, p_ref, q_ref, step_ref, h0_ref, y_ref, hfin_ref,
               u_scr, y_scr, e_scr, hin_scr, *, nc, nb):
    n_chunks = nb * nc
    for s in range(S5_T):
        xs = x_ref[pl.ds(s, n_chunks, stride=S5_T), :]
        half, pos = divmod(s, S5_SEG)
        dst = slice(half * 128 + pos * S5_GS, half * 128 + (pos + 1) * S5_GS)
        for g in range(S5_GB):
            moved = pltpu.roll(xs, ((pos - g) % S5_SEG) * S5_GS, axis=1)
            u_scr[g, :, dst] = moved[:, pos * S5_GS:(pos + 1) * S5_GS].astype(BF16)

    def group(g, carry):
        u = u_scr[g]
        for m in range(4):
            e_scr[m] = jnp.dot(u, p_ref[g, m], preferred_element_type=F32)
        ar = [step_ref[g, 0:1, :], step_ref[g, 2:3, :]]
        ai = [step_ref[g, 1:2, :], step_ref[g, 3:4, :]]
        hr = [h0_ref[g, 0], h0_ref[g, 2]]
        hi = [h0_ref[g, 1], h0_ref[g, 3]]
        for c in range(nc):
            for d in range(2):
                rows = pl.ds(c if d == 0 else nc - 1 - c, nb, stride=nc)
                hin_scr[2 * d, rows, :] = hr[d]
                hin_scr[2 * d + 1, rows, :] = hi[d]
                hr[d], hi[d] = (hr[d] * ar[d] - hi[d] * ai[d] + e_scr[2 * d, rows, :],
                                hi[d] * ar[d] + hr[d] * ai[d] + e_scr[2 * d + 1, rows, :])
        for d in range(2):
            hfin_ref[g, 2 * d] = hr[d]
            hfin_ref[g, 2 * d + 1] = hi[d]
        y = (jnp.dot(u, tf_ref[g], preferred_element_type=F32)
             + jnp.dot(u, tb_ref[g], preferred_element_type=F32))
        for m in range(4):
            y += jnp.dot(hin_scr[m].astype(BF16), q_ref[g, m], preferred_element_type=F32)
        y_scr[g] = y
        return carry

    lax.fori_loop(0, S5_GB, group, 0)

    segment = lax.broadcasted_iota(jnp.int32, (n_chunks, 128), 1) // S5_GS
    for t in range(S5_T):
        half, pos = divmod(t, S5_SEG)
        rows = jnp.zeros((n_chunks, 128), F32)
        for g in range(S5_GB):
            moved = pltpu.roll(y_scr[g, :, half * 128:(half + 1) * 128], ((g - pos) % S5_SEG) * S5_GS, axis=1)
            rows = jnp.where(segment == g, moved, rows)
        y_ref[pl.ds(t, n_chunks, stride=S5_T), :] = rows


def _s5(su, ops, step, h0, *, nb, seq, first_row_block):
    nc = seq // S5_T
    lanes = S5_T * S5_GS
    n_chunks = nb * nc
    gb = lambda *shape: pl.BlockSpec((S5_GB,) + shape, lambda g: (g,) + (0,) * len(shape))
    return pl.pallas_call(
        functools.partial(_s5_kernel, nc=nc, nb=nb),
        grid=(S5_GROUPS // S5_GB,),
        in_specs=[pl.BlockSpec((nb * seq, S5_GB * S5_GS), lambda g: (first_row_block, g)),
                  gb(lanes, lanes), gb(lanes, lanes), gb(4, lanes, S5_N), gb(4, S5_N, lanes),
                  gb(4, S5_N), gb(4, nb, S5_N)],
        out_specs=[pl.BlockSpec((nb * seq, S5_GB * S5_GS), lambda g: (0, g)), gb(4, nb, S5_N)],
        out_shape=[jax.ShapeDtypeStruct((nb * seq, S5_W), F32),
                   jax.ShapeDtypeStruct((S5_GROUPS, 4, nb, S5_N), F32)],
        scratch_shapes=[pltpu.VMEM((S5_GB, n_chunks, lanes), BF16), pltpu.VMEM((S5_GB, n_chunks, lanes), F32),
                        pltpu.VMEM((4, n_chunks, S5_N), F32), pltpu.VMEM((4, n_chunks, S5_N), F32)],
        compiler_params=_cparams(1), name="s5",
    )(su, *ops, step, h0)


def _glu_kernel(y_ref, su_ref, d_ref, w_ref, o_ref, wbf_ref):
    @pl.when(pl.program_id(0) == 0)
    def _():
        wbf_ref[...] = w_ref[...].astype(BF16)

    z = _gelu_tanh(y_ref[...] + d_ref[...] * su_ref[...])
    gate = jax.nn.sigmoid(jnp.dot(z.astype(BF16), wbf_ref[...], preferred_element_type=F32))
    o_ref[...] = (z * gate).astype(o_ref.dtype)


def _glu(y, su, d, w_glu, layer):
    row = pl.BlockSpec((TM, S5_W), lambda i: (i, 0))
    return pl.pallas_call(
        _glu_kernel,
        grid=(T_ALL // TM,),
        in_specs=[row, row, pl.BlockSpec((None, 1, S5_W), lambda i: (layer, 0, 0)),
                  pl.BlockSpec((None, S5_W, S5_W), lambda i: (layer, 0, 0))],
        out_specs=row,
        out_shape=jax.ShapeDtypeStruct((T_ALL, S5_W), BF16),
        scratch_shapes=[pltpu.VMEM((S5_W, S5_W), BF16)],
        compiler_params=_cparams(1), name="glu",
    )(y, su, d.reshape(DEPTH, 1, S5_W), w_glu)


def _merge_kernel(a_ref, b_ref, c_ref, ga_ref, gb_ref, gc_ref, wa_ref, wb_ref, wc_ref, o_ref, wbf_ref):
    @pl.when(pl.program_id(1) == 0)
    def _():
        wbf_ref[0] = wa_ref[...].astype(BF16)
        wbf_ref[1] = wb_ref[...].astype(BF16)
        wbf_ref[2] = wc_ref[...].astype(BF16)

    acc = ga_ref[...] * jnp.dot(a_ref[...], wbf_ref[0], preferred_element_type=F32)
    acc += gb_ref[...] * jnp.dot(b_ref[...], wbf_ref[1], preferred_element_type=F32)
    acc += gc_ref[...] * jnp.dot(c_ref[...], wbf_ref[2], preferred_element_type=F32)
    o_ref[...] = acc.astype(o_ref.dtype)


def _merge(o_na, o_gla, zz, gates, w_na, w_gla, w_s5, layer):
    tn = 1024
    ncb = D_MODEL // tn
    act = pl.BlockSpec((TM, NA_W), lambda j, i: (i, 0))
    gate = lambda k: pl.BlockSpec((TM, tn), lambda j, i: (i, k * ncb + j))
    w = pl.BlockSpec((None, NA_W, tn), lambda j, i: (layer, 0, j))
    return pl.pallas_call(
        _merge_kernel,
        grid=(ncb, T_ALL // TM),
        in_specs=[act, act, act, gate(0), gate(1), gate(2), w, w, w],
        out_specs=pl.BlockSpec((TM, tn), lambda j, i: (i, j)),
        out_shape=jax.ShapeDtypeStruct((T_ALL, D_MODEL), BF16),
        scratch_shapes=[pltpu.VMEM((3, NA_W, tn), BF16)],
        compiler_params=_cparams(2), name="merge",
    )(o_na, o_gla, zz, gates, gates, gates, w_na, w_gla, w_s5)


def _out_proj_kernel(m_ref, w_ref, x_ref, mod_ref, o_ref, wbf_ref, *, gate_row):
    @pl.when(pl.program_id(1) == 0)
    def _():
        wbf_ref[...] = w_ref[...].astype(BF16)

    y = jnp.dot(m_ref[...], wbf_ref[...], preferred_element_type=F32)
    o_ref[...] = x_ref[...] + mod_ref[gate_row:gate_row + 1, :] * y


def _out_proj(merged, w_out, x, mod, layer):
    tn = 1024
    return pl.pallas_call(
        functools.partial(_out_proj_kernel, gate_row=2),
        grid=(D_MODEL // tn, T_ALL // TM),
        in_specs=[pl.BlockSpec((TM, D_MODEL), lambda j, i: (i, 0)),
                  pl.BlockSpec((None, D_MODEL, tn), lambda j, i: (layer, 0, j)),
                  pl.BlockSpec((TM, tn), lambda j, i: (i, j)),
                  pl.BlockSpec((None, 6, tn), lambda j, i: (_group_of_tile(i, TM), 0, j))],
        out_specs=pl.BlockSpec((TM, tn), lambda j, i: (i, j)),
        out_shape=jax.ShapeDtypeStruct((T_ALL, D_MODEL), F32),
        scratch_shapes=[pltpu.VMEM((D_MODEL, tn), BF16)],
        compiler_params=_cparams(2), name="out_proj",
    )(merged, w_out, x, mod)


def _first_max(v, iota, n):
    m = jnp.max(v, axis=0, keepdims=True)
    first = jnp.min(jnp.where(v == m, iota, float(n)), axis=0, keepdims=True)
    return m, first


def _router_kernel(x_ref, wt_ref, bias_ref, idx_ref, w_ref):
    per = N_EXPERTS // N_GROUPS
    x_hi, x_lo = _split_bf16(x_ref[...])
    w_hi, w_lo = _split_bf16(wt_ref[...])
    logits = (lax.dot_general(w_hi, x_hi, NT, preferred_element_type=F32)
              + lax.dot_general(w_hi, x_lo, NT, preferred_element_type=F32)
              + lax.dot_general(w_lo, x_hi, NT, preferred_element_type=F32))
    scores = jax.nn.sigmoid(logits)
    sel = scores + bias_ref[...]
    tm = sel.shape[1]
    iota_g = lax.broadcasted_iota(jnp.int32, (per, tm), 0).astype(F32)
    grp_rows = []
    for g in range(N_GROUPS):
        v = sel[g * per:(g + 1) * per, :]
        m1, first = _first_max(v, iota_g, per)
        m2 = jnp.max(jnp.where(iota_g == first, -jnp.inf, v), axis=0, keepdims=True)
        grp_rows.append(m1 + m2)
    grp = jnp.concatenate(grp_rows, axis=0)
    iota_n = lax.broadcasted_iota(jnp.int32, (N_GROUPS, tm), 0).astype(F32)
    chosen = jnp.zeros((N_GROUPS, tm), F32)
    for _ in range(TOPK_GROUPS):
        _, first = _first_max(grp, iota_n, N_GROUPS)
        hit = iota_n == first
        chosen = jnp.where(hit, 1.0, chosen)
        grp = jnp.where(hit, -jnp.inf, grp)
    mask = jnp.concatenate([jnp.broadcast_to(chosen[g:g + 1, :], (per, tm)) for g in range(N_GROUPS)], axis=0)
    sel = jnp.where(mask > 0.5, sel, -jnp.inf)
    iota_e = lax.broadcasted_iota(jnp.int32, (N_EXPERTS, tm), 0).astype(F32)
    ids, ws = [], []
    for _ in range(TOP_K):
        _, first = _first_max(sel, iota_e, N_EXPERTS)
        hit = iota_e == first
        ids.append(first)
        ws.append(jnp.sum(jnp.where(hit, scores, 0.0), axis=0, keepdims=True))
        sel = jnp.where(hit, -jnp.inf, sel)
    w = jnp.concatenate(ws, axis=0)
    idx_ref[...] = jnp.concatenate(ids, axis=0).astype(jnp.int32)
    w_ref[...] = w / jnp.sum(w, axis=0, keepdims=True) * ROUTED_SCALE


def _router(h, router_w_t, router_bias):
    tm = 256
    return pl.pallas_call(
        _router_kernel,
        grid=(T_ALL // tm,),
        in_specs=[pl.BlockSpec((tm, D_MODEL), lambda i: (i, 0)),
                  pl.BlockSpec((N_EXPERTS, D_MODEL), lambda i: (0, 0)),
                  pl.BlockSpec((N_EXPERTS, 1), lambda i: (0, 0))],
        out_specs=[pl.BlockSpec((TOP_K, tm), lambda i: (0, i)), pl.BlockSpec((TOP_K, tm), lambda i: (0, i))],
        out_shape=[jax.ShapeDtypeStruct((TOP_K, T_ALL), jnp.int32), jax.ShapeDtypeStruct((TOP_K, T_ALL), F32)],
        compiler_params=_cparams(1), name="router",
    )(h, router_w_t, router_bias.reshape(N_EXPERTS, 1))


def _dispatch(idx):
    n_assign = T_ALL * TOP_K
    flat_e = idx.reshape(-1)
    experts = jnp.arange(N_EXPERTS, dtype=jnp.int32)
    sorted_e, order = lax.sort((flat_e, jnp.arange(n_assign, dtype=jnp.int32)), num_keys=1, is_stable=True)
    counts = jnp.sum((flat_e[:, None] == experts[None, :]).astype(jnp.int32), axis=0)
    cnt_start = jnp.cumsum(counts) - counts
    padded = (counts + MOE_MB - 1) // MOE_MB * MOE_MB
    pad_end = jnp.cumsum(padded)
    pad_start = pad_end - padded
    onehot = (sorted_e[:, None] == experts[None, :]).astype(jnp.int32)
    slot_sorted = jnp.arange(n_assign, dtype=jnp.int32) + jnp.sum(onehot * (pad_start - cnt_start)[None, :], axis=1)
    _, slot_of = lax.sort((order, slot_sorted), num_keys=1)
    blk_first = jnp.arange(MOE_NB, dtype=jnp.int32) * MOE_MB
    block_e = jnp.minimum(jnp.sum((pad_end[None, :] <= blk_first[:, None]).astype(jnp.int32), axis=1),
                          N_EXPERTS - 1)
    blk_onehot = (block_e[:, None] == experts[None, :]).astype(jnp.int32)
    pick = lambda v: jnp.sum(blk_onehot * v[None, :], axis=1)
    src = jnp.clip(pick(cnt_start) + blk_first - pick(pad_start), 0, n_assign)
    sorted_tok = jnp.pad(order // TOP_K, (0, MOE_MB))
    n_used = (pad_end[-1] // MOE_MB).astype(jnp.int32).reshape(1)
    return sorted_tok, slot_of.reshape(T_ALL, TOP_K), block_e.astype(jnp.int32), src.astype(jnp.int32), n_used


MOE_PACKED_W = D_MODEL // 2


def _expert_kernel(be_ref, nu_ref, src_ref, stok_ref, x_hbm, wg_ref, wu_ref, wd_ref, o_ref,
                   xbuf, x_bf, wg_bf, wu_bf, wd_bf, sems):
    i = pl.program_id(0)
    n_used = nu_ref[0]
    last = MOE_NB - 1
    half = MOE_PACKED_W

    def row_copy(blk, slot, j):
        tok = stok_ref[src_ref[blk] + j]
        return pltpu.make_async_copy(x_hbm.at[pl.ds(tok, 1), :], xbuf.at[slot, pl.ds(j, 1), :], sems.at[slot])

    def wait_block(slot):
        pltpu.make_async_copy(x_hbm.at[pl.ds(0, MOE_MB), :], xbuf.at[slot], sems.at[slot]).wait()

    @pl.when(i == 0)
    def _():
        def body(j, carry):
            row_copy(0, 0, j).start()
            return carry
        lax.fori_loop(0, MOE_MB, body, 0, unroll=8)

    def compute(prefetch):
        @pl.when((i == 0) | (be_ref[i] != be_ref[jnp.maximum(i - 1, 0)]))
        def _():
            wg_bf[...] = wg_ref[...].astype(BF16)
            wu_bf[...] = wu_ref[...].astype(BF16)
            wd_bf[...] = wd_ref[...].astype(BF16)

        slot = i % 2
        wait_block(slot)
        x_lo, x_hi = _unpack_bf16_pair(xbuf[slot])
        x_bf[0] = x_lo.astype(BF16)
        x_bf[1] = x_hi.astype(BF16)
        if prefetch:
            for j in range(MOE_MB):
                row_copy(i + 1, 1 - slot, j).start()
        up = lambda w: (jnp.dot(x_bf[0], w[:half, :], preferred_element_type=F32)
                        + jnp.dot(x_bf[1], w[half:, :], preferred_element_type=F32))
        a = (_silu(up(wg_bf)) * up(wu_bf)).astype(BF16)
        y = jnp.dot(a, wd_bf[...], preferred_element_type=F32)
        o_ref[...] = _pack_bf16_pair(y[:, :half], y[:, half:])

    @pl.when((i < n_used) & (i < last))
    def _():
        compute(True)

    @pl.when((i < n_used) & (i == last))
    def _():
        compute(False)

    @pl.when(i >= n_used)
    def _():
        o_ref[...] = jnp.zeros_like(o_ref)

        @pl.when(i == n_used)
        def _():
            wait_block(i % 2)


def _experts(x_packed, sorted_tok, block_e, src, n_used, wg, wu, wd, layer):
    w_idx = lambda i, be, nu, src, stok: (layer, be[i], 0, 0)
    w_up = pl.BlockSpec((None, None, D_MODEL, F_EXPERT), w_idx)
    return pl.pallas_call(
        _expert_kernel,
        grid_spec=pltpu.PrefetchScalarGridSpec(
            num_scalar_prefetch=4,
            grid=(MOE_NB,),
            in_specs=[pl.BlockSpec(memory_space=pl.ANY), w_up, w_up,
                      pl.BlockSpec((None, None, F_EXPERT, D_MODEL), w_idx)],
            out_specs=pl.BlockSpec((MOE_MB, MOE_PACKED_W), lambda i, be, nu, src, stok: (i, 0)),
            scratch_shapes=[pltpu.VMEM((2, MOE_MB, MOE_PACKED_W), jnp.uint32),
                            pltpu.VMEM((2, MOE_MB, MOE_PACKED_W), BF16),
                            pltpu.VMEM((D_MODEL, F_EXPERT), BF16), pltpu.VMEM((D_MODEL, F_EXPERT), BF16),
                            pltpu.VMEM((F_EXPERT, D_MODEL), BF16), pltpu.SemaphoreType.DMA((2,))]),
        out_shape=jax.ShapeDtypeStruct((MOE_SLOTS, MOE_PACKED_W), jnp.uint32),
        compiler_params=_cparams(1), name="moe_experts",
    )(block_e, n_used, src, sorted_tok, x_packed, wg, wu, wd)


def _shared_up_kernel(x_ref, wg_ref, wu_ref, o_ref, wg_bf, wu_bf):
    @pl.when(pl.program_id(0) == 0)
    def _():
        wg_bf[...] = wg_ref[...].astype(BF16)
        wu_bf[...] = wu_ref[...].astype(BF16)

    x = x_ref[...].astype(BF16)
    g = jnp.dot(x, wg_bf[...], preferred_element_type=F32)
    u = jnp.dot(x, wu_bf[...], preferred_element_type=F32)
    o_ref[...] = (_silu(g) * u).astype(o_ref.dtype)


def _shared_up(h, wg, wu, layer):
    w = pl.BlockSpec((None, D_MODEL, F_SHARED), lambda i: (layer, 0, 0))
    return pl.pallas_call(
        _shared_up_kernel,
        grid=(T_ALL // TM,),
        in_specs=[pl.BlockSpec((TM, D_MODEL), lambda i: (i, 0)), w, w],
        out_specs=pl.BlockSpec((TM, F_SHARED), lambda i: (i, 0)),
        out_shape=jax.ShapeDtypeStruct((T_ALL, F_SHARED), BF16),
        scratch_shapes=[pltpu.VMEM((D_MODEL, F_SHARED), BF16), pltpu.VMEM((D_MODEL, F_SHARED), BF16)],
        compiler_params=_cparams(1), name="shared_up",
    )(h, wg, wu)


def _combine_kernel(slot_ref, nxt_ref, y_hbm, rw_ref, act_ref, wd_ref, x_ref, mod_ref, o_ref, buf, wd_bf, sems, *,
                    gate_row):
    i = pl.program_id(0)
    n_rows = TOP_K * CMB_TM
    half = MOE_PACKED_W

    def row_copy(idx_ref, slot, j):
        return pltpu.make_async_copy(y_hbm.at[pl.ds(idx_ref[0, 0, j], 1), :], buf.at[slot, pl.ds(j, 1), :],
                                     sems.at[slot])

    @pl.when(i == 0)
    def _():
        wd_bf[...] = wd_ref[...].astype(BF16)

        def body(j, carry):
            row_copy(slot_ref, 0, j).start()
            return carry
        lax.fori_loop(0, n_rows, body, 0, unroll=8)

    def tile(prefetch):
        slot = i % 2
        pltpu.make_async_copy(y_hbm.at[pl.ds(0, n_rows), :], buf.at[slot], sems.at[slot]).wait()
        if prefetch:
            for j in range(n_rows):
                row_copy(nxt_ref, 1 - slot, j).start()
        shared = jnp.dot(act_ref[...], wd_bf[...], preferred_element_type=F32)
        lo = shared[:, :half]
        hi = shared[:, half:]
        for k in range(TOP_K):
            row_lo, row_hi = _unpack_bf16_pair(buf[slot, k * CMB_TM:(k + 1) * CMB_TM, :])
            lo = lo + rw_ref[:, k:k + 1] * row_lo
            hi = hi + rw_ref[:, k:k + 1] * row_hi
        gate = mod_ref[gate_row:gate_row + 1, :]
        o_ref[:, :half] = x_ref[:, :half] + gate[:, :half] * lo
        o_ref[:, half:] = x_ref[:, half:] + gate[:, half:] * hi

    @pl.when(i + 1 < pl.num_programs(0))
    def _():
        tile(True)

    @pl.when(i + 1 == pl.num_programs(0))
    def _():
        tile(False)


def _combine(y_sorted, slot_of, route_w, act, sh_wd, x, mod, layer):
    nt = T_ALL // CMB_TM
    slots = slot_of.reshape(nt, CMB_TM, TOP_K).transpose(0, 2, 1).reshape(nt, 1, TOP_K * CMB_TM)
    slot_spec = lambda nxt: pl.BlockSpec((1, 1, TOP_K * CMB_TM), lambda i: (jnp.minimum(i + nxt, nt - 1), 0, 0),
                                         memory_space=pltpu.SMEM)
    return pl.pallas_call(
        functools.partial(_combine_kernel, gate_row=5),
        grid=(nt,),
        in_specs=[slot_spec(0), slot_spec(1),
                  pl.BlockSpec(memory_space=pl.ANY),
                  pl.BlockSpec((CMB_TM, TOP_K), lambda i: (i, 0)),
                  pl.BlockSpec((CMB_TM, F_SHARED), lambda i: (i, 0)),
                  pl.BlockSpec((None, F_SHARED, D_MODEL), lambda i: (layer, 0, 0)),
                  pl.BlockSpec((CMB_TM, D_MODEL), lambda i: (i, 0)),
                  pl.BlockSpec((None, 6, D_MODEL), lambda i: (_group_of_tile(i, CMB_TM), 0, 0))],
        out_specs=pl.BlockSpec((CMB_TM, D_MODEL), lambda i: (i, 0)),
        out_shape=jax.ShapeDtypeStruct((T_ALL, D_MODEL), F32),
        scratch_shapes=[pltpu.VMEM((2, TOP_K * CMB_TM, MOE_PACKED_W), jnp.uint32),
                        pltpu.VMEM((F_SHARED, D_MODEL), BF16), pltpu.SemaphoreType.DMA((2,))],
        compiler_params=_cparams(1), name="moe_combine",
    )(slots, slots, y_sorted, route_w, act, sh_wd, x, mod)


def kernel(x_prompt, x_sample, cache_na_k, cache_na_v, state_gla, state_s5, c, c_ctx, ada_w, ada_b, norm1_g, norm2_g, w_in, na_rpb, gla_wa, gla_ba, gla_norm_g, s5_a_re, s5_a_im, s5_log_dt, s5_b_re, s5_b_im, s5_c_re, s5_c_im, s5_d, s5_w_glu, w_br_na, w_br_gla, w_br_s5, w_merge, w_out, router_w, router_bias, exp_wg, exp_wu, exp_wd, sh_wg, sh_wu, sh_wd, final_norm_g):
    x = jnp.concatenate([x_prompt.reshape(T_CTX, D_MODEL), x_sample.reshape(T_LAT, D_MODEL)], axis=0)
    cvec = jnp.concatenate([c_ctx[None, :], c, jnp.zeros((MOD_ROWS - N_MOD, D_MODEL), F32)], axis=0)
    mods = _ada(cvec, ada_w, ada_b).reshape(DEPTH, MOD_ROWS, 6, D_MODEL)
    cos, sin = _rope_tables()
    w_in_t = jnp.swapaxes(w_in, 1, 2)
    zero_gla = jnp.zeros((BATCH, 2, GLA_HEADS, GLA_DV, GLA_DK), F32)
    zero_s5 = jnp.zeros((S5_GROUPS, 4, BATCH, S5_N), F32)
    new_k, new_v, new_gla, new_s5 = [], [], [], []
    for l in range(DEPTH):
        mod = mods[l]
        h = _norm(x, norm1_g[l], mod, (0, 1), BF16)
        u = _mm(h, w_in_t, l, 1024, n=IN_MAIN, w_rows_are_outputs=True)
        lr = _mm(h, w_in_t[l, IN_MAIN:IN_MAIN + 2 * GLA_LR], None, 2 * GLA_LR, w_rows_are_outputs=True)
        su = _mm(h, w_in_t[l, IN_MAIN + 2 * GLA_LR:], None, S5_W, w_rows_are_outputs=True)
        gates = _mm(h, w_merge, l, 1024, act="sigmoid")

        o_na = jnp.concatenate([_ctx_attn(u),
                                _na_attn(u, cache_na_k, cache_na_v, _na_bias(na_rpb[l]), l)], axis=0)

        s0_lat = state_gla[:, l].transpose(0, 1, 2, 4, 3)
        og_ctx, sfin = _gla(u, lr, gla_wa, gla_ba, gla_norm_g, zero_gla, cos, sin, l, latent=False)
        og_lat, _ = _gla(u, lr, gla_wa, gla_ba, gla_norm_g, s0_lat, cos, sin, l, latent=True)
        o_gla = jnp.concatenate([og_ctx, og_lat], axis=0)

        ops, step = _s5_operators(s5_a_re[l], s5_a_im[l], s5_log_dt[l], s5_b_re[l], s5_b_im[l],
                                  s5_c_re[l], s5_c_im[l])
        st = state_s5[:, l].astype(F32)
        h0_lat = st.transpose(2, 1, 4, 0, 3).reshape(S5_GROUPS, 4, DEC_BATCH, S5_N)
        y_ctx, hfin = _s5(su, ops, step, zero_s5, nb=BATCH, seq=SEQ, first_row_block=0)
        y_lat, _ = _s5(su, ops, step, h0_lat, nb=DEC_BATCH, seq=DEC_SEQ, first_row_block=T_CTX // T_LAT)
        zz = _glu(jnp.concatenate([y_ctx, y_lat], axis=0), su, s5_d, s5_w_glu, l)

        merged = _merge(o_na, o_gla, zz, gates, w_br_na, w_br_gla, w_br_s5, l)
        x = _out_proj(merged, w_out, x, mod, l)

        h2, h2_packed = _norm(x, norm2_g[l], mod, (3, 4), F32, packed=True)
        idx_t, w_t = _router(h2, router_w[l].T, router_bias[l])
        sorted_tok, slot_of, block_e, src, n_used = _dispatch(idx_t.T)
        y_sorted = _experts(h2_packed, sorted_tok, block_e, src, n_used, exp_wg, exp_wu, exp_wd, l)
        act = _shared_up(h2, sh_wg, sh_wu, l)
        x = _combine(y_sorted, slot_of, w_t.T, act, sh_wd, x, mod, l)

        new_k.append(u[:T_CTX, NA_W:2 * NA_W].reshape(BATCH, SEQ, NA_HEADS, NA_DH))
        new_v.append(u[:T_CTX, 2 * NA_W:3 * NA_W].reshape(BATCH, SEQ, NA_HEADS, NA_DH))
        new_gla.append(sfin.transpose(0, 1, 2, 4, 3))
        new_s5.append(hfin.reshape(S5_GROUPS, 2, 2, BATCH, S5_N).transpose(3, 1, 0, 4, 2))

    y = _norm(x, final_norm_g, None, None, F32)
    return (y[:T_CTX].reshape(BATCH, SEQ, D_MODEL), y[T_CTX:].reshape(DEC_BATCH, DEC_SEQ, D_MODEL),
            jnp.stack(new_k, axis=1), jnp.stack(new_v, axis=1),
            jnp.stack(new_gla, axis=1), jnp.stack(new_s5, axis=1))
```

```python
import functools
import math

import jax
import jax.numpy as jnp
from jax import lax
from jax.experimental import pallas as pl
from jax.experimental.pallas import tpu as pltpu

F32 = jnp.float32
BF16 = jnp.bfloat16

D_MODEL = 2048
BATCH = 16
SEQ = 256
DEPTH = 2
DEC_BATCH = 2
DEC_SEQ = 1024
PAST_LEN = 512
GRID_W = 64
NA_HEADS = 8
NA_DH = 128
NA_W = NA_HEADS * NA_DH
NA_ROWS = 8
NA_COLS = 16
GLA_HEADS = 4
GLA_DK = 128
GLA_DV = 256
GLA_QK_W = GLA_HEADS * GLA_DK
GLA_V_W = GLA_HEADS * GLA_DV
GLA_LR = 16
GLA_TAU = 16.0
GLA_CHUNK = 64
GLA_SUB = 16
GLA_UNROLL = 4
ROPE_BASE = 10000.0
S5_W = 1024
S5_GS = 16
S5_GROUPS = S5_W // S5_GS
S5_N = 64
S5_T = 16
IN_W = 3 * NA_W + 2 * GLA_QK_W + 2 * GLA_V_W + 2 * GLA_LR + S5_W
IN_MAIN = 3 * NA_W + 2 * GLA_QK_W + 2 * GLA_V_W
N_EXPERTS = 64
TOP_K = 8
N_GROUPS = 8
TOPK_GROUPS = 4
F_EXPERT = 512
F_SHARED = 512
ROUTED_SCALE = 2.5
EPS = 1e-6

T_CTX = BATCH * SEQ
T_LAT = DEC_BATCH * DEC_SEQ
T_ALL = T_CTX + T_LAT
N_MOD = 1 + DEC_BATCH
MOD_ROWS = 8

TM = 512
MM_TM = 1024
MOE_MB = 256
MOE_NB = T_ALL * TOP_K // MOE_MB + N_EXPERTS
MOE_SLOTS = MOE_NB * MOE_MB
CMB_TM = 128
NEG = -1e30

VMEM_LIMIT = 56 * 1024 * 1024

NT = (((1,), (1,)), ((), ()))
TN = (((0,), (0,)), ((), ()))


def _cparams(n_axes):
    return pltpu.CompilerParams(dimension_semantics=("arbitrary",) * n_axes,
                                vmem_limit_bytes=VMEM_LIMIT)


def _group_of_tile(i, tm):
    row = i * tm
    return jnp.where(row < T_CTX, 0, 1 + (row - T_CTX) // DEC_SEQ)


def _silu(x):
    return x * jax.nn.sigmoid(x)


def _gelu_tanh(x):
    return 0.5 * x * (1.0 + jnp.tanh(math.sqrt(2.0 / math.pi) * (x + 0.044715 * (x * x * x))))


def _ada_kernel(c_ref, w_ref, b_ref, o_ref):
    s = _silu(c_ref[...]).astype(BF16)
    o_ref[...] = jnp.dot(s, w_ref[...].astype(BF16), preferred_element_type=F32) + b_ref[...]


def _ada(cvec, ada_w, ada_b):
    tn = 1024
    return pl.pallas_call(
        _ada_kernel,
        grid=(DEPTH, 6 * D_MODEL // tn),
        in_specs=[pl.BlockSpec((MOD_ROWS, D_MODEL), lambda l, j: (0, 0)),
                  pl.BlockSpec((None, D_MODEL, tn), lambda l, j: (l, 0, j)),
                  pl.BlockSpec((None, 1, tn), lambda l, j: (l, 0, j))],
        out_specs=pl.BlockSpec((None, MOD_ROWS, tn), lambda l, j: (l, 0, j)),
        out_shape=jax.ShapeDtypeStruct((DEPTH, MOD_ROWS, 6 * D_MODEL), F32),
        compiler_params=_cparams(2), name="ada",
    )(cvec, ada_w, ada_b.reshape(DEPTH, 1, 6 * D_MODEL))


def _pack_bf16_pair(lo, hi):
    bits = lambda v: pltpu.bitcast(v.astype(BF16).astype(F32), jnp.uint32)
    return (bits(lo) >> 16) | (bits(hi) & jnp.uint32(0xFFFF0000))


def _unpack_bf16_pair(w):
    return pltpu.bitcast(w << 16, F32), pltpu.bitcast(w & jnp.uint32(0xFFFF0000), F32)


def _norm_kernel(x_ref, g_ref, *rest, rows, packed):
    x = x_ref[...]
    y = x * lax.rsqrt(jnp.mean(x * x, axis=-1, keepdims=True) + EPS) * g_ref[...]
    if rows is not None:
        mod_ref = rest[0]
        y = y * (1.0 + mod_ref[rows[1]:rows[1] + 1, :]) + mod_ref[rows[0]:rows[0] + 1, :]
    if packed:
        rest[-2][...] = y
        rest[-1][...] = _pack_bf16_pair(y[:, :D_MODEL // 2], y[:, D_MODEL // 2:])
    else:
        rest[-1][...] = y.astype(rest[-1].dtype)


def _norm(x, g, mod, rows, out_dtype, packed=False):
    tm = 256
    in_specs = [pl.BlockSpec((tm, D_MODEL), lambda i: (i, 0)),
                pl.BlockSpec((1, D_MODEL), lambda i: (0, 0))]
    args = [x, g.reshape(1, D_MODEL)]
    if rows is not None:
        in_specs.append(pl.BlockSpec((None, 6, D_MODEL), lambda i: (_group_of_tile(i, tm), 0, 0)))
        args.append(mod)
    out_specs = pl.BlockSpec((tm, D_MODEL), lambda i: (i, 0))
    out_shape = jax.ShapeDtypeStruct((T_ALL, D_MODEL), out_dtype)
    if packed:
        out_specs = [out_specs, pl.BlockSpec((tm, D_MODEL // 2), lambda i: (i, 0))]
        out_shape = [out_shape, jax.ShapeDtypeStruct((T_ALL, D_MODEL // 2), jnp.uint32)]
    return pl.pallas_call(
        functools.partial(_norm_kernel, rows=rows, packed=packed),
        grid=(T_ALL // tm,),
        in_specs=in_specs,
        out_specs=out_specs,
        out_shape=out_shape,
        compiler_params=_cparams(1), name="norm",
    )(*args)


def _mm_kernel(x_ref, w_ref, o_ref, wbf_ref, *, act, w_rows_are_outputs):
    @pl.when(pl.program_id(1) == 0)
    def _():
        wbf_ref[...] = w_ref[...].astype(BF16)

    x = x_ref[...].astype(BF16)
    if w_rows_are_outputs:
        acc = lax.dot_general(x, wbf_ref[...], NT, preferred_element_type=F32)
    else:
        acc = jnp.dot(x, wbf_ref[...], preferred_element_type=F32)
    if act == "sigmoid":
        acc = jax.nn.sigmoid(acc)
    o_ref[...] = acc.astype(o_ref.dtype)


def _mm(x, w, layer, tn, act=None, out_dtype=F32, n=None, w_rows_are_outputs=False):
    k = x.shape[1]
    if n is None:
        n = w.shape[-2] if w_rows_are_outputs else w.shape[-1]
    blk, idx = ((tn, k), lambda j: (j, 0)) if w_rows_are_outputs else ((k, tn), lambda j: (0, j))
    if layer is None:
        w_spec = pl.BlockSpec(blk, lambda j, i: idx(j))
    else:
        w_spec = pl.BlockSpec((None,) + blk, lambda j, i: (layer,) + idx(j))
    return pl.pallas_call(
        functools.partial(_mm_kernel, act=act, w_rows_are_outputs=w_rows_are_outputs),
        grid=(n // tn, T_ALL // MM_TM),
        in_specs=[pl.BlockSpec((MM_TM, k), lambda j, i: (i, 0)), w_spec],
        out_specs=pl.BlockSpec((MM_TM, tn), lambda j, i: (i, j)),
        out_shape=jax.ShapeDtypeStruct((T_ALL, n), out_dtype),
        scratch_shapes=[pltpu.VMEM(blk, BF16)],
        compiler_params=_cparams(2), name="mm",
    )(x, w)


def _ctx_attn_kernel(q_ref, k_ref, v_ref, o_ref):
    scale = NA_DH ** -0.5
    for h in range(NA_HEADS):
        sl = slice(h * NA_DH, (h + 1) * NA_DH)
        q = q_ref[:, sl].astype(BF16)
        k = k_ref[:, sl].astype(BF16)
        v = v_ref[:, sl].astype(BF16)
        s = lax.dot_general(q, k, NT, preferred_element_type=F32) * scale
        p = jnp.exp(s - jnp.max(s, axis=-1, keepdims=True))
        o = jnp.dot(p.astype(BF16), v, preferred_element_type=F32) / jnp.sum(p, axis=-1, keepdims=True)
        o_ref[:, sl] = o.astype(o_ref.dtype)


def _ctx_attn(u):
    spec = lambda cb: pl.BlockSpec((SEQ, NA_W), lambda b: (b, cb))
    return pl.pallas_call(
        _ctx_attn_kernel,
        grid=(BATCH,),
        in_specs=[spec(0), spec(1), spec(2)],
        out_specs=pl.BlockSpec((SEQ, NA_W), lambda b: (b, 0)),
        out_shape=jax.ShapeDtypeStruct((T_CTX, NA_W), BF16),
        compiler_params=_cparams(1), name="ctx_attn",
    )(u, u, u)


NA_GRID_ROWS = DEC_SEQ // GRID_W
NA_KR = min(NA_ROWS, NA_GRID_ROWS)
NA_LOC = NA_KR * GRID_W


def _na_bias(rpb):
    n_off = 2 * NA_ROWS - 1
    pad = GRID_W - NA_COLS
    period = 2 * GRID_W
    p = jnp.pad(rpb, ((0, 0), (0, 0), (pad, period - pad - (2 * NA_COLS - 1))))
    hank = jnp.tile(p, (1, 1, GRID_W + 1))[..., :GRID_W * (period + 1)]
    hank = hank.reshape(NA_HEADS, n_off, GRID_W, period + 1)[..., :GRID_W]
    band = hank[:, :, ::-1, :]
    qc = jnp.arange(GRID_W)[:, None]
    kc = jnp.arange(GRID_W)[None, :]
    win = jnp.clip(qc - NA_COLS // 2, 0, GRID_W - NA_COLS)
    valid = (kc >= win) & (kc < win + NA_COLS)
    band = jnp.where(valid[None, None], band, NEG)
    return band.transpose(0, 2, 1, 3).reshape(NA_HEADS, GRID_W, n_off * GRID_W)


def _na_kernel(q_ref, k_ref, v_ref, ck_ref, cv_ref, bias_ref, o_ref):
    scale = NA_DH ** -0.5
    kb = k_ref[...].astype(BF16)
    vb = v_ref[...].astype(BF16)
    ck = ck_ref[...].astype(BF16)
    cv = cv_ref[...].astype(BF16)
    for r in range(NA_GRID_ROWS):
        first = min(max(r - NA_KR // 2, 0), NA_GRID_ROWS - NA_KR)
        off = first - r + NA_ROWS - 1
        rows = slice(r * GRID_W, (r + 1) * GRID_W)
        keys = slice(first * GRID_W, first * GRID_W + NA_LOC)
        q = q_ref[rows, :].astype(BF16)
        s1 = (lax.dot_general(q, kb[keys, :], NT, preferred_element_type=F32) * scale
              + bias_ref[:, off * GRID_W:off * GRID_W + NA_LOC])
        s2 = lax.dot_general(q, ck, NT, preferred_element_type=F32) * scale
        m = jnp.maximum(jnp.max(s1, axis=-1, keepdims=True), jnp.max(s2, axis=-1, keepdims=True))
        p1 = jnp.exp(s1 - m)
        p2 = jnp.exp(s2 - m)
        den = jnp.sum(p1, axis=-1, keepdims=True) + jnp.sum(p2, axis=-1, keepdims=True)
        o = (jnp.dot(p1.astype(BF16), vb[keys, :], preferred_element_type=F32)
             + jnp.dot(p2.astype(BF16), cv, preferred_element_type=F32))
        o_ref[rows, :] = (o / den).astype(o_ref.dtype)


def _na_attn(u, cache_k, cache_v, bias, layer):
    lat_sb = T_CTX // DEC_SEQ
    ck = cache_k.reshape(DEC_BATCH, DEPTH, PAST_LEN, NA_W)
    cv = cache_v.reshape(DEC_BATCH, DEPTH, PAST_LEN, NA_W)
    qkv_spec = lambda cb: pl.BlockSpec((DEC_SEQ, NA_DH), lambda b, h: (lat_sb + b, cb * NA_HEADS + h))
    c_spec = pl.BlockSpec((None, None, PAST_LEN, NA_DH), lambda b, h: (b, layer, 0, h))
    return pl.pallas_call(
        _na_kernel,
        grid=(DEC_BATCH, NA_HEADS),
        in_specs=[qkv_spec(0), qkv_spec(1), qkv_spec(2), c_spec, c_spec,
                  pl.BlockSpec((None, GRID_W, (2 * NA_ROWS - 1) * GRID_W), lambda b, h: (h, 0, 0))],
        out_specs=pl.BlockSpec((DEC_SEQ, NA_DH), lambda b, h: (b, h)),
        out_shape=jax.ShapeDtypeStruct((T_LAT, NA_W), BF16),
        compiler_params=_cparams(2), name="na_attn",
    )(u, u, u, ck, cv, bias)


def _rope_tables():
    half = GLA_DK // 2
    nf = half // 2
    t = jnp.arange(DEC_SEQ)
    freqs = ROPE_BASE ** (-jnp.arange(nf, dtype=F32) / nf)
    ang_r = (t // GRID_W).astype(F32)[:, None] * freqs
    ang_c = (t % GRID_W).astype(F32)[:, None] * freqs
    cos = jnp.concatenate([jnp.cos(ang_r), jnp.cos(ang_r), jnp.cos(ang_c), jnp.cos(ang_c)], axis=-1)
    sin = jnp.concatenate([-jnp.sin(ang_r), jnp.sin(ang_r), -jnp.sin(ang_c), jnp.sin(ang_c)], axis=-1)
    return cos, sin


def _split_bf16(x):
    hi = x.astype(BF16)
    return hi, (x - hi.astype(F32)).astype(BF16)


def _gla_kernel(q_ref, k_ref, v_ref, gg_ref, lr_ref, wa_ref, ba_ref, ng_ref, cos_ref, sin_ref, s0_ref,
                o_ref, sfin_ref, qs, ks, las, o_acc, st, *, seq, rope):
    nc = seq // GLA_CHUNK
    nq = GLA_DK // 4

    def rot(x):
        lane = lax.broadcasted_iota(jnp.int32, x.shape, 1)
        partner = jnp.where((lane % (2 * nq)) < nq,
                            pltpu.roll(x, GLA_DK - nq, axis=1), pltpu.roll(x, nq, axis=1))
        return x * cos_ref[...] + partner * sin_ref[...]

    q = q_ref[...] * (GLA_DK ** -0.5)
    k = k_ref[...]
    if rope:
        q = rot(q)
        k = rot(k)
    qs[...] = q
    ks[...] = k
    lr = lr_ref[...]
    for d in range(2):
        z = jnp.dot(lr[:, d * GLA_LR:(d + 1) * GLA_LR].astype(BF16), wa_ref[d].astype(BF16),
                    preferred_element_type=F32) + ba_ref[d:d + 1, :]
        las[d] = -(jnp.maximum(-z, 0.0) + jnp.log1p(jnp.exp(-jnp.abs(z)))) / GLA_TAU
        st[d] = s0_ref[d]

    row = lax.broadcasted_iota(jnp.int32, (GLA_CHUNK, GLA_CHUNK), 0)
    col = lax.broadcasted_iota(jnp.int32, (GLA_CHUNK, GLA_CHUNK), 1)
    key_row = lax.broadcasted_iota(jnp.int32, (GLA_CHUNK, 1), 0)

    def chunk(c, d):
        rev = d == 1
        rows = pl.ds(pl.multiple_of(c * GLA_CHUNK, GLA_CHUNK), GLA_CHUNK)
        qc = qs[rows, :]
        kc = ks[rows, :]
        vc = v_ref[rows, :].astype(BF16)
        la = las[d, rows, :]
        causal = (col >= row) if rev else (col <= row)
        tri = jnp.where(causal, 1.0, 0.0).astype(BF16)
        la_hi, la_lo = _split_bf16(la)
        b = (jnp.dot(tri, la_hi, preferred_element_type=F32)
             + jnp.dot(tri, la_lo, preferred_element_type=F32))
        bex = b - la
        b_last = b[0:1, :] if rev else b[GLA_CHUNK - 1:GLA_CHUNK, :]
        blocks = []
        for i in range(GLA_CHUNK // GLA_SUB):
            lo, hi = i * GLA_SUB, (i + 1) * GLA_SUB
            ref = bex[hi - 1:hi, :] if rev else bex[lo:lo + 1, :]
            qt = (qc[lo:hi, :] * jnp.exp(b[lo:hi, :] - ref)).astype(BF16)
            allowed = (key_row >= lo) if rev else (key_row < hi)
            kt = (kc * jnp.exp(jnp.where(allowed, ref - b, -jnp.inf))).astype(BF16)
            blocks.append(lax.dot_general(qt, kt, NT, preferred_element_type=F32))
        att = jnp.where(causal, jnp.concatenate(blocks, axis=0), 0.0)
        s_t = st[d]
        o = (jnp.dot(att.astype(BF16), vc, preferred_element_type=F32)
             + lax.dot_general((qc * jnp.exp(b)).astype(BF16), s_t.astype(BF16), NT,
                               preferred_element_type=F32))
        khat = (kc * jnp.exp(b_last - b)).astype(BF16)
        st[d] = s_t * jnp.exp(b_last) + lax.dot_general(vc, khat, TN, preferred_element_type=F32)
        o_acc[d, rows, :] = o

    def both(c, carry):
        chunk(c, 0)
        chunk(nc - 1 - c, 1)
        return carry

    lax.fori_loop(0, nc, both, 0, unroll=GLA_UNROLL)
    sfin_ref[...] = st[...]
    o = o_acc[0] + o_acc[1]
    o = o * lax.rsqrt(jnp.mean(o * o, axis=-1, keepdims=True) + EPS) * ng_ref[...]
    o_ref[...] = (o * _silu(gg_ref[...])).astype(o_ref.dtype)


def _gla(u, lr, wa, ba, norm_g, s0_t, cos, sin, layer, *, latent):
    seq, nb, first = (DEC_SEQ, DEC_BATCH, T_CTX // DEC_SEQ) if latent else (SEQ, BATCH, 0)
    qk_cb = 3 * NA_W // GLA_DK
    v_cb = (3 * NA_W + 2 * GLA_QK_W) // GLA_DV
    row = lambda w, cb: pl.BlockSpec((seq, w), lambda b, h: (first + b, cb + h))
    return pl.pallas_call(
        functools.partial(_gla_kernel, seq=seq, rope=latent),
        grid=(nb, GLA_HEADS),
        in_specs=[row(GLA_DK, qk_cb), row(GLA_DK, qk_cb + GLA_HEADS), row(GLA_DV, v_cb),
                  row(GLA_DV, v_cb + GLA_HEADS),
                  pl.BlockSpec((seq, 2 * GLA_LR), lambda b, h: (first + b, 0)),
                  pl.BlockSpec((None, 2, GLA_LR, GLA_DK), lambda b, h: (layer, 0, 0, h)),
                  pl.BlockSpec((None, 2, GLA_DK), lambda b, h: (layer, 0, h)),
                  pl.BlockSpec((None, 1, GLA_DV), lambda b, h: (layer, 0, h)),
                  pl.BlockSpec((seq, GLA_DK), lambda b, h: (0, 0)),
                  pl.BlockSpec((seq, GLA_DK), lambda b, h: (0, 0)),
                  pl.BlockSpec((None, 2, None, GLA_DV, GLA_DK), lambda b, h: (b, 0, h, 0, 0))],
        out_specs=[pl.BlockSpec((seq, GLA_DV), lambda b, h: (b, h)),
                   pl.BlockSpec((None, 2, None, GLA_DV, GLA_DK), lambda b, h: (b, 0, h, 0, 0))],
        out_shape=[jax.ShapeDtypeStruct((nb * seq, GLA_V_W), BF16),
                   jax.ShapeDtypeStruct((nb, 2, GLA_HEADS, GLA_DV, GLA_DK), F32)],
        scratch_shapes=[pltpu.VMEM((seq, GLA_DK), F32), pltpu.VMEM((seq, GLA_DK), F32),
                        pltpu.VMEM((2, seq, GLA_DK), F32), pltpu.VMEM((2, seq, GLA_DV), F32),
                        pltpu.VMEM((2, GLA_DV, GLA_DK), F32)],
        compiler_params=_cparams(2), name="gla",
    )(u, u, u, u, lr, wa, ba, norm_g.reshape(DEPTH, 1, GLA_V_W), cos[:seq], sin[:seq], s0_t)


def _s5_operators(a_re, a_im, log_dt, b_re, b_im, c_re, c_im):
    hp = lax.Precision.HIGHEST
    lam = lax.complex(a_re.astype(F32), a_im.astype(F32))
    lam_dt = lam * jnp.exp(log_dt.astype(F32))[..., None]
    a_bar = jnp.exp(lam_dt)
    b_bar = ((a_bar - 1.0) / lam)[..., None] * lax.complex(b_re.astype(F32), b_im.astype(F32))
    cc = lax.complex(c_re.astype(F32), c_im.astype(F32))
    taus = jnp.arange(S5_T + 1, dtype=F32)
    pw = jnp.exp(lam_dt[:, :, None, :] * taus[None, None, :, None])
    lanes = S5_T * S5_GS

    def out_weights(pwd, cd):
        m = pwd.transpose(0, 2, 1)[:, :, :, None] * cd.transpose(0, 2, 1)[:, :, None, :]
        return m.reshape(S5_GROUPS, S5_N, lanes)

    def lag_response(d):
        r = out_weights(pw[d, :, :S5_T], cc[d])
        bt = b_bar[d].transpose(0, 2, 1)
        return (jnp.einsum("gin,gnx->gix", jnp.real(bt), jnp.real(r), precision=hp)
                - jnp.einsum("gin,gnx->gix", jnp.imag(bt), jnp.imag(r), precision=hp))

    def toeplitz(lags):
        rows = [lags[:, :, (S5_T - 1 - s) * S5_GS:(2 * S5_T - 1 - s) * S5_GS] for s in range(S5_T)]
        return jnp.stack(rows, axis=1).reshape(S5_GROUPS, lanes, lanes)

    no_lag = jnp.zeros((S5_GROUPS, S5_GS, (S5_T - 1) * S5_GS), F32)
    rev_lags = lag_response(1).reshape(S5_GROUPS, S5_GS, S5_T, S5_GS)[:, :, ::-1].reshape(S5_GROUPS, S5_GS, lanes)
    toep_f = toeplitz(jnp.concatenate([no_lag, lag_response(0)], axis=-1))
    toep_b = toeplitz(jnp.concatenate([rev_lags, no_lag], axis=-1))

    def state_in(pwd, bd):
        m = pwd[:, :, None, :] * bd.transpose(0, 2, 1)[:, None, :, :]
        m = m.reshape(S5_GROUPS, lanes, S5_N)
        return [jnp.real(m), jnp.imag(m)]

    p = jnp.stack(state_in(pw[0, :, S5_T - 1::-1], b_bar[0])
                  + state_in(pw[1, :, :S5_T], b_bar[1]), axis=1)

    def state_out(pwd, cd):
        m = out_weights(pwd, cd)
        return [jnp.real(m), -jnp.imag(m)]

    q = jnp.stack(state_out(pw[0, :, 1:S5_T + 1], cc[0])
                  + state_out(pw[1, :, S5_T:0:-1], cc[1]), axis=1)
    ops = [m.astype(BF16) for m in (toep_f, toep_b, p, q)]
    a_t = pw[:, :, S5_T]
    step = jnp.stack([jnp.real(a_t[0]), jnp.imag(a_t[0]), jnp.real(a_t[1]), jnp.imag(a_t[1])], axis=1)
    return ops, step


S5_GB = 128 // S5_GS
S5_SEG = 128 // S5_GS


def _s5_kernel(x_ref, tf_ref, tb_ref, p_ref, q_ref, step_ref, h0_ref, y_ref, hfin_ref,
               u_scr, y_scr, e_scr, hin_scr, *, nc, nb):
    n_chunks = nb * nc
    for s in range(S5_T):
        xs = x_ref[pl.ds(s, n_chunks, stride=S5_T), :]
        half, pos = divmod(s, S5_SEG)
        dst = slice(half * 128 + pos * S5_GS, half * 128 + (pos + 1) * S5_GS)
        for g in range(S5_GB):
            moved = pltpu.roll(xs, ((pos - g) % S5_SEG) * S5_GS, axis=1)
            u_scr[g, :, dst] = moved[:, pos * S5_GS:(pos + 1) * S5_GS].astype(BF16)

    def group(g, carry):
        u = u_scr[g]
        for m in range(4):
            e_scr[m] = jnp.dot(u, p_ref[g, m], preferred_element_type=F32)
        ar = [step_ref[g, 0:1, :], step_ref[g, 2:3, :]]
        ai = [step_ref[g, 1:2, :], step_ref[g, 3:4, :]]
        hr = [h0_ref[g, 0], h0_ref[g, 2]]
        hi = [h0_ref[g, 1], h0_ref[g, 3]]
        for c in range(nc):
            for d in range(2):
                rows = pl.ds(c if d == 0 else nc - 1 - c, nb, stride=nc)
                hin_scr[2 * d, rows, :] = hr[d]
                hin_scr[2 * d + 1, rows, :] = hi[d]
                hr[d], hi[d] = (hr[d] * ar[d] - hi[d] * ai[d] + e_scr[2 * d, rows, :],
                                hi[d] * ar[d] + hr[d] * ai[d] + e_scr[2 * d + 1, rows, :])
        for d in range(2):
            hfin_ref[g, 2 * d] = hr[d]
            hfin_ref[g, 2 * d + 1] = hi[d]
        y = (jnp.dot(u, tf_ref[g], preferred_element_type=F32)
             + jnp.dot(u, tb_ref[g], preferred_element_type=F32))
        for m in range(4):
            y += jnp.dot(hin_scr[m].astype(BF16), q_ref[g, m], preferred_element_type=F32)
        y_scr[g] = y
        return carry

    lax.fori_loop(0, S5_GB, group, 0)

    segment = lax.broadcasted_iota(jnp.int32, (n_chunks, 128), 1) // S5_GS
    for t in range(S5_T):
        half, pos = divmod(t, S5_SEG)
        rows = jnp.zeros((n_chunks, 128), F32)
        for g in range(S5_GB):
            moved = pltpu.roll(y_scr[g, :, half * 128:(half + 1) * 128], ((g - pos) % S5_SEG) * S5_GS, axis=1)
            rows = jnp.where(segment == g, moved, rows)
        y_ref[pl.ds(t, n_chunks, stride=S5_T), :] = rows


def _s5(su, ops, step, h0, *, nb, seq, first_row_block):
    nc = seq // S5_T
    lanes = S5_T * S5_GS
    n_chunks = nb * nc
    gb = lambda *shape: pl.BlockSpec((S5_GB,) + shape, lambda g: (g,) + (0,) * len(shape))
    return pl.pallas_call(
        functools.partial(_s5_kernel, nc=nc, nb=nb),
        grid=(S5_GROUPS // S5_GB,),
        in_specs=[pl.BlockSpec((nb * seq, S5_GB * S5_GS), lambda g: (first_row_block, g)),
                  gb(lanes, lanes), gb(lanes, lanes), gb(4, lanes, S5_N), gb(4, S5_N, lanes),
                  gb(4, S5_N), gb(4, nb, S5_N)],
        out_specs=[pl.BlockSpec((nb * seq, S5_GB * S5_GS), lambda g: (0, g)), gb(4, nb, S5_N)],
        out_shape=[jax.ShapeDtypeStruct((nb * seq, S5_W), F32),
                   jax.ShapeDtypeStruct((S5_GROUPS, 4, nb, S5_N), F32)],
        scratch_shapes=[pltpu.VMEM((S5_GB, n_chunks, lanes), BF16), pltpu.VMEM((S5_GB, n_chunks, lanes), F32),
                        pltpu.VMEM((4, n_chunks, S5_N), F32), pltpu.VMEM((4, n_chunks, S5_N), F32)],
        compiler_params=_cparams(1), name="s5",
    )(su, *ops, step, h0)


def _glu_kernel(y_ref, su_ref, d_ref, w_ref, o_ref, wbf_ref):
    @pl.when(pl.program_id(0) == 0)
    def _():
        wbf_ref[...] = w_ref[...].astype(BF16)

    z = _gelu_tanh(y_ref[...] + d_ref[...] * su_ref[...])
    gate = jax.nn.sigmoid(jnp.dot(z.astype(BF16), wbf_ref[...], preferred_element_type=F32))
    o_ref[...] = (z * gate).astype(o_ref.dtype)


def _glu(y, su, d, w_glu, layer):
    row = pl.BlockSpec((TM, S5_W), lambda i: (i, 0))
    return pl.pallas_call(
        _glu_kernel,
        grid=(T_ALL // TM,),
        in_specs=[row, row, pl.BlockSpec((None, 1, S5_W), lambda i: (layer, 0, 0)),
                  pl.BlockSpec((None, S5_W, S5_W), lambda i: (layer, 0, 0))],
        out_specs=row,
        out_shape=jax.ShapeDtypeStruct((T_ALL, S5_W), BF16),
        scratch_shapes=[pltpu.VMEM((S5_W, S5_W), BF16)],
        compiler_params=_cparams(1), name="glu",
    )(y, su, d.reshape(DEPTH, 1, S5_W), w_glu)


def _merge_kernel(a_ref, b_ref, c_ref, ga_ref, gb_ref, gc_ref, wa_ref, wb_ref, wc_ref, o_ref, wbf_ref):
    @pl.when(pl.program_id(1) == 0)
    def _():
        wbf_ref[0] = wa_ref[...].astype(BF16)
        wbf_ref[1] = wb_ref[...].astype(BF16)
        wbf_ref[2] = wc_ref[...].astype(BF16)

    acc = ga_ref[...] * jnp.dot(a_ref[...], wbf_ref[0], preferred_element_type=F32)
    acc += gb_ref[...] * jnp.dot(b_ref[...], wbf_ref[1], preferred_element_type=F32)
    acc += gc_ref[...] * jnp.dot(c_ref[...], wbf_ref[2], preferred_element_type=F32)
    o_ref[...] = acc.astype(o_ref.dtype)


def _merge(o_na, o_gla, zz, gates, w_na, w_gla, w_s5, layer):
    tn = 1024
    ncb = D_MODEL // tn
    act = pl.BlockSpec((TM, NA_W), lambda j, i: (i, 0))
    gate = lambda k: pl.BlockSpec((TM, tn), lambda j, i: (i, k * ncb + j))
    w = pl.BlockSpec((None, NA_W, tn), lambda j, i: (layer, 0, j))
    return pl.pallas_call(
        _merge_kernel,
        grid=(ncb, T_ALL // TM),
        in_specs=[act, act, act, gate(0), gate(1), gate(2), w, w, w],
        out_specs=pl.BlockSpec((TM, tn), lambda j, i: (i, j)),
        out_shape=jax.ShapeDtypeStruct((T_ALL, D_MODEL), BF16),
        scratch_shapes=[pltpu.VMEM((3, NA_W, tn), BF16)],
        compiler_params=_cparams(2), name="merge",
    )(o_na, o_gla, zz, gates, gates, gates, w_na, w_gla, w_s5)


def _out_proj_kernel(m_ref, w_ref, x_ref, mod_ref, o_ref, wbf_ref, *, gate_row):
    @pl.when(pl.program_id(1) == 0)
    def _():
        wbf_ref[...] = w_ref[...].astype(BF16)

    y = jnp.dot(m_ref[...], wbf_ref[...], preferred_element_type=F32)
    o_ref[...] = x_ref[...] + mod_ref[gate_row:gate_row + 1, :] * y


def _out_proj(merged, w_out, x, mod, layer):
    tn = 1024
    return pl.pallas_call(
        functools.partial(_out_proj_kernel, gate_row=2),
        grid=(D_MODEL // tn, T_ALL // MM_TM),
        in_specs=[pl.BlockSpec((MM_TM, D_MODEL), lambda j, i: (i, 0)),
                  pl.BlockSpec((None, D_MODEL, tn), lambda j, i: (layer, 0, j)),
                  pl.BlockSpec((MM_TM, tn), lambda j, i: (i, j)),
                  pl.BlockSpec((None, 6, tn), lambda j, i: (_group_of_tile(i, MM_TM), 0, j))],
        out_specs=pl.BlockSpec((MM_TM, tn), lambda j, i: (i, j)),
        out_shape=jax.ShapeDtypeStruct((T_ALL, D_MODEL), F32),
        scratch_shapes=[pltpu.VMEM((D_MODEL, tn), BF16)],
        compiler_params=_cparams(2), name="out_proj",
    )(merged, w_out, x, mod)


def _first_max(v, iota, n):
    m = jnp.max(v, axis=0, keepdims=True)
    first = jnp.min(jnp.where(v == m, iota, float(n)), axis=0, keepdims=True)
    return m, first


def _router_kernel(x_ref, wt_ref, bias_ref, idx_ref, w_ref):
    per = N_EXPERTS // N_GROUPS
    x_hi, x_lo = _split_bf16(x_ref[...])
    w_hi, w_lo = _split_bf16(wt_ref[...])
    logits = (lax.dot_general(w_hi, x_hi, NT, preferred_element_type=F32)
              + lax.dot_general(w_hi, x_lo, NT, preferred_element_type=F32)
              + lax.dot_general(w_lo, x_hi, NT, preferred_element_type=F32))
    scores = jax.nn.sigmoid(logits)
    sel = scores + bias_ref[...]
    tm = sel.shape[1]
    iota_g = lax.broadcasted_iota(jnp.int32, (per, tm), 0).astype(F32)
    grp_rows = []
    for g in range(N_GROUPS):
        v = sel[g * per:(g + 1) * per, :]
        m1, first = _first_max(v, iota_g, per)
        m2 = jnp.max(jnp.where(iota_g == first, -jnp.inf, v), axis=0, keepdims=True)
        grp_rows.append(m1 + m2)
    grp = jnp.concatenate(grp_rows, axis=0)
    iota_n = lax.broadcasted_iota(jnp.int32, (N_GROUPS, tm), 0).astype(F32)
    chosen = jnp.zeros((N_GROUPS, tm), F32)
    for _ in range(TOPK_GROUPS):
        _, first = _first_max(grp, iota_n, N_GROUPS)
        hit = iota_n == first
        chosen = jnp.where(hit, 1.0, chosen)
        grp = jnp.where(hit, -jnp.inf, grp)
    mask = jnp.concatenate([jnp.broadcast_to(chosen[g:g + 1, :], (per, tm)) for g in range(N_GROUPS)], axis=0)
    sel = jnp.where(mask > 0.5, sel, -jnp.inf)
    iota_e = lax.broadcasted_iota(jnp.int32, (N_EXPERTS, tm), 0).astype(F32)
    ids, ws = [], []
    for _ in range(TOP_K):
        _, first = _first_max(sel, iota_e, N_EXPERTS)
        hit = iota_e == first
        ids.append(first)
        ws.append(jnp.sum(jnp.where(hit, scores, 0.0), axis=0, keepdims=True))
        sel = jnp.where(hit, -jnp.inf, sel)
    w = jnp.concatenate(ws, axis=0)
    idx_ref[...] = jnp.concatenate(ids, axis=0).astype(jnp.int32)
    w_ref[...] = w / jnp.sum(w, axis=0, keepdims=True) * ROUTED_SCALE


def _router(h, router_w_t, router_bias):
    tm = 256
    return pl.pallas_call(
        _router_kernel,
        grid=(T_ALL // tm,),
        in_specs=[pl.BlockSpec((tm, D_MODEL), lambda i: (i, 0)),
                  pl.BlockSpec((N_EXPERTS, D_MODEL), lambda i: (0, 0)),
                  pl.BlockSpec((N_EXPERTS, 1), lambda i: (0, 0))],
        out_specs=[pl.BlockSpec((TOP_K, tm), lambda i: (0, i)), pl.BlockSpec((TOP_K, tm), lambda i: (0, i))],
        out_shape=[jax.ShapeDtypeStruct((TOP_K, T_ALL), jnp.int32), jax.ShapeDtypeStruct((TOP_K, T_ALL), F32)],
        compiler_params=_cparams(1), name="router",
    )(h, router_w_t, router_bias.reshape(N_EXPERTS, 1))


def _dispatch(idx):
    n_assign = T_ALL * TOP_K
    flat_e = idx.reshape(-1)
    experts = jnp.arange(N_EXPERTS, dtype=jnp.int32)
    sorted_e, order = lax.sort((flat_e, jnp.arange(n_assign, dtype=jnp.int32)), num_keys=1, is_stable=True)
    counts = jnp.sum((flat_e[:, None] == experts[None, :]).astype(jnp.int32), axis=0)
    cnt_start = jnp.cumsum(counts) - counts
    padded = (counts + MOE_MB - 1) // MOE_MB * MOE_MB
    pad_end = jnp.cumsum(padded)
    pad_start = pad_end - padded
    onehot = (sorted_e[:, None] == experts[None, :]).astype(jnp.int32)
    slot_sorted = jnp.arange(n_assign, dtype=jnp.int32) + jnp.sum(onehot * (pad_start - cnt_start)[None, :], axis=1)
    _, slot_of = lax.sort((order, slot_sorted), num_keys=1)
    blk_first = jnp.arange(MOE_NB, dtype=jnp.int32) * MOE_MB
    block_e = jnp.minimum(jnp.sum((pad_end[None, :] <= blk_first[:, None]).astype(jnp.int32), axis=1),
                          N_EXPERTS - 1)
    blk_onehot = (block_e[:, None] == experts[None, :]).astype(jnp.int32)
    pick = lambda v: jnp.sum(blk_onehot * v[None, :], axis=1)
    src = jnp.clip(pick(cnt_start) + blk_first - pick(pad_start), 0, n_assign)
    sorted_tok = jnp.pad(order // TOP_K, (0, MOE_MB))
    n_used = (pad_end[-1] // MOE_MB).astype(jnp.int32).reshape(1)
    return sorted_tok, slot_of.reshape(T_ALL, TOP_K), block_e.astype(jnp.int32), src.astype(jnp.int32), n_used


MOE_PACKED_W = D_MODEL // 2


def _expert_kernel(be_ref, nu_ref, src_ref, stok_ref, x_hbm, wg_ref, wu_ref, wd_ref, o_ref,
                   xbuf, x_bf, wg_bf, wu_bf, wd_bf, sems):
    i = pl.program_id(0)
    n_used = nu_ref[0]
    last = MOE_NB - 1
    half = MOE_PACKED_W

    def row_copy(blk, slot, j):
        tok = stok_ref[src_ref[blk] + j]
        return pltpu.make_async_copy(x_hbm.at[pl.ds(tok, 1), :], xbuf.at[slot, pl.ds(j, 1), :], sems.at[slot])

    def wait_block(slot):
        pltpu.make_async_copy(x_hbm.at[pl.ds(0, MOE_MB), :], xbuf.at[slot], sems.at[slot]).wait()

    @pl.when(i == 0)
    def _():
        def body(j, carry):
            row_copy(0, 0, j).start()
            return carry
        lax.fori_loop(0, MOE_MB, body, 0, unroll=8)

    def compute(prefetch):
        @pl.when((i == 0) | (be_ref[i] != be_ref[jnp.maximum(i - 1, 0)]))
        def _():
            wg_bf[...] = wg_ref[...].astype(BF16)
            wu_bf[...] = wu_ref[...].astype(BF16)
            wd_bf[...] = wd_ref[...].astype(BF16)

        slot = i % 2
        wait_block(slot)
        x_lo, x_hi = _unpack_bf16_pair(xbuf[slot])
        x_bf[0] = x_lo.astype(BF16)
        x_bf[1] = x_hi.astype(BF16)
        if prefetch:
            for j in range(MOE_MB):
                row_copy(i + 1, 1 - slot, j).start()
        up = lambda w: (jnp.dot(x_bf[0], w[:half, :], preferred_element_type=F32)
                        + jnp.dot(x_bf[1], w[half:, :], preferred_element_type=F32))
        a = (_silu(up(wg_bf)) * up(wu_bf)).astype(BF16)
        y = jnp.dot(a, wd_bf[...], preferred_element_type=F32)
        o_ref[...] = _pack_bf16_pair(y[:, :half], y[:, half:])

    @pl.when((i < n_used) & (i < last))
    def _():
        compute(True)

    @pl.when((i < n_used) & (i == last))
    def _():
        compute(False)

    @pl.when(i >= n_used)
    def _():
        o_ref[...] = jnp.zeros_like(o_ref)

        @pl.when(i == n_used)
        def _():
            wait_block(i % 2)


def _experts(x_packed, sorted_tok, block_e, src, n_used, wg, wu, wd, layer):
    w_idx = lambda i, be, nu, src, stok: (layer, be[i], 0, 0)
    w_up = pl.BlockSpec((None, None, D_MODEL, F_EXPERT), w_idx)
    return pl.pallas_call(
        _expert_kernel,
        grid_spec=pltpu.PrefetchScalarGridSpec(
            num_scalar_prefetch=4,
            grid=(MOE_NB,),
            in_specs=[pl.BlockSpec(memory_space=pl.ANY), w_up, w_up,
                      pl.BlockSpec((None, None, F_EXPERT, D_MODEL), w_idx)],
            out_specs=pl.BlockSpec((MOE_MB, MOE_PACKED_W), lambda i, be, nu, src, stok: (i, 0)),
            scratch_shapes=[pltpu.VMEM((2, MOE_MB, MOE_PACKED_W), jnp.uint32),
                            pltpu.VMEM((2, MOE_MB, MOE_PACKED_W), BF16),
                            pltpu.VMEM((D_MODEL, F_EXPERT), BF16), pltpu.VMEM((D_MODEL, F_EXPERT), BF16),
                            pltpu.VMEM((F_EXPERT, D_MODEL), BF16), pltpu.SemaphoreType.DMA((2,))]),
        out_shape=jax.ShapeDtypeStruct((MOE_SLOTS, MOE_PACKED_W), jnp.uint32),
        compiler_params=_cparams(1), name="moe_experts",
    )(block_e, n_used, src, sorted_tok, x_packed, wg, wu, wd)


def _shared_up_kernel(x_ref, wg_ref, wu_ref, o_ref, wg_bf, wu_bf):
    @pl.when(pl.program_id(0) == 0)
    def _():
        wg_bf[...] = wg_ref[...].astype(BF16)
        wu_bf[...] = wu_ref[...].astype(BF16)

    x = x_ref[...].astype(BF16)
    g = jnp.dot(x, wg_bf[...], preferred_element_type=F32)
    u = jnp.dot(x, wu_bf[...], preferred_element_type=F32)
    o_ref[...] = (_silu(g) * u).astype(o_ref.dtype)


def _shared_up(h, wg, wu, layer):
    w = pl.BlockSpec((None, D_MODEL, F_SHARED), lambda i: (layer, 0, 0))
    return pl.pallas_call(
        _shared_up_kernel,
        grid=(T_ALL // TM,),
        in_specs=[pl.BlockSpec((TM, D_MODEL), lambda i: (i, 0)), w, w],
        out_specs=pl.BlockSpec((TM, F_SHARED), lambda i: (i, 0)),
        out_shape=jax.ShapeDtypeStruct((T_ALL, F_SHARED), BF16),
        scratch_shapes=[pltpu.VMEM((D_MODEL, F_SHARED), BF16), pltpu.VMEM((D_MODEL, F_SHARED), BF16)],
        compiler_params=_cparams(1), name="shared_up",
    )(h, wg, wu)


def _combine_kernel(slot_ref, nxt_ref, y_hbm, rw_ref, act_ref, wd_ref, x_ref, mod_ref, o_ref, buf, wd_bf, sems, *,
                    gate_row):
    i = pl.program_id(0)
    n_rows = TOP_K * CMB_TM
    half = MOE_PACKED_W

    def row_copy(idx_ref, slot, j):
        return pltpu.make_async_copy(y_hbm.at[pl.ds(idx_ref[0, 0, j], 1), :], buf.at[slot, pl.ds(j, 1), :],
                                     sems.at[slot])

    @pl.when(i == 0)
    def _():
        wd_bf[...] = wd_ref[...].astype(BF16)

        def body(j, carry):
            row_copy(slot_ref, 0, j).start()
            return carry
        lax.fori_loop(0, n_rows, body, 0, unroll=8)

    def tile(prefetch):
        slot = i % 2
        pltpu.make_async_copy(y_hbm.at[pl.ds(0, n_rows), :], buf.at[slot], sems.at[slot]).wait()
        if prefetch:
            for j in range(n_rows):
                row_copy(nxt_ref, 1 - slot, j).start()
        shared = jnp.dot(act_ref[...], wd_bf[...], preferred_element_type=F32)
        lo = shared[:, :half]
        hi = shared[:, half:]
        for k in range(TOP_K):
            row_lo, row_hi = _unpack_bf16_pair(buf[slot, k * CMB_TM:(k + 1) * CMB_TM, :])
            lo = lo + rw_ref[:, k:k + 1] * row_lo
            hi = hi + rw_ref[:, k:k + 1] * row_hi
        gate = mod_ref[gate_row:gate_row + 1, :]
        o_ref[:, :half] = x_ref[:, :half] + gate[:, :half] * lo
        o_ref[:, half:] = x_ref[:, half:] + gate[:, half:] * hi

    @pl.when(i + 1 < pl.num_programs(0))
    def _():
        tile(True)

    @pl.when(i + 1 == pl.num_programs(0))
    def _():
        tile(False)


def _combine(y_sorted, slot_of, route_w, act, sh_wd, x, mod, layer):
    nt = T_ALL // CMB_TM
    slots = slot_of.reshape(nt, CMB_TM, TOP_K).transpose(0, 2, 1).reshape(nt, 1, TOP_K * CMB_TM)
    slot_spec = lambda nxt: pl.BlockSpec((1, 1, TOP_K * CMB_TM), lambda i: (jnp.minimum(i + nxt, nt - 1), 0, 0),
                                         memory_space=pltpu.SMEM)
    return pl.pallas_call(
        functools.partial(_combine_kernel, gate_row=5),
        grid=(nt,),
        in_specs=[slot_spec(0), slot_spec(1),
                  pl.BlockSpec(memory_space=pl.ANY),
                  pl.BlockSpec((CMB_TM, TOP_K), lambda i: (i, 0)),
                  pl.BlockSpec((CMB_TM, F_SHARED), lambda i: (i, 0)),
                  pl.BlockSpec((None, F_SHARED, D_MODEL), lambda i: (layer, 0, 0)),
                  pl.BlockSpec((CMB_TM, D_MODEL), lambda i: (i, 0)),
                  pl.BlockSpec((None, 6, D_MODEL), lambda i: (_group_of_tile(i, CMB_TM), 0, 0))],
        out_specs=pl.BlockSpec((CMB_TM, D_MODEL), lambda i: (i, 0)),
        out_shape=jax.ShapeDtypeStruct((T_ALL, D_MODEL), F32),
        scratch_shapes=[pltpu.VMEM((2, TOP_K * CMB_TM, MOE_PACKED_W), jnp.uint32),
                        pltpu.VMEM((F_SHARED, D_MODEL), BF16), pltpu.SemaphoreType.DMA((2,))],
        compiler_params=_cparams(1), name="moe_combine",
    )(slots, slots, y_sorted, route_w, act, sh_wd, x, mod)


def kernel(x_prompt, x_sample, cache_na_k, cache_na_v, state_gla, state_s5, c, c_ctx, ada_w, ada_b, norm1_g, norm2_g, w_in, na_rpb, gla_wa, gla_ba, gla_norm_g, s5_a_re, s5_a_im, s5_log_dt, s5_b_re, s5_b_im, s5_c_re, s5_c_im, s5_d, s5_w_glu, w_br_na, w_br_gla, w_br_s5, w_merge, w_out, router_w, router_bias, exp_wg, exp_wu, exp_wd, sh_wg, sh_wu, sh_wd, final_norm_g):
    x = jnp.concatenate([x_prompt.reshape(T_CTX, D_MODEL), x_sample.reshape(T_LAT, D_MODEL)], axis=0)
    cvec = jnp.concatenate([c_ctx[None, :], c, jnp.zeros((MOD_ROWS - N_MOD, D_MODEL), F32)], axis=0)
    mods = _ada(cvec, ada_w, ada_b).reshape(DEPTH, MOD_ROWS, 6, D_MODEL)
    cos, sin = _rope_tables()
    w_in_t = jnp.swapaxes(w_in, 1, 2)
    zero_gla = jnp.zeros((BATCH, 2, GLA_HEADS, GLA_DV, GLA_DK), F32)
    zero_s5 = jnp.zeros((S5_GROUPS, 4, BATCH, S5_N), F32)
    new_k, new_v, new_gla, new_s5 = [], [], [], []
    for l in range(DEPTH):
        mod = mods[l]
        h = _norm(x, norm1_g[l], mod, (0, 1), BF16)
        u = _mm(h, w_in_t, l, 1024, n=IN_MAIN, w_rows_are_outputs=True)
        lr = _mm(h, w_in_t[l, IN_MAIN:IN_MAIN + 2 * GLA_LR], None, 2 * GLA_LR, w_rows_are_outputs=True)
        su = _mm(h, w_in_t[l, IN_MAIN + 2 * GLA_LR:], None, S5_W, w_rows_are_outputs=True)
        gates = _mm(h, w_merge, l, 1024, act="sigmoid")

        o_na = jnp.concatenate([_ctx_attn(u),
                                _na_attn(u, cache_na_k, cache_na_v, _na_bias(na_rpb[l]), l)], axis=0)

        s0_lat = state_gla[:, l].transpose(0, 1, 2, 4, 3)
        og_ctx, sfin = _gla(u, lr, gla_wa, gla_ba, gla_norm_g, zero_gla, cos, sin, l, latent=False)
        og_lat, _ = _gla(u, lr, gla_wa, gla_ba, gla_norm_g, s0_lat, cos, sin, l, latent=True)
        o_gla = jnp.concatenate([og_ctx, og_lat], axis=0)

        ops, step = _s5_operators(s5_a_re[l], s5_a_im[l], s5_log_dt[l], s5_b_re[l], s5_b_im[l],
                                  s5_c_re[l], s5_c_im[l])
        st = state_s5[:, l].astype(F32)
        h0_lat = st.transpose(2, 1, 4, 0, 3).reshape(S5_GROUPS, 4, DEC_BATCH, S5_N)
        y_ctx, hfin = _s5(su, ops, step, zero_s5, nb=BATCH, seq=SEQ, first_row_block=0)
        y_lat, _ = _s5(su, ops, step, h0_lat, nb=DEC_BATCH, seq=DEC_SEQ, first_row_block=T_CTX // T_LAT)
        zz = _glu(jnp.concatenate([y_ctx, y_lat], axis=0), su, s5_d, s5_w_glu, l)

        merged = _merge(o_na, o_gla, zz, gates, w_br_na, w_br_gla, w_br_s5, l)
        x = _out_proj(merged, w_out, x, mod, l)

        h2, h2_packed = _norm(x, norm2_g[l], mod, (3, 4), F32, packed=True)
        idx_t, w_t = _router(h2, router_w[l].T, router_bias[l])
        sorted_tok, slot_of, block_e, src, n_used = _dispatch(idx_t.T)
        y_sorted = _experts(h2_packed, sorted_tok, block_e, src, n_used, exp_wg, exp_wu, exp_wd, l)
        act = _shared_up(h2, sh_wg, sh_wu, l)
        x = _combine(y_sorted, slot_of, w_t.T, act, sh_wd, x, mod, l)

        new_k.append(u[:T_CTX, NA_W:2 * NA_W].reshape(BATCH, SEQ, NA_HEADS, NA_DH))
        new_v.append(u[:T_CTX, 2 * NA_W:3 * NA_W].reshape(BATCH, SEQ, NA_HEADS, NA_DH))
        new_gla.append(sfin.transpose(0, 1, 2, 4, 3))
        new_s5.append(hfin.reshape(S5_GROUPS, 2, 2, BATCH, S5_N).transpose(3, 1, 0, 4, 2))

    y = _norm(x, final_norm_g, None, None, F32)
    return (y[:T_CTX].reshape(BATCH, SEQ, D_MODEL), y[T_CTX:].reshape(DEC_BATCH, DEC_SEQ, D_MODEL),
            jnp.stack(new_k, axis=1), jnp.stack(new_v, axis=1),
            jnp.stack(new_gla, axis=1), jnp.stack(new_s5, axis=1))
```

```python
import functools
import math

import jax
import jax.numpy as jnp
from jax import lax
from jax.experimental import pallas as pl
from jax.experimental.pallas import tpu as pltpu

F32 = jnp.float32
BF16 = jnp.bfloat16

D_MODEL = 2048
BATCH = 16
SEQ = 256
DEPTH = 2
DEC_BATCH = 2
DEC_SEQ = 1024
PAST_LEN = 512
GRID_W = 64
NA_HEADS = 8
NA_DH = 128
NA_W = NA_HEADS * NA_DH
NA_ROWS = 8
NA_COLS = 16
GLA_HEADS = 4
GLA_DK = 128
GLA_DV = 256
GLA_QK_W = GLA_HEADS * GLA_DK
GLA_V_W = GLA_HEADS * GLA_DV
GLA_LR = 16
GLA_TAU = 16.0
GLA_CHUNK = 64
GLA_SUB = 16
GLA_UNROLL = 4
ROPE_BASE = 10000.0
S5_W = 1024
S5_GS = 16
S5_GROUPS = S5_W // S5_GS
S5_N = 64
S5_T = 16
IN_W = 3 * NA_W + 2 * GLA_QK_W + 2 * GLA_V_W + 2 * GLA_LR + S5_W
IN_MAIN = 3 * NA_W + 2 * GLA_QK_W + 2 * GLA_V_W
N_EXPERTS = 64
TOP_K = 8
N_GROUPS = 8
TOPK_GROUPS = 4
F_EXPERT = 512
F_SHARED = 512
ROUTED_SCALE = 2.5
EPS = 1e-6

T_CTX = BATCH * SEQ
T_LAT = DEC_BATCH * DEC_SEQ
T_ALL = T_CTX + T_LAT
N_MOD = 1 + DEC_BATCH
MOD_ROWS = 8

TM = 512
MM_TM = 1024
MOE_MB = 256
MOE_NB = T_ALL * TOP_K // MOE_MB + N_EXPERTS
MOE_SLOTS = MOE_NB * MOE_MB
CMB_TM = 128
NEG = -1e30

VMEM_LIMIT = 56 * 1024 * 1024

NT = (((1,), (1,)), ((), ()))
TN = (((0,), (0,)), ((), ()))


def _cparams(n_axes):
    return pltpu.CompilerParams(dimension_semantics=("arbitrary",) * n_axes,
                                vmem_limit_bytes=VMEM_LIMIT)


def _group_of_tile(i, tm):
    row = i * tm
    return jnp.where(row < T_CTX, 0, 1 + (row - T_CTX) // DEC_SEQ)


def _silu(x):
    return x * jax.nn.sigmoid(x)


def _gelu_tanh(x):
    return 0.5 * x * (1.0 + jnp.tanh(math.sqrt(2.0 / math.pi) * (x + 0.044715 * (x * x * x))))


def _ada_kernel(c_ref, w_ref, b_ref, o_ref):
    s = _silu(c_ref[...]).astype(BF16)
    o_ref[...] = jnp.dot(s, w_ref[...].astype(BF16), preferred_element_type=F32) + b_ref[...]


def _ada(cvec, ada_w, ada_b):
    tn = 1024
    return pl.pallas_call(
        _ada_kernel,
        grid=(DEPTH, 6 * D_MODEL // tn),
        in_specs=[pl.BlockSpec((MOD_ROWS, D_MODEL), lambda l, j: (0, 0)),
                  pl.BlockSpec((None, D_MODEL, tn), lambda l, j: (l, 0, j)),
                  pl.BlockSpec((None, 1, tn), lambda l, j: (l, 0, j))],
        out_specs=pl.BlockSpec((None, MOD_ROWS, tn), lambda l, j: (l, 0, j)),
        out_shape=jax.ShapeDtypeStruct((DEPTH, MOD_ROWS, 6 * D_MODEL), F32),
        compiler_params=_cparams(2), name="ada",
    )(cvec, ada_w, ada_b.reshape(DEPTH, 1, 6 * D_MODEL))


def _pack_bf16_pair(lo, hi):
    bits = lambda v: pltpu.bitcast(v.astype(BF16).astype(F32), jnp.uint32)
    return (bits(lo) >> 16) | (bits(hi) & jnp.uint32(0xFFFF0000))


def _unpack_bf16_pair(w):
    return pltpu.bitcast(w << 16, F32), pltpu.bitcast(w & jnp.uint32(0xFFFF0000), F32)


def _norm_kernel(x_ref, g_ref, *rest, rows, packed):
    x = x_ref[...]
    y = x * lax.rsqrt(jnp.mean(x * x, axis=-1, keepdims=True) + EPS) * g_ref[...]
    if rows is not None:
        mod_ref = rest[0]
        y = y * (1.0 + mod_ref[rows[1]:rows[1] + 1, :]) + mod_ref[rows[0]:rows[0] + 1, :]
    if packed:
        rest[-2][...] = y
        rest[-1][...] = _pack_bf16_pair(y[:, :D_MODEL // 2], y[:, D_MODEL // 2:])
    else:
        rest[-1][...] = y.astype(rest[-1].dtype)


def _norm(x, g, mod, rows, out_dtype, packed=False):
    tm = 256
    in_specs = [pl.BlockSpec((tm, D_MODEL), lambda i: (i, 0)),
                pl.BlockSpec((1, D_MODEL), lambda i: (0, 0))]
    args = [x, g.reshape(1, D_MODEL)]
    if rows is not None:
        in_specs.append(pl.BlockSpec((None, 6, D_MODEL), lambda i: (_group_of_tile(i, tm), 0, 0)))
        args.append(mod)
    out_specs = pl.BlockSpec((tm, D_MODEL), lambda i: (i, 0))
    out_shape = jax.ShapeDtypeStruct((T_ALL, D_MODEL), out_dtype)
    if packed:
        out_specs = [out_specs, pl.BlockSpec((tm, D_MODEL // 2), lambda i: (i, 0))]
        out_shape = [out_shape, jax.ShapeDtypeStruct((T_ALL, D_MODEL // 2), jnp.uint32)]
    return pl.pallas_call(
        functools.partial(_norm_kernel, rows=rows, packed=packed),
        grid=(T_ALL // tm,),
        in_specs=in_specs,
        out_specs=out_specs,
        out_shape=out_shape,
        compiler_params=_cparams(1), name="norm",
    )(*args)


def _mm_kernel(x_ref, w_ref, o_ref, wbf_ref, *, act, w_rows_are_outputs):
    @pl.when(pl.program_id(1) == 0)
    def _():
        wbf_ref[...] = w_ref[...].astype(BF16)

    x = x_ref[...].astype(BF16)
    if w_rows_are_outputs:
        acc = lax.dot_general(x, wbf_ref[...], NT, preferred_element_type=F32)
    else:
        acc = jnp.dot(x, wbf_ref[...], preferred_element_type=F32)
    if act == "sigmoid":
        acc = jax.nn.sigmoid(acc)
    o_ref[...] = acc.astype(o_ref.dtype)


def _mm(x, w, layer, tn, act=None, out_dtype=F32, n=None, w_rows_are_outputs=False):
    k = x.shape[1]
    if n is None:
        n = w.shape[-2] if w_rows_are_outputs else w.shape[-1]
    blk, idx = ((tn, k), lambda j: (j, 0)) if w_rows_are_outputs else ((k, tn), lambda j: (0, j))
    if layer is None:
        w_spec = pl.BlockSpec(blk, lambda j, i: idx(j))
    else:
        w_spec = pl.BlockSpec((None,) + blk, lambda j, i: (layer,) + idx(j))
    return pl.pallas_call(
        functools.partial(_mm_kernel, act=act, w_rows_are_outputs=w_rows_are_outputs),
        grid=(n // tn, T_ALL // MM_TM),
        in_specs=[pl.BlockSpec((MM_TM, k), lambda j, i: (i, 0)), w_spec],
        out_specs=pl.BlockSpec((MM_TM, tn), lambda j, i: (i, j)),
        out_shape=jax.ShapeDtypeStruct((T_ALL, n), out_dtype),
        scratch_shapes=[pltpu.VMEM(blk, BF16)],
        compiler_params=_cparams(2), name="mm",
    )(x, w)


def _ctx_attn_kernel(q_ref, k_ref, v_ref, o_ref):
    scale = NA_DH ** -0.5
    for h in range(NA_HEADS):
        sl = slice(h * NA_DH, (h + 1) * NA_DH)
        q = q_ref[:, sl].astype(BF16)
        k = k_ref[:, sl].astype(BF16)
        v = v_ref[:, sl].astype(BF16)
        s = lax.dot_general(q, k, NT, preferred_element_type=F32) * scale
        p = jnp.exp(s - jnp.max(s, axis=-1, keepdims=True))
        o = jnp.dot(p.astype(BF16), v, preferred_element_type=F32) / jnp.sum(p, axis=-1, keepdims=True)
        o_ref[:, sl] = o.astype(o_ref.dtype)


def _ctx_attn(u):
    spec = lambda cb: pl.BlockSpec((SEQ, NA_W), lambda b: (b, cb))
    return pl.pallas_call(
        _ctx_attn_kernel,
        grid=(BATCH,),
        in_specs=[spec(0), spec(1), spec(2)],
        out_specs=pl.BlockSpec((SEQ, NA_W), lambda b: (b, 0)),
        out_shape=jax.ShapeDtypeStruct((T_CTX, NA_W), BF16),
        compiler_params=_cparams(1), name="ctx_attn",
    )(u, u, u)


NA_GRID_ROWS = DEC_SEQ // GRID_W
NA_KR = min(NA_ROWS, NA_GRID_ROWS)
NA_LOC = NA_KR * GRID_W


def _na_bias(rpb):
    n_off = 2 * NA_ROWS - 1
    pad = GRID_W - NA_COLS
    period = 2 * GRID_W
    p = jnp.pad(rpb, ((0, 0), (0, 0), (pad, period - pad - (2 * NA_COLS - 1))))
    hank = jnp.tile(p, (1, 1, GRID_W + 1))[..., :GRID_W * (period + 1)]
    hank = hank.reshape(NA_HEADS, n_off, GRID_W, period + 1)[..., :GRID_W]
    band = hank[:, :, ::-1, :]
    qc = jnp.arange(GRID_W)[:, None]
    kc = jnp.arange(GRID_W)[None, :]
    win = jnp.clip(qc - NA_COLS // 2, 0, GRID_W - NA_COLS)
    valid = (kc >= win) & (kc < win + NA_COLS)
    band = jnp.where(valid[None, None], band, NEG)
    return band.transpose(0, 2, 1, 3).reshape(NA_HEADS, GRID_W, n_off * GRID_W)


def _na_kernel(q_ref, k_ref, v_ref, ck_ref, cv_ref, bias_ref, o_ref):
    scale = NA_DH ** -0.5
    kb = k_ref[...].astype(BF16)
    vb = v_ref[...].astype(BF16)
    ck = ck_ref[...].astype(BF16)
    cv = cv_ref[...].astype(BF16)
    for r in range(NA_GRID_ROWS):
        first = min(max(r - NA_KR // 2, 0), NA_GRID_ROWS - NA_KR)
        off = first - r + NA_ROWS - 1
        rows = slice(r * GRID_W, (r + 1) * GRID_W)
        keys = slice(first * GRID_W, first * GRID_W + NA_LOC)
        q = q_ref[rows, :].astype(BF16)
        s1 = (lax.dot_general(q, kb[keys, :], NT, preferred_element_type=F32) * scale
              + bias_ref[:, off * GRID_W:off * GRID_W + NA_LOC])
        s2 = lax.dot_general(q, ck, NT, preferred_element_type=F32) * scale
        m = jnp.maximum(jnp.max(s1, axis=-1, keepdims=True), jnp.max(s2, axis=-1, keepdims=True))
        p1 = jnp.exp(s1 - m)
        p2 = jnp.exp(s2 - m)
        den = jnp.sum(p1, axis=-1, keepdims=True) + jnp.sum(p2, axis=-1, keepdims=True)
        o = (jnp.dot(p1.astype(BF16), vb[keys, :], preferred_element_type=F32)
             + jnp.dot(p2.astype(BF16), cv, preferred_element_type=F32))
        o_ref[rows, :] = (o / den).astype(o_ref.dtype)


def _na_attn(u, cache_k, cache_v, bias, layer):
    lat_sb = T_CTX // DEC_SEQ
    ck = cache_k.reshape(DEC_BATCH, DEPTH, PAST_LEN, NA_W)
    cv = cache_v.reshape(DEC_BATCH, DEPTH, PAST_LEN, NA_W)
    qkv_spec = lambda cb: pl.BlockSpec((DEC_SEQ, NA_DH), lambda b, h: (lat_sb + b, cb * NA_HEADS + h))
    c_spec = pl.BlockSpec((None, None, PAST_LEN, NA_DH), lambda b, h: (b, layer, 0, h))
    return pl.pallas_call(
        _na_kernel,
        grid=(DEC_BATCH, NA_HEADS),
        in_specs=[qkv_spec(0), qkv_spec(1), qkv_spec(2), c_spec, c_spec,
                  pl.BlockSpec((None, GRID_W, (2 * NA_ROWS - 1) * GRID_W), lambda b, h: (h, 0, 0))],
        out_specs=pl.BlockSpec((DEC_SEQ, NA_DH), lambda b, h: (b, h)),
        out_shape=jax.ShapeDtypeStruct((T_LAT, NA_W), BF16),
        compiler_params=_cparams(2), name="na_attn",
    )(u, u, u, ck, cv, bias)


def _rope_tables():
    half = GLA_DK // 2
    nf = half // 2
    t = jnp.arange(DEC_SEQ)
    freqs = ROPE_BASE ** (-jnp.arange(nf, dtype=F32) / nf)
    ang_r = (t // GRID_W).astype(F32)[:, None] * freqs
    ang_c = (t % GRID_W).astype(F32)[:, None] * freqs
    cos = jnp.concatenate([jnp.cos(ang_r), jnp.cos(ang_r), jnp.cos(ang_c), jnp.cos(ang_c)], axis=-1)
    sin = jnp.concatenate([-jnp.sin(ang_r), jnp.sin(ang_r), -jnp.sin(ang_c), jnp.sin(ang_c)], axis=-1)
    return cos, sin


def _split_bf16(x):
    hi = x.astype(BF16)
    return hi, (x - hi.astype(F32)).astype(BF16)


def _gla_kernel(q_ref, k_ref, v_ref, gg_ref, lr_ref, wa_ref, ba_ref, ng_ref, cos_ref, sin_ref, s0_ref,
                o_ref, sfin_ref, qs, ks, las, o_acc, st, *, seq, rope):
    nc = seq // GLA_CHUNK
    nq = GLA_DK // 4

    def rot(x):
        lane = lax.broadcasted_iota(jnp.int32, x.shape, 1)
        partner = jnp.where((lane % (2 * nq)) < nq,
                            pltpu.roll(x, GLA_DK - nq, axis=1), pltpu.roll(x, nq, axis=1))
        return x * cos_ref[...] + partner * sin_ref[...]

    q = q_ref[...] * (GLA_DK ** -0.5)
    k = k_ref[...]
    if rope:
        q = rot(q)
        k = rot(k)
    qs[...] = q
    ks[...] = k
    lr = lr_ref[...]
    for d in range(2):
        z = jnp.dot(lr[:, d * GLA_LR:(d + 1) * GLA_LR].astype(BF16), wa_ref[d].astype(BF16),
                    preferred_element_type=F32) + ba_ref[d:d + 1, :]
        las[d] = -(jnp.maximum(-z, 0.0) + jnp.log1p(jnp.exp(-jnp.abs(z)))) / GLA_TAU
        st[d] = s0_ref[d]

    row = lax.broadcasted_iota(jnp.int32, (GLA_CHUNK, GLA_CHUNK), 0)
    col = lax.broadcasted_iota(jnp.int32, (GLA_CHUNK, GLA_CHUNK), 1)
    key_row = lax.broadcasted_iota(jnp.int32, (GLA_CHUNK, 1), 0)

    def chunk(c, d):
        rev = d == 1
        rows = pl.ds(pl.multiple_of(c * GLA_CHUNK, GLA_CHUNK), GLA_CHUNK)
        qc = qs[rows, :]
        kc = ks[rows, :]
        vc = v_ref[rows, :].astype(BF16)
        la = las[d, rows, :]
        causal = (col >= row) if rev else (col <= row)
        tri = jnp.where(causal, 1.0, 0.0).astype(BF16)
        la_hi, la_lo = _split_bf16(la)
        b = (jnp.dot(tri, la_hi, preferred_element_type=F32)
             + jnp.dot(tri, la_lo, preferred_element_type=F32))
        bex = b - la
        b_last = b[0:1, :] if rev else b[GLA_CHUNK - 1:GLA_CHUNK, :]
        blocks = []
        for i in range(GLA_CHUNK // GLA_SUB):
            lo, hi = i * GLA_SUB, (i + 1) * GLA_SUB
            ref = bex[hi - 1:hi, :] if rev else bex[lo:lo + 1, :]
            qt = (qc[lo:hi, :] * jnp.exp(b[lo:hi, :] - ref)).astype(BF16)
            allowed = (key_row >= lo) if rev else (key_row < hi)
            kt = (kc * jnp.exp(jnp.where(allowed, ref - b, -jnp.inf))).astype(BF16)
            blocks.append(lax.dot_general(qt, kt, NT, preferred_element_type=F32))
        att = jnp.where(causal, jnp.concatenate(blocks, axis=0), 0.0)
        s_t = st[d]
        o = (jnp.dot(att.astype(BF16), vc, preferred_element_type=F32)
             + lax.dot_general((qc * jnp.exp(b)).astype(BF16), s_t.astype(BF16), NT,
                               preferred_element_type=F32))
        khat = (kc * jnp.exp(b_last - b)).astype(BF16)
        st[d] = s_t * jnp.exp(b_last) + lax.dot_general(vc, khat, TN, preferred_element_type=F32)
        o_acc[d, rows, :] = o

    def both(c, carry):
        chunk(c, 0)
        chunk(nc - 1 - c, 1)
        return carry

    lax.fori_loop(0, nc, both, 0, unroll=GLA_UNROLL)
    sfin_ref[...] = st[...]
    o = o_acc[0] + o_acc[1]
    o = o * lax.rsqrt(jnp.mean(o * o, axis=-1, keepdims=True) + EPS) * ng_ref[...]
    o_ref[...] = (o * _silu(gg_ref[...])).astype(o_ref.dtype)


def _gla(u, lr, wa, ba, norm_g, s0_t, cos, sin, layer, *, latent):
    seq, nb, first = (DEC_SEQ, DEC_BATCH, T_CTX // DEC_SEQ) if latent else (SEQ, BATCH, 0)
    qk_cb = 3 * NA_W // GLA_DK
    v_cb = (3 * NA_W + 2 * GLA_QK_W) // GLA_DV
    row = lambda w, cb: pl.BlockSpec((seq, w), lambda b, h: (first + b, cb + h))
    return pl.pallas_call(
        functools.partial(_gla_kernel, seq=seq, rope=latent),
        grid=(nb, GLA_HEADS),
        in_specs=[row(GLA_DK, qk_cb), row(GLA_DK, qk_cb + GLA_HEADS), row(GLA_DV, v_cb),
                  row(GLA_DV, v_cb + GLA_HEADS),
                  pl.BlockSpec((seq, 2 * GLA_LR), lambda b, h: (first + b, 0)),
                  pl.BlockSpec((None, 2, GLA_LR, GLA_DK), lambda b, h: (layer, 0, 0, h)),
                  pl.BlockSpec((None, 2, GLA_DK), lambda b, h: (layer, 0, h)),
                  pl.BlockSpec((None, 1, GLA_DV), lambda b, h: (layer, 0, h)),
                  pl.BlockSpec((seq, GLA_DK), lambda b, h: (0, 0)),
                  pl.BlockSpec((seq, GLA_DK), lambda b, h: (0, 0)),
                  pl.BlockSpec((None, 2, None, GLA_DV, GLA_DK), lambda b, h: (b, 0, h, 0, 0))],
        out_specs=[pl.BlockSpec((seq, GLA_DV), lambda b, h: (b, h)),
                   pl.BlockSpec((None, 2, None, GLA_DV, GLA_DK), lambda b, h: (b, 0, h, 0, 0))],
        out_shape=[jax.ShapeDtypeStruct((nb * seq, GLA_V_W), BF16),
                   jax.ShapeDtypeStruct((nb, 2, GLA_HEADS, GLA_DV, GLA_DK), F32)],
        scratch_shapes=[pltpu.VMEM((seq, GLA_DK), F32), pltpu.VMEM((seq, GLA_DK), F32),
                        pltpu.VMEM((2, seq, GLA_DK), F32), pltpu.VMEM((2, seq, GLA_DV), F32),
                        pltpu.VMEM((2, GLA_DV, GLA_DK), F32)],
        compiler_params=_cparams(2), name="gla",
    )(u, u, u, u, lr, wa, ba, norm_g.reshape(DEPTH, 1, GLA_V_W), cos[:seq], sin[:seq], s0_t)


def _s5_operators(a_re, a_im, log_dt, b_re, b_im, c_re, c_im):
    hp = lax.Precision.HIGHEST
    lam = lax.complex(a_re.astype(F32), a_im.astype(F32))
    lam_dt = lam * jnp.exp(log_dt.astype(F32))[..., None]
    a_bar = jnp.exp(lam_dt)
    b_bar = ((a_bar - 1.0) / lam)[..., None] * lax.complex(b_re.astype(F32), b_im.astype(F32))
    cc = lax.complex(c_re.astype(F32), c_im.astype(F32))
    taus = jnp.arange(S5_T + 1, dtype=F32)
    pw = jnp.exp(lam_dt[:, :, None, :] * taus[None, None, :, None])
    lanes = S5_T * S5_GS

    def out_weights(pwd, cd):
        m = pwd.transpose(0, 2, 1)[:, :, :, None] * cd.transpose(0, 2, 1)[:, :, None, :]
        return m.reshape(S5_GROUPS, S5_N, lanes)

    def lag_response(d):
        r = out_weights(pw[d, :, :S5_T], cc[d])
        bt = b_bar[d].transpose(0, 2, 1)
        return (jnp.einsum("gin,gnx->gix", jnp.real(bt), jnp.real(r), precision=hp)
                - jnp.einsum("gin,gnx->gix", jnp.imag(bt), jnp.imag(r), precision=hp))

    def toeplitz(lags):
        rows = [lags[:, :, (S5_T - 1 - s) * S5_GS:(2 * S5_T - 1 - s) * S5_GS] for s in range(S5_T)]
        return jnp.stack(rows, axis=1).reshape(S5_GROUPS, lanes, lanes)

    no_lag = jnp.zeros((S5_GROUPS, S5_GS, (S5_T - 1) * S5_GS), F32)
    rev_lags = lag_response(1).reshape(S5_GROUPS, S5_GS, S5_T, S5_GS)[:, :, ::-1].reshape(S5_GROUPS, S5_GS, lanes)
    toep_f = toeplitz(jnp.concatenate([no_lag, lag_response(0)], axis=-1))
    toep_b = toeplitz(jnp.concatenate([rev_lags, no_lag], axis=-1))

    def state_in(pwd, bd):
        m = pwd[:, :, None, :] * bd.transpose(0, 2, 1)[:, None, :, :]
        m = m.reshape(S5_GROUPS, lanes, S5_N)
        return [jnp.real(m), jnp.imag(m)]

    p = jnp.stack(state_in(pw[0, :, S5_T - 1::-1], b_bar[0])
                  + state_in(pw[1, :, :S5_T], b_bar[1]), axis=1)

    def state_out(pwd, cd):
        m = out_weights(pwd, cd)
        return [jnp.real(m), -jnp.imag(m)]

    q = jnp.stack(state_out(pw[0, :, 1:S5_T + 1], cc[0])
                  + state_out(pw[1, :, S5_T:0:-1], cc[1]), axis=1)
    ops = [m.astype(BF16) for m in (toep_f, toep_b, p, q)]
    a_t = pw[:, :, S5_T]
    step = jnp.stack([jnp.real(a_t[0]), jnp.imag(a_t[0]), jnp.real(a_t[1]), jnp.imag(a_t[1])], axis=1)
    return ops, step


S5_GB = 128 // S5_GS
S5_SEG = 128 // S5_GS


def _s5_kernel(x_ref, tf_ref, tb_ref, p_ref, q_ref, step_ref, h0_ref, y_ref, hfin_ref,
               u_scr, y_scr, e_scr, hin_scr, *, nc, nb):
    n_chunks = nb * nc
    for s in range(S5_T):
        xs = x_ref[pl.ds(s, n_chunks, stride=S5_T), :]
        half, pos = divmod(s, S5_SEG)
        dst = slice(half * 128 + pos * S5_GS, half * 128 + (pos + 1) * S5_GS)
        for g in range(S5_GB):
            moved = pltpu.roll(xs, ((pos - g) % S5_SEG) * S5_GS, axis=1)
            u_scr[g, :, dst] = moved[:, pos * S5_GS:(pos + 1) * S5_GS].astype(BF16)

    def group(g, carry):
        u = u_scr[g]
        for m in range(4):
            e_scr[m] = jnp.dot(u, p_ref[g, m], preferred_element_type=F32)
        ar = [step_ref[g, 0:1, :], step_ref[g, 2:3, :]]
        ai = [step_ref[g, 1:2, :], step_ref[g, 3:4, :]]
        hr = [h0_ref[g, 0], h0_ref[g, 2]]
        hi = [h0_ref[g, 1], h0_ref[g, 3]]
        for c in range(nc):
            for d in range(2):
                rows = pl.ds(c if d == 0 else nc - 1 - c, nb, stride=nc)
                hin_scr[2 * d, rows, :] = hr[d]
                hin_scr[2 * d + 1, rows, :] = hi[d]
                hr[d], hi[d] = (hr[d] * ar[d] - hi[d] * ai[d] + e_scr[2 * d, rows, :],
                                hi[d] * ar[d] + hr[d] * ai[d] + e_scr[2 * d + 1, rows, :])
        for d in range(2):
            hfin_ref[g, 2 * d] = hr[d]
            hfin_ref[g, 2 * d + 1] = hi[d]
        y = (jnp.dot(u, tf_ref[g], preferred_element_type=F32)
             + jnp.dot(u, tb_ref[g], preferred_element_type=F32))
        for m in range(4):
            y += jnp.dot(hin_scr[m].astype(BF16), q_ref[g, m], preferred_element_type=F32)
        y_scr[g] = y
        return carry

    lax.fori_loop(0, S5_GB, group, 0)

    segment = lax.broadcasted_iota(jnp.int32, (n_chunks, 128), 1) // S5_GS
    for t in range(S5_T):
        half, pos = divmod(t, S5_SEG)
        rows = jnp.zeros((n_chunks, 128), F32)
        for g in range(S5_GB):
            moved = pltpu.roll(y_scr[g, :, half * 128:(half + 1) * 128], ((g - pos) % S5_SEG) * S5_GS, axis=1)
            rows = jnp.where(segment == g, moved, rows)
        y_ref[pl.ds(t, n_chunks, stride=S5_T), :] = rows


def _s5(su, ops, step, h0, *, nb, seq, first_row_block):
    nc = seq // S5_T
    lanes = S5_T * S5_GS
    n_chunks = nb * nc
    gb = lambda *shape: pl.BlockSpec((S5_GB,) + shape, lambda g: (g,) + (0,) * len(shape))
    return pl.pallas_call(
        functools.partial(_s5_kernel, nc=nc, nb=nb),
        grid=(S5_GROUPS // S5_GB,),
        in_specs=[pl.BlockSpec((nb * seq, S5_GB * S5_GS), lambda g: (first_row_block, g)),
                  gb(lanes, lanes), gb(lanes, lanes), gb(4, lanes, S5_N), gb(4, S5_N, lanes),
                  gb(4, S5_N), gb(4, nb, S5_N)],
        out_specs=[pl.BlockSpec((nb * seq, S5_GB * S5_GS), lambda g: (0, g)), gb(4, nb, S5_N)],
        out_shape=[jax.ShapeDtypeStruct((nb * seq, S5_W), F32),
                   jax.ShapeDtypeStruct((S5_GROUPS, 4, nb, S5_N), F32)],
        scratch_shapes=[pltpu.VMEM((S5_GB, n_chunks, lanes), BF16), pltpu.VMEM((S5_GB, n_chunks, lanes), F32),
                        pltpu.VMEM((4, n_chunks, S5_N), F32), pltpu.VMEM((4, n_chunks, S5_N), F32)],
        compiler_params=_cparams(1), name="s5",
    )(su, *ops, step, h0)


def _glu_kernel(y_ref, su_ref, d_ref, w_ref, o_ref, wbf_ref):
    @pl.when(pl.program_id(0) == 0)
    def _():
        wbf_ref[...] = w_ref[...].astype(BF16)

    z = _gelu_tanh(y_ref[...] + d_ref[...] * su_ref[...])
    gate = jax.nn.sigmoid(jnp.dot(z.astype(BF16), wbf_ref[...], preferred_element_type=F32))
    o_ref[...] = (z * gate).astype(o_ref.dtype)


def _glu(y, su, d, w_glu, layer):
    row = pl.BlockSpec((TM, S5_W), lambda i: (i, 0))
    return pl.pallas_call(
        _glu_kernel,
        grid=(T_ALL // TM,),
        in_specs=[row, row, pl.BlockSpec((None, 1, S5_W), lambda i: (layer, 0, 0)),
                  pl.BlockSpec((None, S5_W, S5_W), lambda i: (layer, 0, 0))],
        out_specs=row,
        out_shape=jax.ShapeDtypeStruct((T_ALL, S5_W), BF16),
        scratch_shapes=[pltpu.VMEM((S5_W, S5_W), BF16)],
        compiler_params=_cparams(1), name="glu",
    )(y, su, d.reshape(DEPTH, 1, S5_W), w_glu)


def _merge_kernel(a_ref, b_ref, c_ref, ga_ref, gb_ref, gc_ref, wa_ref, wb_ref, wc_ref, o_ref, wbf_ref):
    @pl.when(pl.program_id(1) == 0)
    def _():
        wbf_ref[0] = wa_ref[...].astype(BF16)
        wbf_ref[1] = wb_ref[...].astype(BF16)
        wbf_ref[2] = wc_ref[...].astype(BF16)

    acc = ga_ref[...] * jnp.dot(a_ref[...], wbf_ref[0], preferred_element_type=F32)
    acc += gb_ref[...] * jnp.dot(b_ref[...], wbf_ref[1], preferred_element_type=F32)
    acc += gc_ref[...] * jnp.dot(c_ref[...], wbf_ref[2], preferred_element_type=F32)
    o_ref[...] = acc.astype(o_ref.dtype)


def _merge(o_na, o_gla, zz, gates, w_na, w_gla, w_s5, layer):
    tn = 1024
    ncb = D_MODEL // tn
    act = pl.BlockSpec((TM, NA_W), lambda j, i: (i, 0))
    gate = lambda k: pl.BlockSpec((TM, tn), lambda j, i: (i, k * ncb + j))
    w = pl.BlockSpec((None, NA_W, tn), lambda j, i: (layer, 0, j))
    return pl.pallas_call(
        _merge_kernel,
        grid=(ncb, T_ALL // TM),
        in_specs=[act, act, act, gate(0), gate(1), gate(2), w, w, w],
        out_specs=pl.BlockSpec((TM, tn), lambda j, i: (i, j)),
        out_shape=jax.ShapeDtypeStruct((T_ALL, D_MODEL), BF16),
        scratch_shapes=[pltpu.VMEM((3, NA_W, tn), BF16)],
        compiler_params=_cparams(2), name="merge",
    )(o_na, o_gla, zz, gates, gates, gates, w_na, w_gla, w_s5)


def _out_proj_kernel(m_ref, w_ref, x_ref, mod_ref, o_ref, wbf_ref, *, gate_row):
    @pl.when(pl.program_id(1) == 0)
    def _():
        wbf_ref[...] = w_ref[...].astype(BF16)

    y = jnp.dot(m_ref[...], wbf_ref[...], preferred_element_type=F32)
    o_ref[...] = x_ref[...] + mod_ref[gate_row:gate_row + 1, :] * y


def _out_proj(merged, w_out, x, mod, layer):
    tn = 1024
    return pl.pallas_call(
        functools.partial(_out_proj_kernel, gate_row=2),
        grid=(D_MODEL // tn, T_ALL // MM_TM),
        in_specs=[pl.BlockSpec((MM_TM, D_MODEL), lambda j, i: (i, 0)),
                  pl.BlockSpec((None, D_MODEL, tn), lambda j, i: (layer, 0, j)),
                  pl.BlockSpec((MM_TM, tn), lambda j, i: (i, j)),
                  pl.BlockSpec((None, 6, tn), lambda j, i: (_group_of_tile(i, MM_TM), 0, j))],
        out_specs=pl.BlockSpec((MM_TM, tn), lambda j, i: (i, j)),
        out_shape=jax.ShapeDtypeStruct((T_ALL, D_MODEL), F32),
        scratch_shapes=[pltpu.VMEM((D_MODEL, tn), BF16)],
        compiler_params=_cparams(2), name="out_proj",
    )(merged, w_out, x, mod)


def _first_max(v, iota, n):
    m = jnp.max(v, axis=0, keepdims=True)
    first = jnp.min(jnp.where(v == m, iota, float(n)), axis=0, keepdims=True)
    return m, first


def _router_kernel(x_ref, wt_ref, bias_ref, idx_ref, w_ref):
    per = N_EXPERTS // N_GROUPS
    x_hi, x_lo = _split_bf16(x_ref[...])
    w_hi, w_lo = _split_bf16(wt_ref[...])
    logits = (lax.dot_general(w_hi, x_hi, NT, preferred_element_type=F32)
              + lax.dot_general(w_hi, x_lo, NT, preferred_element_type=F32)
              + lax.dot_general(w_lo, x_hi, NT, preferred_element_type=F32))
    scores = jax.nn.sigmoid(logits)
    sel = scores + bias_ref[...]
    tm = sel.shape[1]
    iota_g = lax.broadcasted_iota(jnp.int32, (per, tm), 0).astype(F32)
    grp_rows = []
    for g in range(N_GROUPS):
        v = sel[g * per:(g + 1) * per, :]
        m1, first = _first_max(v, iota_g, per)
        m2 = jnp.max(jnp.where(iota_g == first, -jnp.inf, v), axis=0, keepdims=True)
        grp_rows.append(m1 + m2)
    grp = jnp.concatenate(grp_rows, axis=0)
    iota_n = lax.broadcasted_iota(jnp.int32, (N_GROUPS, tm), 0).astype(F32)
    chosen = jnp.zeros((N_GROUPS, tm), F32)
    for _ in range(TOPK_GROUPS):
        _, first = _first_max(grp, iota_n, N_GROUPS)
        hit = iota_n == first
        chosen = jnp.where(hit, 1.0, chosen)
        grp = jnp.where(hit, -jnp.inf, grp)
    mask = jnp.concatenate([jnp.broadcast_to(chosen[g:g + 1, :], (per, tm)) for g in range(N_GROUPS)], axis=0)
    sel = jnp.where(mask > 0.5, sel, -jnp.inf)
    iota_e = lax.broadcasted_iota(jnp.int32, (N_EXPERTS, tm), 0).astype(F32)
    ids, ws = [], []
    for _ in range(TOP_K):
        _, first = _first_max(sel, iota_e, N_EXPERTS)
        hit = iota_e == first
        ids.append(first)
        ws.append(jnp.sum(jnp.where(hit, scores, 0.0), axis=0, keepdims=True))
        sel = jnp.where(hit, -jnp.inf, sel)
    w = jnp.concatenate(ws, axis=0)
    idx_ref[...] = jnp.concatenate(ids, axis=0).astype(jnp.int32)
    w_ref[...] = w / jnp.sum(w, axis=0, keepdims=True) * ROUTED_SCALE


def _router(h, router_w_t, router_bias):
    tm = 256
    return pl.pallas_call(
        _router_kernel,
        grid=(T_ALL // tm,),
        in_specs=[pl.BlockSpec((tm, D_MODEL), lambda i: (i, 0)),
                  pl.BlockSpec((N_EXPERTS, D_MODEL), lambda i: (0, 0)),
                  pl.BlockSpec((N_EXPERTS, 1), lambda i: (0, 0))],
        out_specs=[pl.BlockSpec((TOP_K, tm), lambda i: (0, i)), pl.BlockSpec((TOP_K, tm), lambda i: (0, i))],
        out_shape=[jax.ShapeDtypeStruct((TOP_K, T_ALL), jnp.int32), jax.ShapeDtypeStruct((TOP_K, T_ALL), F32)],
        compiler_params=_cparams(1), name="router",
    )(h, router_w_t, router_bias.reshape(N_EXPERTS, 1))


def _dispatch(idx):
    n_assign = T_ALL * TOP_K
    flat_e = idx.reshape(-1)
    experts = jnp.arange(N_EXPERTS, dtype=jnp.int32)
    sorted_e, order = lax.sort((flat_e, jnp.arange(n_assign, dtype=jnp.int32)), num_keys=1, is_stable=True)
    counts = jnp.sum((flat_e[:, None] == experts[None, :]).astype(jnp.int32), axis=0)
    cnt_start = jnp.cumsum(counts) - counts
    padded = (counts + MOE_MB - 1) // MOE_MB * MOE_MB
    pad_end = jnp.cumsum(padded)
    pad_start = pad_end - padded
    onehot = (sorted_e[:, None] == experts[None, :]).astype(jnp.int32)
    slot_sorted = jnp.arange(n_assign, dtype=jnp.int32) + jnp.sum(onehot * (pad_start - cnt_start)[None, :], axis=1)
    _, slot_of = lax.sort((order, slot_sorted), num_keys=1)
    blk_first = jnp.arange(MOE_NB, dtype=jnp.int32) * MOE_MB
    block_e = jnp.minimum(jnp.sum((pad_end[None, :] <= blk_first[:, None]).astype(jnp.int32), axis=1),
                          N_EXPERTS - 1)
    blk_onehot = (block_e[:, None] == experts[None, :]).astype(jnp.int32)
    pick = lambda v: jnp.sum(blk_onehot * v[None, :], axis=1)
    src = jnp.clip(pick(cnt_start) + blk_first - pick(pad_start), 0, n_assign)
    sorted_tok = jnp.pad(order // TOP_K, (0, MOE_MB))
    n_used = (pad_end[-1] // MOE_MB).astype(jnp.int32).reshape(1)
    return sorted_tok, slot_of.reshape(T_ALL, TOP_K), block_e.astype(jnp.int32), src.astype(jnp.int32), n_used


MOE_PACKED_W = D_MODEL // 2


def _expert_kernel(be_ref, nu_ref, src_ref, stok_ref, x_hbm, wg_ref, wu_ref, wd_ref, o_ref,
                   xbuf, x_bf, wg_bf, wu_bf, wd_bf, sems):
    i = pl.program_id(0)
    n_used = nu_ref[0]
    last = MOE_NB - 1
    half = MOE_PACKED_W

    def row_copy(blk, slot, j):
        tok = stok_ref[src_ref[blk] + j]
        return pltpu.make_async_copy(x_hbm.at[pl.ds(tok, 1), :], xbuf.at[slot, pl.ds(j, 1), :], sems.at[slot])

    def wait_block(slot):
        pltpu.make_async_copy(x_hbm.at[pl.ds(0, MOE_MB), :], xbuf.at[slot], sems.at[slot]).wait()

    @pl.when(i == 0)
    def _():
        def body(j, carry):
            row_copy(0, 0, j).start()
            return carry
        lax.fori_loop(0, MOE_MB, body, 0, unroll=8)

    def compute(prefetch):
        @pl.when((i == 0) | (be_ref[i] != be_ref[jnp.maximum(i - 1, 0)]))
        def _():
            wg_bf[...] = wg_ref[...].astype(BF16)
            wu_bf[...] = wu_ref[...].astype(BF16)
            wd_bf[...] = wd_ref[...].astype(BF16)

        slot = i % 2
        wait_block(slot)
        x_lo, x_hi = _unpack_bf16_pair(xbuf[slot])
        x_bf[0] = x_lo.astype(BF16)
        x_bf[1] = x_hi.astype(BF16)
        if prefetch:
            for j in range(MOE_MB):
                row_copy(i + 1, 1 - slot, j).start(priority=j % 2)
        up = lambda w: (jnp.dot(x_bf[0], w[:half, :], preferred_element_type=F32)
                        + jnp.dot(x_bf[1], w[half:, :], preferred_element_type=F32))
        a = (_silu(up(wg_bf)) * up(wu_bf)).astype(BF16)
        y = jnp.dot(a, wd_bf[...], preferred_element_type=F32)
        o_ref[...] = _pack_bf16_pair(y[:, :half], y[:, half:])

    @pl.when((i < n_used) & (i < last))
    def _():
        compute(True)

    @pl.when((i < n_used) & (i == last))
    def _():
        compute(False)

    @pl.when(i >= n_used)
    def _():
        o_ref[...] = jnp.zeros_like(o_ref)

        @pl.when(i == n_used)
        def _():
            wait_block(i % 2)


def _experts(x_packed, sorted_tok, block_e, src, n_used, wg, wu, wd, layer):
    w_idx = lambda i, be, nu, src, stok: (layer, be[i], 0, 0)
    w_up = pl.BlockSpec((None, None, D_MODEL, F_EXPERT), w_idx)
    return pl.pallas_call(
        _expert_kernel,
        grid_spec=pltpu.PrefetchScalarGridSpec(
            num_scalar_prefetch=4,
            grid=(MOE_NB,),
            in_specs=[pl.BlockSpec(memory_space=pl.ANY), w_up, w_up,
                      pl.BlockSpec((None, None, F_EXPERT, D_MODEL), w_idx)],
            out_specs=pl.BlockSpec((MOE_MB, MOE_PACKED_W), lambda i, be, nu, src, stok: (i, 0)),
            scratch_shapes=[pltpu.VMEM((2, MOE_MB, MOE_PACKED_W), jnp.uint32),
                            pltpu.VMEM((2, MOE_MB, MOE_PACKED_W), BF16),
                            pltpu.VMEM((D_MODEL, F_EXPERT), BF16), pltpu.VMEM((D_MODEL, F_EXPERT), BF16),
                            pltpu.VMEM((F_EXPERT, D_MODEL), BF16), pltpu.SemaphoreType.DMA((2,))]),
        out_shape=jax.ShapeDtypeStruct((MOE_SLOTS, MOE_PACKED_W), jnp.uint32),
        compiler_params=_cparams(1), name="moe_experts",
    )(block_e, n_used, src, sorted_tok, x_packed, wg, wu, wd)


def _shared_up_kernel(x_ref, wg_ref, wu_ref, o_ref, wg_bf, wu_bf):
    @pl.when(pl.program_id(0) == 0)
    def _():
        wg_bf[...] = wg_ref[...].astype(BF16)
        wu_bf[...] = wu_ref[...].astype(BF16)

    x = x_ref[...].astype(BF16)
    g = jnp.dot(x, wg_bf[...], preferred_element_type=F32)
    u = jnp.dot(x, wu_bf[...], preferred_element_type=F32)
    o_ref[...] = (_silu(g) * u).astype(o_ref.dtype)


def _shared_up(h, wg, wu, layer):
    w = pl.BlockSpec((None, D_MODEL, F_SHARED), lambda i: (layer, 0, 0))
    return pl.pallas_call(
        _shared_up_kernel,
        grid=(T_ALL // TM,),
        in_specs=[pl.BlockSpec((TM, D_MODEL), lambda i: (i, 0)), w, w],
        out_specs=pl.BlockSpec((TM, F_SHARED), lambda i: (i, 0)),
        out_shape=jax.ShapeDtypeStruct((T_ALL, F_SHARED), BF16),
        scratch_shapes=[pltpu.VMEM((D_MODEL, F_SHARED), BF16), pltpu.VMEM((D_MODEL, F_SHARED), BF16)],
        compiler_params=_cparams(1), name="shared_up",
    )(h, wg, wu)


def _combine_kernel(slot_ref, nxt_ref, y_hbm, rw_ref, act_ref, wd_ref, x_ref, mod_ref, o_ref, buf, wd_bf, sems, *,
                    gate_row):
    i = pl.program_id(0)
    n_rows = TOP_K * CMB_TM
    half = MOE_PACKED_W

    def row_copy(idx_ref, slot, j):
        return pltpu.make_async_copy(y_hbm.at[pl.ds(idx_ref[0, 0, j], 1), :], buf.at[slot, pl.ds(j, 1), :],
                                     sems.at[slot])

    @pl.when(i == 0)
    def _():
        wd_bf[...] = wd_ref[...].astype(BF16)

        def body(j, carry):
            row_copy(slot_ref, 0, j).start()
            return carry
        lax.fori_loop(0, n_rows, body, 0, unroll=8)

    def tile(prefetch):
        slot = i % 2
        pltpu.make_async_copy(y_hbm.at[pl.ds(0, n_rows), :], buf.at[slot], sems.at[slot]).wait()
        if prefetch:
            for j in range(n_rows):
                row_copy(nxt_ref, 1 - slot, j).start(priority=j % 2)
        shared = jnp.dot(act_ref[...], wd_bf[...], preferred_element_type=F32)
        lo = shared[:, :half]
        hi = shared[:, half:]
        for k in range(TOP_K):
            row_lo, row_hi = _unpack_bf16_pair(buf[slot, k * CMB_TM:(k + 1) * CMB_TM, :])
            lo = lo + rw_ref[:, k:k + 1] * row_lo
            hi = hi + rw_ref[:, k:k + 1] * row_hi
        gate = mod_ref[gate_row:gate_row + 1, :]
        o_ref[:, :half] = x_ref[:, :half] + gate[:, :half] * lo
        o_ref[:, half:] = x_ref[:, half:] + gate[:, half:] * hi

    @pl.when(i + 1 < pl.num_programs(0))
    def _():
        tile(True)

    @pl.when(i + 1 == pl.num_programs(0))
    def _():
        tile(False)


def _combine(y_sorted, slot_of, route_w, act, sh_wd, x, mod, layer):
    nt = T_ALL // CMB_TM
    slots = slot_of.reshape(nt, CMB_TM, TOP_K).transpose(0, 2, 1).reshape(nt, 1, TOP_K * CMB_TM)
    slot_spec = lambda nxt: pl.BlockSpec((1, 1, TOP_K * CMB_TM), lambda i: (jnp.minimum(i + nxt, nt - 1), 0, 0),
                                         memory_space=pltpu.SMEM)
    return pl.pallas_call(
        functools.partial(_combine_kernel, gate_row=5),
        grid=(nt,),
        in_specs=[slot_spec(0), slot_spec(1),
                  pl.BlockSpec(memory_space=pl.ANY),
                  pl.BlockSpec((CMB_TM, TOP_K), lambda i: (i, 0)),
                  pl.BlockSpec((CMB_TM, F_SHARED), lambda i: (i, 0)),
                  pl.BlockSpec((None, F_SHARED, D_MODEL), lambda i: (layer, 0, 0)),
                  pl.BlockSpec((CMB_TM, D_MODEL), lambda i: (i, 0)),
                  pl.BlockSpec((None, 6, D_MODEL), lambda i: (_group_of_tile(i, CMB_TM), 0, 0))],
        out_specs=pl.BlockSpec((CMB_TM, D_MODEL), lambda i: (i, 0)),
        out_shape=jax.ShapeDtypeStruct((T_ALL, D_MODEL), F32),
        scratch_shapes=[pltpu.VMEM((2, TOP_K * CMB_TM, MOE_PACKED_W), jnp.uint32),
                        pltpu.VMEM((F_SHARED, D_MODEL), BF16), pltpu.SemaphoreType.DMA((2,))],
        compiler_params=_cparams(1), name="moe_combine",
    )(slots, slots, y_sorted, route_w, act, sh_wd, x, mod)


def kernel(x_prompt, x_sample, cache_na_k, cache_na_v, state_gla, state_s5, c, c_ctx, ada_w, ada_b, norm1_g, norm2_g, w_in, na_rpb, gla_wa, gla_ba, gla_norm_g, s5_a_re, s5_a_im, s5_log_dt, s5_b_re, s5_b_im, s5_c_re, s5_c_im, s5_d, s5_w_glu, w_br_na, w_br_gla, w_br_s5, w_merge, w_out, router_w, router_bias, exp_wg, exp_wu, exp_wd, sh_wg, sh_wu, sh_wd, final_norm_g):
    x = jnp.concatenate([x_prompt.reshape(T_CTX, D_MODEL), x_sample.reshape(T_LAT, D_MODEL)], axis=0)
    cvec = jnp.concatenate([c_ctx[None, :], c, jnp.zeros((MOD_ROWS - N_MOD, D_MODEL), F32)], axis=0)
    mods = _ada(cvec, ada_w, ada_b).reshape(DEPTH, MOD_ROWS, 6, D_MODEL)
    cos, sin = _rope_tables()
    w_in_t = jnp.swapaxes(w_in, 1, 2)
    zero_gla = jnp.zeros((BATCH, 2, GLA_HEADS, GLA_DV, GLA_DK), F32)
    zero_s5 = jnp.zeros((S5_GROUPS, 4, BATCH, S5_N), F32)
    new_k, new_v, new_gla, new_s5 = [], [], [], []
    for l in range(DEPTH):
        mod = mods[l]
        h = _norm(x, norm1_g[l], mod, (0, 1), BF16)
        u = _mm(h, w_in_t, l, 1024, n=IN_MAIN, w_rows_are_outputs=True)
        lr = _mm(h, w_in_t[l, IN_MAIN:IN_MAIN + 2 * GLA_LR], None, 2 * GLA_LR, w_rows_are_outputs=True)
        su = _mm(h, w_in_t[l, IN_MAIN + 2 * GLA_LR:], None, S5_W, w_rows_are_outputs=True)
        gates = _mm(h, w_merge, l, 1024, act="sigmoid")

        o_na = jnp.concatenate([_ctx_attn(u),
                                _na_attn(u, cache_na_k, cache_na_v, _na_bias(na_rpb[l]), l)], axis=0)

        s0_lat = state_gla[:, l].transpose(0, 1, 2, 4, 3)
        og_ctx, sfin = _gla(u, lr, gla_wa, gla_ba, gla_norm_g, zero_gla, cos, sin, l, latent=False)
        og_lat, _ = _gla(u, lr, gla_wa, gla_ba, gla_norm_g, s0_lat, cos, sin, l, latent=True)
        o_gla = jnp.concatenate([og_ctx, og_lat], axis=0)

        ops, step = _s5_operators(s5_a_re[l], s5_a_im[l], s5_log_dt[l], s5_b_re[l], s5_b_im[l],
                                  s5_c_re[l], s5_c_im[l])
        st = state_s5[:, l].astype(F32)
        h0_lat = st.transpose(2, 1, 4, 0, 3).reshape(S5_GROUPS, 4, DEC_BATCH, S5_N)
        y_ctx, hfin = _s5(su, ops, step, zero_s5, nb=BATCH, seq=SEQ, first_row_block=0)
        y_lat, _ = _s5(su, ops, step, h0_lat, nb=DEC_BATCH, seq=DEC_SEQ, first_row_block=T_CTX // T_LAT)
        zz = _glu(jnp.concatenate([y_ctx, y_lat], axis=0), su, s5_d, s5_w_glu, l)

        merged = _merge(o_na, o_gla, zz, gates, w_br_na, w_br_gla, w_br_s5, l)
        x = _out_proj(merged, w_out, x, mod, l)

        h2, h2_packed = _norm(x, norm2_g[l], mod, (3, 4), F32, packed=True)
        idx_t, w_t = _router(h2, router_w[l].T, router_bias[l])
        sorted_tok, slot_of, block_e, src, n_used = _dispatch(idx_t.T)
        y_sorted = _experts(h2_packed, sorted_tok, block_e, src, n_used, exp_wg, exp_wu, exp_wd, l)
        act = _shared_up(h2, sh_wg, sh_wu, l)
        x = _combine(y_sorted, slot_of, w_t.T, act, sh_wd, x, mod, l)

        new_k.append(u[:T_CTX, NA_W:2 * NA_W].reshape(BATCH, SEQ, NA_HEADS, NA_DH))
        new_v.append(u[:T_CTX, 2 * NA_W:3 * NA_W].reshape(BATCH, SEQ, NA_HEADS, NA_DH))
        new_gla.append(sfin.transpose(0, 1, 2, 4, 3))
        new_s5.append(hfin.reshape(S5_GROUPS, 2, 2, BATCH, S5_N).transpose(3, 1, 0, 4, 2))

    y = _norm(x, final_norm_g, None, None, F32)
    return (y[:T_CTX].reshape(BATCH, SEQ, D_MODEL), y[T_CTX:].reshape(DEC_BATCH, DEC_SEQ, D_MODEL),
            jnp.stack(new_k, axis=1), jnp.stack(new_v, axis=1),
            jnp.stack(new_gla, axis=1), jnp.stack(new_s5, axis=1))
```
